```python
import math
import jax, jax.numpy as jnp
from jax import lax
import numpy as np

D_MODEL = 2048
BATCH = 32
SEQ = 256
DEPTH = 2
DEC_BATCH = 2
DEC_SEQ = 2048
PAST_LEN = 256

GRID_W = 64
N_ATT_HEADS = D_MODEL // 256
D_QK = 64
D_V = 2 * D_QK
W_ATT = N_ATT_HEADS * D_V
ROPE_THETA = 10000.0
ROPE_FREQS = D_QK // 4
Q_BLOCK = 128
W_LRU = D_MODEL // 2
N_LRU_BLOCKS = 16
LRU_BS = W_LRU // N_LRU_BLOCKS
LRU_C = 8.0
LRU_CONV = 4
W_HY = D_MODEL // 2
HY_ORDER = 2
HY_CONV = 3
HY_BANDS = 8
HY_EMB = 1 + 2 * HY_BANDS
HY_HIDDEN = 64
HY_MIN_DECAY = 3.07
HY_MAX_DECAY = 15.35
N_BRANCH = 3
W_BR = W_ATT
D_FF = ((8 * D_MODEL // 3 + 255) // 256) * 256
W_IN = 3 * W_ATT + 2 * W_LRU + (HY_ORDER + 1) * W_HY
NORM_EPS = 1e-6
SUBLN_EPS = 1e-5

kernel_name = 'hybrid_diffattn_rglru_hyena_dit_step'


def rmsnorm(x, g, eps=NORM_EPS):
    xf = x.astype(jnp.float32)
    y = xf * lax.rsqrt(jnp.mean(xf * xf, axis=-1, keepdims=True) + eps)
    return (y * g.astype(jnp.float32)).astype(x.dtype)


def dwconv(x, w, b, pad_left):
    K = w.shape[0]
    L = x.shape[1]
    xp = jnp.pad(x, ((0, 0), (pad_left, K - 1 - pad_left), (0, 0)))
    y = b + w[0] * xp[:, 0:L]
    for k in range(1, K):
        y = y + w[k] * xp[:, k:k + L]
    return y


def axial_rope(L):
    n_rows = L // GRID_W
    row = jnp.repeat(jnp.arange(n_rows), GRID_W)
    col = jnp.tile(jnp.arange(GRID_W), n_rows)
    inv_freq = ROPE_THETA ** (-jnp.arange(ROPE_FREQS, dtype=jnp.float32) / ROPE_FREQS)
    pos = jnp.stack([row, col], axis=-1).astype(jnp.float32)
    ang = pos[:, :, None] * inv_freq
    return jnp.cos(ang), jnp.sin(ang)


def apply_rope(x, cos, sin):
    B, L, H, _ = x.shape
    xr = x.astype(jnp.float32).reshape(B, L, H, 2, 2, ROPE_FREQS)
    x1, x2 = xr[..., 0, :], xr[..., 1, :]
    c = cos[None, :, None]
    s = sin[None, :, None]
    out = jnp.stack([x1 * c - x2 * s, x2 * c + x1 * s], axis=-2)
    return out.reshape(B, L, H, D_QK).astype(x.dtype)


def diff_attention(q1, q2, k1, k2, v, lam):
    B, Lq, H, _ = q1.shape
    nb = Lq // Q_BLOCK
    scale = D_QK ** -0.5

    def blocks(q):
        return q.reshape(B, nb, Q_BLOCK, H, D_QK).transpose(1, 0, 2, 3, 4)

    def one_block(qb):
        qb1, qb2 = qb
        s1 = jnp.einsum('bqhd,bkhd->bhqk', qb1, k1).astype(jnp.float32) * scale
        s2 = jnp.einsum('bqhd,bkhd->bhqk', qb2, k2).astype(jnp.float32) * scale
        w = jax.nn.softmax(s1, axis=-1) - lam * jax.nn.softmax(s2, axis=-1)
        return jnp.einsum('bhqk,bkhe->bqhe', w.astype(v.dtype), v)

    out = lax.map(one_block, (blocks(q1), blocks(q2)))
    return out.transpose(1, 0, 2, 3, 4).reshape(B, Lq, H, D_V)


def _lin_combine(e1, e2):
    a1, b1 = e1
    a2, b2 = e2
    return a1 * a2, a2 * b1 + b2


def rglru_scan(x, wa, ba, wx, bx, lam, h0, reverse):
    B, L, W = x.shape
    f32 = jnp.float32
    xf = x.astype(f32)
    xb = xf.reshape(B, L, N_LRU_BLOCKS, LRU_BS)
    r = jax.nn.sigmoid(jnp.einsum('blnd,nde->blne', xb, wa.astype(f32)).reshape(B, L, W) + ba.astype(f32))
    i = jax.nn.sigmoid(jnp.einsum('blnd,nde->blne', xb, wx.astype(f32)).reshape(B, L, W) + bx.astype(f32))
    log_a = -LRU_C * r * jax.nn.softplus(-lam.astype(f32))
    a = jnp.exp(log_a)
    b = jnp.sqrt(-jnp.expm1(2.0 * log_a)) * (i * xf)
    if reverse:
        a, b = a[:, ::-1], b[:, ::-1]
    a_cum, b_cum = lax.associative_scan(_lin_combine, (a, b), axis=1)
    h = a_cum * h0.astype(f32)[:, None] + b_cum
    h_last = h[:, -1]
    if reverse:
        h = h[:, ::-1]
    return h, h_last


def hyena_filters(L, lp):
    f32 = jnp.float32
    t = jnp.linspace(0.0, 1.0, L, dtype=f32)[:, None]
    w = 2.0 * math.pi * jnp.arange(L, dtype=f32)[:, None] / L
    f = jnp.linspace(1e-4, HY_BANDS - 1, HY_BANDS, dtype=f32)[None]
    z = jnp.concatenate([t, jnp.cos(f * w), -jnp.sin(f * w)], axis=-1)
    freq = lp['hy_freq'].astype(f32)
    hdn = jnp.sin(freq * (z @ lp['hy_w1'].astype(f32) + lp['hy_b1'].astype(f32)))
    hdn = jnp.sin(freq * (hdn @ lp['hy_w2'].astype(f32) + lp['hy_b2'].astype(f32)))
    filt = (hdn @ lp['hy_w3'].astype(f32) + lp['hy_b3'].astype(f32)).reshape(L, HY_ORDER, 2, W_HY).transpose(1, 2, 0, 3)
    window = jnp.exp(-t[None, None] * jnp.abs(lp['hy_decay'].astype(f32))[:, :, None, :])
    return filt * window


def fft_bidir_conv(u, h_fwd, h_bwd):
    B, L, W = u.shape
    f = jnp.concatenate([h_fwd, jnp.zeros((1, W), jnp.float32), h_bwd[:0:-1]], axis=0)
    U = jnp.fft.rfft(u, n=2 * L, axis=1)
    F = jnp.fft.rfft(f, n=2 * L, axis=0)
    return jnp.fft.irfft(U * F[None], n=2 * L, axis=1)[:, :L]


def trunk_layer(x, cvec, lp, lam_init, rope, ctx_kv, lru_h0):
    f32 = jnp.float32
    B, L, _ = x.shape
    mod = (jax.nn.silu(cvec) @ lp['w_mod'] + lp['b_mod'])[:, None, :]
    sh1, sc1, g1, sh2, sc2, g2 = jnp.split(mod, 6, axis=-1)
    h = rmsnorm(x, lp['norm_mix']) * (1.0 + sc1) + sh1
    proj = h @ lp['w_in']
    q, k, v, xl, gl, hy = jnp.split(proj, (W_ATT, 2 * W_ATT, 3 * W_ATT, 3 * W_ATT + W_LRU, 3 * W_ATT + 2 * W_LRU), axis=-1)

    q = q.reshape(B, L, N_ATT_HEADS, 2, D_QK)
    k = k.reshape(B, L, N_ATT_HEADS, 2, D_QK)
    v = v.reshape(B, L, N_ATT_HEADS, D_V)
    q1, q2, k1, k2 = q[..., 0, :], q[..., 1, :], k[..., 0, :], k[..., 1, :]
    k_self = k.reshape(B, L, N_ATT_HEADS, 2 * D_QK)
    if rope is not None:
        cos, sin = rope
        q1 = apply_rope(q1, cos, sin)
        q2 = apply_rope(q2, cos, sin)
        k1 = apply_rope(k1, cos, sin)
        k2 = apply_rope(k2, cos, sin)
    if ctx_kv is not None:
        ck, cv = ctx_kv
        k1 = jnp.concatenate([k1, ck[..., :D_QK]], axis=1)
        k2 = jnp.concatenate([k2, ck[..., D_QK:]], axis=1)
        v_all = jnp.concatenate([v, cv], axis=1)
    else:
        v_all = v
    lam_p = lp['att_lambda'].astype(f32)
    lam = jnp.exp(jnp.sum(lam_p[0] * lam_p[1])) - jnp.exp(jnp.sum(lam_p[2] * lam_p[3])) + lam_init
    att = diff_attention(q1, q2, k1, k2, v_all, lam)
    att = (rmsnorm(att, lp['att_subln'], SUBLN_EPS) * (1.0 - lam_init)).reshape(B, L, W_ATT)

    xl = dwconv(xl, lp['lru_conv_w'], lp['lru_conv_b'], LRU_CONV // 2)
    h_f, s_f = rglru_scan(xl, lp['lru_wa'][0], lp['lru_ba'][0], lp['lru_wx'][0], lp['lru_bx'][0], lp['lru_lambda'][0], lru_h0[:, 0], False)
    h_b, s_b = rglru_scan(xl, lp['lru_wa'][1], lp['lru_ba'][1], lp['lru_wx'][1], lp['lru_bx'][1], lp['lru_lambda'][1], lru_h0[:, 1], True)
    lru = ((h_f + h_b) * jax.nn.gelu(gl.astype(f32))).astype(x.dtype)

    hy = dwconv(hy, lp['hy_conv_w'], lp['hy_conv_b'], HY_CONV // 2)
    filt = hyena_filters(L, lp)
    z = hy[..., :W_HY].astype(f32)
    for o in range(HY_ORDER):
        gate = hy[..., (o + 1) * W_HY:(o + 2) * W_HY].astype(f32)
        z = gate * (fft_bidir_conv(z, filt[o, 0], filt[o, 1]) + lp['hy_d'][o].astype(f32) * z)
    hyo = z.astype(x.dtype)

    branches = jnp.stack([att.astype(x.dtype), lru, hyo], axis=2)
    br = jnp.einsum('blnw,nwd->blnd', branches, lp['w_br'])
    gates = jax.nn.sigmoid(h @ lp['w_gate']).reshape(B, L, N_BRANCH, D_MODEL)
    mixed = jnp.sum(gates * br, axis=2) @ lp['w_out']
    x = x + g1 * mixed

    h2 = rmsnorm(x, lp['norm_ffn']) * (1.0 + sc2) + sh2
    ffn = (jax.nn.silu(h2 @ lp['w_ff_gate']) * (h2 @ lp['w_ff_up'])) @ lp['w_ff_down']
    x = x + g2 * ffn
    return x, k_self, v, jnp.stack([s_f, s_b], axis=1).astype(x.dtype)


def setup_inputs(seed: int = 0) -> dict:
    key = jax.random.key(seed)
    ks = iter(jax.random.split(key, 48))
    f32 = jnp.float32
    D = D_MODEL

    def nrm(shape, scale):
        return jax.random.normal(next(ks), shape, f32) * scale

    def gain(shape):
        return 1.0 + nrm(shape, 0.02)

    a_lru = jax.random.uniform(next(ks), (DEPTH, 2, W_LRU), f32, 0.9, 0.999)
    hy_decay = jax.random.uniform(next(ks), (DEPTH, HY_ORDER, 2, W_HY), f32, HY_MIN_DECAY, HY_MAX_DECAY)
    return {
        'x_prompt': nrm((BATCH, SEQ, D), 1.0),
        'x_sample': nrm((DEC_BATCH, DEC_SEQ, D), 1.0),
        'c': nrm((DEC_BATCH, D), 1.0),
        'cache_k': nrm((DEC_BATCH, DEPTH, PAST_LEN, N_ATT_HEADS, 2 * D_QK), 1.0),
        'cache_v': nrm((DEC_BATCH, DEPTH, PAST_LEN, N_ATT_HEADS, D_V), 1.0),
        'state_lru': nrm((DEC_BATCH, DEPTH, 2, W_LRU), 0.5),
        'c_ctx': nrm((D,), 1.0),
        'w_mod': nrm((DEPTH, D, 6 * D), 0.5 * D ** -0.5),
        'b_mod': nrm((DEPTH, 6 * D), 0.02),
        'norm_mix': gain((DEPTH, D)),
        'norm_ffn': gain((DEPTH, D)),
        'w_in': nrm((DEPTH, D, W_IN), D ** -0.5),
        'w_gate': nrm((DEPTH, D, N_BRANCH * D), D ** -0.5),
        'att_lambda': nrm((DEPTH, 4, D_QK), 0.1),
        'att_subln': gain((DEPTH, D_V)),
        'lru_conv_w': nrm((DEPTH, LRU_CONV, W_LRU), LRU_CONV ** -0.5),
        'lru_conv_b': nrm((DEPTH, W_LRU), 0.02),
        'lru_wa': nrm((DEPTH, 2, N_LRU_BLOCKS, LRU_BS, LRU_BS), LRU_BS ** -0.5),
        'lru_ba': nrm((DEPTH, 2, W_LRU), 0.02),
        'lru_wx': nrm((DEPTH, 2, N_LRU_BLOCKS, LRU_BS, LRU_BS), LRU_BS ** -0.5),
        'lru_bx': nrm((DEPTH, 2, W_LRU), 0.02),
        'lru_lambda': jnp.log(a_lru) - jnp.log1p(-a_lru),
        'hy_conv_w': nrm((DEPTH, HY_CONV, (HY_ORDER + 1) * W_HY), HY_CONV ** -0.5),
        'hy_conv_b': nrm((DEPTH, (HY_ORDER + 1) * W_HY), 0.02),
        'hy_w1': nrm((DEPTH, HY_EMB, HY_HIDDEN), HY_EMB ** -0.5),
        'hy_b1': nrm((DEPTH, HY_HIDDEN), 0.02),
        'hy_w2': nrm((DEPTH, HY_HIDDEN, HY_HIDDEN), HY_HIDDEN ** -0.5),
        'hy_b2': nrm((DEPTH, HY_HIDDEN), 0.02),
        'hy_w3': nrm((DEPTH, HY_HIDDEN, HY_ORDER * 2 * W_HY), 0.01),
        'hy_b3': nrm((DEPTH, HY_ORDER * 2 * W_HY), 0.002),
        'hy_freq': gain((DEPTH, HY_HIDDEN)),
        'hy_decay': hy_decay,
        'hy_d': nrm((DEPTH, HY_ORDER, W_HY), 1.0),
        'w_br': nrm((DEPTH, N_BRANCH, W_BR, D), W_BR ** -0.5),
        'w_out': nrm((DEPTH, D, D), D ** -0.5),
        'w_ff_gate': nrm((DEPTH, D, D_FF), D ** -0.5),
        'w_ff_up': nrm((DEPTH, D, D_FF), D ** -0.5),
        'w_ff_down': nrm((DEPTH, D_FF, D), D_FF ** -0.5),
        'final_norm': gain((D,)),
    }


def reference(x_prompt, x_sample, c, cache_k, cache_v, state_lru, c_ctx, w_mod, b_mod, norm_mix, norm_ffn, w_in, w_gate, att_lambda, att_subln, lru_conv_w, lru_conv_b, lru_wa, lru_ba, lru_wx, lru_bx, lru_lambda, hy_conv_w, hy_conv_b, hy_w1, hy_b1, hy_w2, hy_b2, hy_w3, hy_b3, hy_freq, hy_decay, hy_d, w_br, w_out, w_ff_gate, w_ff_up, w_ff_down, final_norm):
    weights = dict(w_mod=w_mod, b_mod=b_mod, norm_mix=norm_mix, norm_ffn=norm_ffn, w_in=w_in, w_gate=w_gate,
                   att_lambda=att_lambda, att_subln=att_subln, lru_conv_w=lru_conv_w, lru_conv_b=lru_conv_b,
                   lru_wa=lru_wa, lru_ba=lru_ba, lru_wx=lru_wx, lru_bx=lru_bx, lru_lambda=lru_lambda,
                   hy_conv_w=hy_conv_w, hy_conv_b=hy_conv_b, hy_w1=hy_w1, hy_b1=hy_b1, hy_w2=hy_w2, hy_b2=hy_b2,
                   hy_w3=hy_w3, hy_b3=hy_b3, hy_freq=hy_freq, hy_decay=hy_decay, hy_d=hy_d, w_br=w_br, w_out=w_out,
                   w_ff_gate=w_ff_gate, w_ff_up=w_ff_up, w_ff_down=w_ff_down)

    xp = x_prompt
    h0_ctx = jnp.zeros((x_prompt.shape[0], 2, W_LRU), x_prompt.dtype)
    ks, vs, ss = [], [], []
    for l in range(DEPTH):
        lp = {name: arr[l] for name, arr in weights.items()}
        lam_init = 0.8 - 0.6 * math.exp(-0.3 * l)
        xp, k_l, v_l, s_l = trunk_layer(xp, c_ctx[None], lp, lam_init, None, None, h0_ctx)
        ks.append(k_l)
        vs.append(v_l)
        ss.append(s_l)
    y_prompt = rmsnorm(xp, final_norm)
    new_cache_k = jnp.stack(ks, axis=1)
    new_cache_v = jnp.stack(vs, axis=1)
    new_state_lru = jnp.stack(ss, axis=1)

    xs = x_sample
    rope = axial_rope(x_sample.shape[1])
    for l in range(DEPTH):
        lp = {name: arr[l] for name, arr in weights.items()}
        lam_init = 0.8 - 0.6 * math.exp(-0.3 * l)
        xs, _, _, _ = trunk_layer(xs, c, lp, lam_init, rope, (cache_k[:, l], cache_v[:, l]), state_lru[:, l])
    y_sample = rmsnorm(xs, final_norm)
    return (y_prompt, y_sample, new_cache_k, new_cache_v, new_state_lru)
```

```python
import functools
import math

import jax
import jax.numpy as jnp
from jax import lax
from jax.experimental import pallas as pl
from jax.experimental.pallas import tpu as pltpu

F32 = jnp.float32
BF16 = jnp.bfloat16

GRID_W = 64
ROPE_THETA = 10000.0
NORM_EPS = 1e-6
SUBLN_EPS = 1e-5
LRU_C = 8.0
HY_BANDS = 8
LANES = 128
SUBLANES = 8
VMEM_LIMIT_BYTES = 56 * 1024 * 1024


def _pick(n, prefs):
    for p in prefs:
        if n % p == 0:
            return p
    raise ValueError(f"no tile in {prefs} divides {n}")


def _params(sem):
    return pltpu.CompilerParams(dimension_semantics=sem, vmem_limit_bytes=VMEM_LIMIT_BYTES)


def _mod_row(i, tm, m_ctx, l_s):
    n_ctx = m_ctx // tm
    return jnp.where(i < n_ctx, 0, 1 + (i - n_ctx) // (l_s // tm))


def _rms_mod(x, g, sc, sh):
    y = x * lax.rsqrt(jnp.mean(x * x, axis=-1, keepdims=True) + NORM_EPS)
    return (y * g) * (1.0 + sc) + sh


def _mod_kernel(c_ref, w_ref, b_ref, o_ref):
    c = c_ref[...]
    s = (c * jax.nn.sigmoid(c)).astype(BF16)
    o_ref[...] = jnp.dot(s, w_ref[...].astype(BF16), preferred_element_type=F32) + b_ref[...]


def _modulation(c8, w_mod, b_mod):
    depth, d, n = w_mod.shape
    tn = _pick(n, (1024, 512, 256, 128))
    return pl.pallas_call(
        _mod_kernel,
        grid=(depth, n // tn),
        in_specs=[
            pl.BlockSpec((SUBLANES, d), lambda l, j: (0, 0)),
            pl.BlockSpec((None, d, tn), lambda l, j: (l, 0, j)),
            pl.BlockSpec((None, 1, tn), lambda l, j: (l, 0, j)),
        ],
        out_specs=pl.BlockSpec((None, SUBLANES, tn), lambda l, j: (l, 0, j)),
        out_shape=jax.ShapeDtypeStruct((depth, SUBLANES, n), F32),
        compiler_params=_params(("arbitrary", "arbitrary")),
        name="modulation",
    )(c8, w_mod, b_mod.reshape(depth, 1, n))


def _norm_proj_kernel(x_ref, mod_ref, g_ref, w_ref, o_ref, h_scr, *, sh_row, sc_row):
    @pl.when(pl.program_id(1) == 0)
    def _():
        h = _rms_mod(x_ref[...], g_ref[...], mod_ref[sc_row:sc_row + 1, :], mod_ref[sh_row:sh_row + 1, :])
        h_scr[...] = h.astype(BF16)

    o_ref[...] = jnp.dot(h_scr[...], w_ref[...], preferred_element_type=F32)


def _norm_proj(x, mod, g, w, layer, m_ctx, l_s):
    m, d = x.shape
    n = w.shape[-1]
    tm = _pick(math.gcd(m_ctx, l_s), (512, 256, 128))
    tn = _pick(n, (1024, 512, 256, 128))
    return pl.pallas_call(
        functools.partial(_norm_proj_kernel, sh_row=0, sc_row=1),
        grid=(m // tm, n // tn),
        in_specs=[
            pl.BlockSpec((tm, d), lambda i, j: (i, 0)),
            pl.BlockSpec((None, 6, d), lambda i, j: (_mod_row(i, tm, m_ctx, l_s), 0, 0)),
            pl.BlockSpec((None, 1, d), lambda i, j: (layer, 0, 0)),
            pl.BlockSpec((None, d, tn), lambda i, j: (layer, 0, j)),
        ],
        out_specs=pl.BlockSpec((tm, tn), lambda i, j: (i, j)),
        out_shape=jax.ShapeDtypeStruct((m, n), F32),
        scratch_shapes=[pltpu.VMEM((tm, d), BF16)],
        compiler_params=_params(("arbitrary", "arbitrary")),
        name="norm_proj",
    )(x, mod, g, w)


def _rope(x, cos, sin_signed):
    n = x.shape[-1]
    half = 16
    lane = lax.broadcasted_iota(jnp.int32, x.shape, 1)
    swapped = jnp.where((lane % (2 * half)) < half, pltpu.roll(x, n - half, 1), pltpu.roll(x, half, 1))
    return x * cos + swapped * sin_signed


def _attn_kernel(*refs, lam_init, d_qk, use_rope, use_cache):
    it = iter(refs)
    lam_ref, sub_ref, q_ref, k_ref, v_ref = next(it), next(it), next(it), next(it), next(it)
    if use_rope:
        cq_ref, sq_ref, ck_ref, sk_ref = next(it), next(it), next(it), next(it)
    if use_cache:
        pk_ref, pv_ref = next(it), next(it)
    o_ref = next(it)
    k_scr = next(it)

    @pl.when(pl.program_id(2) == 0)
    def _():
        k = k_ref[...]
        if use_rope:
            k = _rope(k, ck_ref[...], sk_ref[...])
        k_scr[...] = k.astype(BF16)

    a = lam_ref[...]
    lam = (jnp.exp(jnp.sum(a[0:1] * a[1:2], axis=-1, keepdims=True))
           - jnp.exp(jnp.sum(a[2:3] * a[3:4], axis=-1, keepdims=True)) + lam_init)

    q = q_ref[...]
    if use_rope:
        q = _rope(q, cq_ref[...], sq_ref[...])
    lane = lax.broadcasted_iota(jnp.int32, q.shape, 1)
    q1 = jnp.where(lane < d_qk, q, 0.0).astype(BF16)
    q2 = jnp.where(lane >= d_qk, q, 0.0).astype(BF16)
    scale = d_qk ** -0.5
    nt = (((1,), (1,)), ((), ()))
    keys = [k_scr[...]]
    vals = [v_ref[...].astype(BF16)]
    if use_cache:
        keys.append(pk_ref[...].astype(BF16))
        vals.append(pv_ref[...].astype(BF16))

    def probs(qh):
        s = [lax.dot_general(qh, kk, nt, preferred_element_type=F32) * scale for kk in keys]
        mx = functools.reduce(jnp.maximum, [jnp.max(x, axis=-1, keepdims=True) for x in s])
        p = [jnp.exp(x - mx) for x in s]
        den = functools.reduce(jnp.add, [jnp.sum(x, axis=-1, keepdims=True) for x in p])
        return p, 1.0 / den

    p1, r1 = probs(q1)
    p2, r2 = probs(q2)
    r2 = lam * r2
    o = None
    for a1, a2, vv in zip(p1, p2, vals):
        w = (a1 * r1 - a2 * r2).astype(BF16)
        t = jnp.dot(w, vv, preferred_element_type=F32)
        o = t if o is None else o + t
    y = o * lax.rsqrt(jnp.mean(o * o, axis=-1, keepdims=True) + SUBLN_EPS)
    o_ref[...] = ((y * sub_ref[...]) * (1.0 - lam_init)).astype(o_ref.dtype)


def _attention(proj, tok0, b, l, h, d_v, lam_init, att_lambda, att_subln, layer, rope=None, cache=None):
    assert d_v == LANES
    tq = _pick(l, (256, 128, 64))
    assert tok0 % l == 0 and tok0 % tq == 0
    nq = l // tq
    q_map = lambda bi, hi, qi: (tok0 // tq + bi * nq + qi, hi)
    in_specs = [
        pl.BlockSpec((None, 4, att_lambda.shape[-1]), lambda bi, hi, qi: (layer, 0, 0)),
        pl.BlockSpec((None, 1, d_v), lambda bi, hi, qi: (layer, 0, 0)),
        pl.BlockSpec((tq, d_v), q_map),
        pl.BlockSpec((l, d_v), lambda bi, hi, qi: (tok0 // l + bi, h + hi)),
        pl.BlockSpec((l, d_v), lambda bi, hi, qi: (tok0 // l + bi, 2 * h + hi)),
    ]
    args = [att_lambda, att_subln, proj, proj, proj]
    if rope is not None:
        cos, sin = rope
        in_specs += [
            pl.BlockSpec((tq, d_v), lambda bi, hi, qi: (qi, 0)),
            pl.BlockSpec((tq, d_v), lambda bi, hi, qi: (qi, 0)),
            pl.BlockSpec((l, d_v), lambda bi, hi, qi: (0, 0)),
            pl.BlockSpec((l, d_v), lambda bi, hi, qi: (0, 0)),
        ]
        args += [cos, sin, cos, sin]
    if cache is not None:
        ck, cv = cache
        past = ck.shape[1]
        in_specs += [
            pl.BlockSpec((None, past, d_v), lambda bi, hi, qi: (bi, 0, hi)),
            pl.BlockSpec((None, past, d_v), lambda bi, hi, qi: (bi, 0, hi)),
        ]
        args += [ck, cv]
    return pl.pallas_call(
        functools.partial(_attn_kernel, lam_init=lam_init, d_qk=d_v // 2,
                          use_rope=rope is not None, use_cache=cache is not None),
        grid=(b, h, nq),
        in_specs=in_specs,
        out_specs=pl.BlockSpec((tq, d_v), lambda bi, hi, qi: (bi * nq + qi, hi)),
        out_shape=jax.ShapeDtypeStruct((b * l, h * d_v), BF16),
        scratch_shapes=[pltpu.VMEM((l, d_v), BF16)],
        compiler_params=_params(("arbitrary", "arbitrary", "arbitrary")),
        name="diff_attention",
    )(*args)


def _rope_tables(l, d_qk):
    n_freq = d_qk // 4
    lane = jnp.arange(2 * d_qk)
    axis = (lane % d_qk) // (2 * n_freq)
    freq = lane % n_freq
    first = (lane % (2 * n_freq)) < n_freq
    inv_freq = ROPE_THETA ** (-freq.astype(F32) / n_freq)
    t = jnp.arange(l)
    pos = jnp.where(axis[None, :] == 0, (t // GRID_W)[:, None], (t % GRID_W)[:, None]).astype(F32)
    ang = pos * inv_freq[None, :]
    return jnp.cos(ang), jnp.where(first[None, :], -jnp.sin(ang), jnp.sin(ang))


def _shift_rows(x, d):
    n = x.shape[0]
    if d == 0:
        return x
    row = lax.broadcasted_iota(jnp.int32, x.shape, 0)
    rolled = pltpu.roll(x, (-d) % n, 0)
    ok = (row + d >= 0) & (row + d < n)
    return jnp.where(ok, rolled, 0.0)


def _dwconv(x, w, bias, pad_left):
    y = bias + w[0:1] * _shift_rows(x, -pad_left)
    for k in range(1, w.shape[0]):
        y = y + w[k:k + 1] * _shift_rows(x, k - pad_left)
    return y


def _lru_kernel(xl_ref, gl_ref, cw_ref, cb_ref, wg_ref, bg_ref, lam_ref, h0_ref, o_ref, st_ref,
                a_scr, b_scr, hf_scr, *, pad_left):
    l, cb = xl_ref.shape
    n_grp = cb // LANES
    xc = _dwconv(xl_ref[...], cw_ref[...], cb_ref[...], pad_left)
    lam = lam_ref[...]
    neg = -lam
    softplus = jnp.maximum(neg, 0.0) + jnp.log1p(jnp.exp(-jnp.abs(neg)))
    row8 = lax.broadcasted_iota(jnp.int32, (SUBLANES, cb), 0)
    n_chunks = l // SUBLANES

    def fill(direction):
        for g in range(n_grp):
            sl = slice(g * LANES, (g + 1) * LANES)
            xg = xc[:, sl]
            y = jnp.dot(xg.astype(BF16), wg_ref[g, :, 2 * direction * LANES:(2 * direction + 2) * LANES],
                        preferred_element_type=F32)
            r = jax.nn.sigmoid(y[:, :LANES] + bg_ref[2 * direction:2 * direction + 1, sl])
            i = jax.nn.sigmoid(y[:, LANES:] + bg_ref[2 * direction + 1:2 * direction + 2, sl])
            log_a = (-LRU_C * r) * softplus[direction:direction + 1, sl]
            th = jnp.tanh(log_a)
            one_minus_a2 = (-2.0 * th) / (1.0 - th)
            a_scr[:, sl] = jnp.exp(log_a)
            b_scr[:, sl] = jnp.sqrt(one_minus_a2) * (i * xg)

    fill(0)

    def fwd(c, h):
        r0 = pl.multiple_of(c * SUBLANES, SUBLANES)
        a = a_scr[pl.ds(r0, SUBLANES), :]
        b = b_scr[pl.ds(r0, SUBLANES), :]
        for s in (1, 2, 4):
            keep = row8 >= s
            b = jnp.where(keep, a * pltpu.roll(b, s, 0) + b, b)
            a = jnp.where(keep, a * pltpu.roll(a, s, 0), a)
        h8 = a * h + b
        hf_scr[pl.ds(r0, SUBLANES), :] = h8
        return h8[SUBLANES - 1:SUBLANES, :]

    s_f = lax.fori_loop(0, n_chunks, fwd, h0_ref[0:1, :])

    fill(1)

    def bwd(c, h):
        r0 = pl.multiple_of((n_chunks - 1 - c) * SUBLANES, SUBLANES)
        a = a_scr[pl.ds(r0, SUBLANES), :]
        b = b_scr[pl.ds(r0, SUBLANES), :]
        for s in (1, 2, 4):
            keep = row8 < SUBLANES - s
            b = jnp.where(keep, a * pltpu.roll(b, SUBLANES - s, 0) + b, b)
            a = jnp.where(keep, a * pltpu.roll(a, SUBLANES - s, 0), a)
        h8 = a * h + b
        b_scr[pl.ds(r0, SUBLANES), :] = h8
        return h8[0:1, :]

    s_b = lax.fori_loop(0, n_chunks, bwd, h0_ref[1:2, :])

    st_ref[0:1, :] = s_f
    st_ref[1:2, :] = s_b
    o_ref[...] = ((hf_scr[...] + b_scr[...]) * jax.nn.gelu(gl_ref[...])).astype(o_ref.dtype)


def _lru(proj, tok0, b, l, col_x, col_g, w_lru, conv_w, conv_b, wg, bg, lam, h0, layer):
    cb = _pick(w_lru, (256, 128))
    assert tok0 % l == 0 and col_x % cb == 0 and col_g % cb == 0
    k = conv_w.shape[1]
    n_grp = cb // LANES
    out, st = pl.pallas_call(
        functools.partial(_lru_kernel, pad_left=k // 2),
        grid=(b, w_lru // cb),
        in_specs=[
            pl.BlockSpec((l, cb), lambda bi, j: (tok0 // l + bi, col_x // cb + j)),
            pl.BlockSpec((l, cb), lambda bi, j: (tok0 // l + bi, col_g // cb + j)),
            pl.BlockSpec((None, k, cb), lambda bi, j: (layer, 0, j)),
            pl.BlockSpec((None, 1, cb), lambda bi, j: (layer, 0, j)),
            pl.BlockSpec((None, n_grp, LANES, 4 * LANES), lambda bi, j: (layer, j, 0, 0)),
            pl.BlockSpec((None, 4, cb), lambda bi, j: (layer, 0, j)),
            pl.BlockSpec((None, 2, cb), lambda bi, j: (layer, 0, j)),
            pl.BlockSpec((None, 2, cb), lambda bi, j: (bi, 0, j)),
        ],
        out_specs=[
            pl.BlockSpec((l, cb), lambda bi, j: (bi, j)),
            pl.BlockSpec((None, 2, cb), lambda bi, j: (bi, 0, j)),
        ],
        out_shape=[jax.ShapeDtypeStruct((b * l, w_lru), BF16), jax.ShapeDtypeStruct((b, 2, w_lru), F32)],
        scratch_shapes=[pltpu.VMEM((l, cb), F32)] * 3,
        compiler_params=_params(("arbitrary", "arbitrary")),
        name="rglru",
    )(proj, proj, conv_w, conv_b, wg, bg, lam, h0)
    return out, st


def _lru_gate_weights(wa, wx):
    depth, _, nb, bs, _ = wa.shape
    per = LANES // bs
    eye = jnp.eye(per, dtype=wa.dtype)

    def bd(w):
        w = w.reshape(depth, nb // per, per, bs, bs)
        return jnp.einsum("lgpde,pq->lgpdqe", w, eye).reshape(depth, nb // per, LANES, LANES)

    return jnp.concatenate([bd(wa[:, 0]), bd(wx[:, 0]), bd(wa[:, 1]), bd(wx[:, 1])], axis=-1).astype(BF16)


def _hy_conv_kernel(x_ref, w_ref, b_ref, o_ref, *, pad_left):
    o_ref[...] = _dwconv(x_ref[...], w_ref[...], b_ref[...], pad_left)


def _hy_conv(proj, tok0, b, l, col0, width, conv_w, conv_b, layer):
    cb = _pick(width, (512, 256, 128))
    assert tok0 % l == 0 and col0 % cb == 0
    k = conv_w.shape[1]
    return pl.pallas_call(
        functools.partial(_hy_conv_kernel, pad_left=k // 2),
        grid=(b, width // cb),
        in_specs=[
            pl.BlockSpec((l, cb), lambda bi, j: (tok0 // l + bi, col0 // cb + j)),
            pl.BlockSpec((None, k, cb), lambda bi, j: (layer, 0, j)),
            pl.BlockSpec((None, 1, cb), lambda bi, j: (layer, 0, j)),
        ],
        out_specs=pl.BlockSpec((l, cb), lambda bi, j: (bi, j)),
        out_shape=jax.ShapeDtypeStruct((b * l, width), F32),
        compiler_params=_params(("arbitrary", "arbitrary")),
        name="hyena_dwconv",
    )(proj, conv_w, conv_b)


def _hy_filter_kernel(z_ref, w1_ref, b1_ref, fr_ref, w2_ref, b2_ref, w3_ref, b3_ref, dec_ref, o_ref):
    hi = lax.Precision.HIGHEST
    z = z_ref[...]
    fr = fr_ref[...]
    h = jnp.sin(fr * (jnp.dot(z, w1_ref[...], precision=hi, preferred_element_type=F32) + b1_ref[...]))
    h = jnp.sin(fr * (jnp.dot(h, w2_ref[...], precision=hi, preferred_element_type=F32) + b2_ref[...]))
    filt = jnp.dot(h, w3_ref[...], precision=hi, preferred_element_type=F32) + b3_ref[...]
    o_ref[...] = filt * jnp.exp(-z[:, 0:1] * jnp.abs(dec_ref[...]))


def _pad2(x, rows, cols):
    return jnp.pad(x, ((0, rows - x.shape[0]), (0, cols - x.shape[1])))


def _hy_filters(l, w1, b1, freq, w2, b2, w3, b3, decay):
    n = w3.shape[-1]
    t = jnp.linspace(0.0, 1.0, l, dtype=F32)[:, None]
    w = 2.0 * math.pi * jnp.arange(l, dtype=F32)[:, None] / l
    f = jnp.linspace(1e-4, HY_BANDS - 1, HY_BANDS, dtype=F32)[None]
    z = _pad2(jnp.concatenate([t, jnp.cos(f * w), -jnp.sin(f * w)], axis=-1), l, LANES)
    tn = _pick(n, (512, 256, 128))
    row = lambda v: _pad2(v.reshape(1, -1), 1, LANES)
    full = lambda shape: pl.BlockSpec(shape, lambda j: (0, 0))
    return pl.pallas_call(
        _hy_filter_kernel,
        grid=(n // tn,),
        in_specs=[full((l, LANES)), full((LANES, LANES)), full((1, LANES)), full((1, LANES)),
                  full((LANES, LANES)), full((1, LANES)),
                  pl.BlockSpec((LANES, tn), lambda j: (0, j)),
                  pl.BlockSpec((1, tn), lambda j: (0, j)),
                  pl.BlockSpec((1, tn), lambda j: (0, j))],
        out_specs=pl.BlockSpec((l, tn), lambda j: (0, j)),
        out_shape=jax.ShapeDtypeStruct((l, n), F32),
        compiler_params=_params(("arbitrary",)),
        name="hyena_filters",
    )(z, _pad2(w1, LANES, LANES), row(b1), row(freq), _pad2(w2, LANES, LANES), row(b2),
      _pad2(w3, LANES, n), b3.reshape(1, n), decay.reshape(1, n))


def _dft_tile(l):
    return _pick(2 * l, (512, 256, 128))


def _dft_matrix(l):
    tile = _dft_tile(l)
    half = tile // 2
    r = jnp.arange(2 * l)
    k = (r // tile) * half + (r % half)
    is_im = (r % tile) >= half
    t = jnp.arange(l)
    ph = ((2 * k + 1)[:, None] * t[None, :]) % (4 * l)
    ang = ph.astype(F32) * (math.pi / (2 * l))
    return jnp.where(is_im[:, None], -jnp.sin(ang), jnp.cos(ang)).astype(BF16)


def _cmul_store(o_ref, s, f_re, f_im):
    half = s.shape[0] // 2
    s_re, s_im = s[:half], s[half:]
    o_ref[:half, :] = (s_re * f_re - s_im * f_im).astype(o_ref.dtype)
    o_ref[half:, :] = (s_re * f_im + s_im * f_re).astype(o_ref.dtype)


def _filter_spec_kernel(a_ref, hf_ref, hb_ref, o_ref, hf_scr, hb_scr):
    @pl.when(pl.program_id(2) == 0)
    def _():
        hf_scr[...] = hf_ref[...].astype(BF16)
        hb = hb_ref[...]
        row = lax.broadcasted_iota(jnp.int32, hb.shape, 0)
        hb_scr[...] = jnp.where(row == 0, 0.0, hb).astype(BF16)

    a = a_ref[...]
    sf = jnp.dot(a, hf_scr[...], preferred_element_type=F32)
    sb = jnp.dot(a, hb_scr[...], preferred_element_type=F32)
    half = a.shape[0] // 2
    o_ref[:half, :] = sf[:half] + sb[:half]
    o_ref[half:, :] = sf[half:] - sb[half:]


def _filter_spectrum(a_mat, filt, n_order, w_hy):
    n2, l = a_mat.shape
    tm = _dft_tile(l)
    tn = _pick(w_hy, (512, 256, 128))
    nj = w_hy // tn
    return pl.pallas_call(
        _filter_spec_kernel,
        grid=(n_order, nj, n2 // tm),
        in_specs=[
            pl.BlockSpec((tm, l), lambda o, j, i: (i, 0)),
            pl.BlockSpec((l, tn), lambda o, j, i: (0, (2 * o) * nj + j)),
            pl.BlockSpec((l, tn), lambda o, j, i: (0, (2 * o + 1) * nj + j)),
        ],
        out_specs=pl.BlockSpec((None, tm, tn), lambda o, j, i: (o, i, j)),
        out_shape=jax.ShapeDtypeStruct((n_order, n2, w_hy), F32),
        scratch_shapes=[pltpu.VMEM((l, tn), BF16)] * 2,
        compiler_params=_params(("arbitrary", "arbitrary", "arbitrary")),
        name="hyena_filter_spectrum",
    )(a_mat, filt, filt)


def _dft_fwd_kernel(a_ref, z_ref, f_ref, o_ref, z_scr):
    @pl.when(pl.program_id(2) == 0)
    def _():
        z_scr[...] = z_ref[...].astype(BF16)

    s = jnp.dot(a_ref[...], z_scr[...], preferred_element_type=F32)
    half = s.shape[0] // 2
    _cmul_store(o_ref, s, f_ref[:half, :], f_ref[half:, :])


def _dft_fwd(a_mat, z, z_col0, fspec, order, b, l, w_hy):
    n2 = a_mat.shape[0]
    tm = _dft_tile(l)
    tn = _pick(w_hy, (512, 256, 128))
    assert z_col0 % tn == 0
    return pl.pallas_call(
        _dft_fwd_kernel,
        grid=(b, w_hy // tn, n2 // tm),
        in_specs=[
            pl.BlockSpec((tm, l), lambda bi, j, i: (i, 0)),
            pl.BlockSpec((l, tn), lambda bi, j, i: (bi, z_col0 // tn + j)),
            pl.BlockSpec((None, tm, tn), lambda bi, j, i: (order, i, j)),
        ],
        out_specs=pl.BlockSpec((None, tm, tn), lambda bi, j, i: (bi, i, j)),
        out_shape=jax.ShapeDtypeStruct((b, n2, w_hy), BF16),
        scratch_shapes=[pltpu.VMEM((l, tn), BF16)],
        compiler_params=_params(("arbitrary", "arbitrary", "arbitrary")),
        name="hyena_dft_fwd",
    )(a_mat, z, fspec)


def _dft_inv_kernel(at_ref, y_ref, z_ref, g_ref, d_ref, o_ref, *, inv_l):
    y = jnp.dot(at_ref[...], y_ref[...], preferred_element_type=F32) * inv_l
    z = z_ref[...]
    o_ref[...] = (g_ref[...] * (y + d_ref[...] * z)).astype(o_ref.dtype)


def _dft_inv(at_mat, y, z, z_col0, gate, gate_col0, d, layer, order, b, l, w_hy, out_dtype):
    n2 = at_mat.shape[1]
    tm = _pick(l, (256, 128, 64))
    tn = _pick(w_hy, (512, 256, 128))
    nt = l // tm
    assert z_col0 % tn == 0 and gate_col0 % tn == 0
    return pl.pallas_call(
        functools.partial(_dft_inv_kernel, inv_l=1.0 / l),
        grid=(b, w_hy // tn, nt),
        in_specs=[
            pl.BlockSpec((tm, n2), lambda bi, j, i: (i, 0)),
            pl.BlockSpec((None, n2, tn), lambda bi, j, i: (bi, 0, j)),
            pl.BlockSpec((tm, tn), lambda bi, j, i: (bi * nt + i, z_col0 // tn + j)),
            pl.BlockSpec((tm, tn), lambda bi, j, i: (bi * nt + i, gate_col0 // tn + j)),
            pl.BlockSpec((None, None, 1, tn), lambda bi, j, i: (layer, order, 0, j)),
        ],
        out_specs=pl.BlockSpec((tm, tn), lambda bi, j, i: (bi * nt + i, j)),
        out_shape=jax.ShapeDtypeStruct((b * l, w_hy), out_dtype),
        compiler_params=_params(("arbitrary", "arbitrary", "arbitrary")),
        name="hyena_dft_inv",
    )(at_mat, y, z, gate, d)


def _hyena(proj, tok0, b, l, col0, w_hy, n_order, a_mat, at_mat, fspec, conv_w, conv_b, hy_d, layer):
    hyc = _hy_conv(proj, tok0, b, l, col0, (n_order + 1) * w_hy, conv_w, conv_b, layer)
    z, z_col0 = hyc, 0
    for o in range(n_order):
        y = _dft_fwd(a_mat, z, z_col0, fspec, o, b, l, w_hy)
        last = o == n_order - 1
        z = _dft_inv(at_mat, y, z, z_col0, hyc, (o + 1) * w_hy, hy_d, layer, o, b, l, w_hy, BF16 if last else F32)
        z_col0 = 0
    return z


def _merge_kernel(x_ref, mod_ref, g_ref, wg0_ref, wg1_ref, wg2_ref, b0_ref, b1_ref, b2_ref,
                  wb0_ref, wb1_ref, wb2_ref, o_ref, h_scr):
    @pl.when(pl.program_id(1) == 0)
    def _():
        h = _rms_mod(x_ref[...], g_ref[...], mod_ref[1:2, :], mod_ref[0:1, :])
        h_scr[...] = h.astype(BF16)

    h = h_scr[...]
    acc = None
    for wg_ref, b_ref, wb_ref in ((wg0_ref, b0_ref, wb0_ref), (wg1_ref, b1_ref, wb1_ref), (wg2_ref, b2_ref, wb2_ref)):
        gate = jax.nn.sigmoid(jnp.dot(h, wg_ref[...], preferred_element_type=F32))
        t = gate * jnp.dot(b_ref[...], wb_ref[...], preferred_element_type=F32)
        acc = t if acc is None else acc + t
    o_ref[...] = acc.astype(o_ref.dtype)


def _merge(x, mod, g, w_gate, branches, w_br, layer, m_ctx, l_s):
    m, d = x.shape
    wb = w_br.shape[2]
    tm = _pick(math.gcd(m_ctx, l_s), (512, 256, 128))
    tn = _pick(d, (256, 128))
    nj = d // tn
    gate_spec = lambda n: pl.BlockSpec((None, d, tn), lambda i, j: (layer, 0, n * nj + j))
    br_spec = pl.BlockSpec((tm, wb), lambda i, j: (i, 0))
    wbr_spec = lambda n: pl.BlockSpec((None, None, wb, tn), lambda i, j: (layer, n, 0, j))
    return pl.pallas_call(
        _merge_kernel,
        grid=(m // tm, nj),
        in_specs=[
            pl.BlockSpec((tm, d), lambda i, j: (i, 0)),
            pl.BlockSpec((None, 6, d), lambda i, j: (_mod_row(i, tm, m_ctx, l_s), 0, 0)),
            pl.BlockSpec((None, 1, d), lambda i, j: (layer, 0, 0)),
            gate_spec(0), gate_spec(1), gate_spec(2),
            br_spec, br_spec, br_spec,
            wbr_spec(0), wbr_spec(1), wbr_spec(2),
        ],
        out_specs=pl.BlockSpec((tm, tn), lambda i, j: (i, j)),
        out_shape=jax.ShapeDtypeStruct((m, d), BF16),
        scratch_shapes=[pltpu.VMEM((tm, d), BF16)],
        compiler_params=_params(("arbitrary", "arbitrary")),
        name="gated_merge",
    )(x, mod, g, w_gate, w_gate, w_gate, *branches, w_br, w_br, w_br)


def _proj_residual_kernel(a_ref, w_ref, x_ref, mod_ref, o_ref, *, gate_row):
    gate = mod_ref[gate_row:gate_row + 1, :]
    o_ref[...] = x_ref[...] + gate * jnp.dot(a_ref[...], w_ref[...], preferred_element_type=F32)


def _proj_residual(a, w, x, mod, gate_row, layer, m_ctx, l_s):
    m, k = a.shape
    d = x.shape[1]
    tm = _pick(math.gcd(m_ctx, l_s), (512, 256, 128))
    tn = _pick(d, (512, 256, 128))
    return pl.pallas_call(
        functools.partial(_proj_residual_kernel, gate_row=gate_row),
        grid=(m // tm, d // tn),
        in_specs=[
            pl.BlockSpec((tm, k), lambda i, j: (i, 0)),
            pl.BlockSpec((None, k, tn), lambda i, j: (layer, 0, j)),
            pl.BlockSpec((tm, tn), lambda i, j: (i, j)),
            pl.BlockSpec((None, 6, tn), lambda i, j: (_mod_row(i, tm, m_ctx, l_s), 0, j)),
        ],
        out_specs=pl.BlockSpec((tm, tn), lambda i, j: (i, j)),
        out_shape=jax.ShapeDtypeStruct((m, d), F32),
        compiler_params=_params(("arbitrary", "arbitrary")),
        name="proj_residual",
    )(a, w, x, mod)


def _ffn_up_kernel(x_ref, mod_ref, g_ref, wg_ref, wu_ref, o_ref, h_scr):
    @pl.when(pl.program_id(1) == 0)
    def _():
        h = _rms_mod(x_ref[...], g_ref[...], mod_ref[4:5, :], mod_ref[3:4, :])
        h_scr[...] = h.astype(BF16)

    h = h_scr[...]
    a = jnp.dot(h, wg_ref[...], preferred_element_type=F32)
    u = jnp.dot(h, wu_ref[...], preferred_element_type=F32)
    o_ref[...] = ((a * jax.nn.sigmoid(a)) * u).astype(o_ref.dtype)


def _ffn_up(x, mod, g, w_gate, w_up, layer, m_ctx, l_s):
    m, d = x.shape
    n = w_gate.shape[-1]
    tm = _pick(math.gcd(m_ctx, l_s), (512, 256, 128))
    tn = _pick(n, (512, 256, 128))
    w_spec = pl.BlockSpec((None, d, tn), lambda i, j: (layer, 0, j))
    return pl.pallas_call(
        _ffn_up_kernel,
        grid=(m // tm, n // tn),
        in_specs=[
            pl.BlockSpec((tm, d), lambda i, j: (i, 0)),
            pl.BlockSpec((None, 6, d), lambda i, j: (_mod_row(i, tm, m_ctx, l_s), 0, 0)),
            pl.BlockSpec((None, 1, d), lambda i, j: (layer, 0, 0)),
            w_spec, w_spec,
        ],
        out_specs=pl.BlockSpec((tm, tn), lambda i, j: (i, j)),
        out_shape=jax.ShapeDtypeStruct((m, n), BF16),
        scratch_shapes=[pltpu.VMEM((tm, d), BF16)],
        compiler_params=_params(("arbitrary", "arbitrary")),
        name="ffn_up",
    )(x, mod, g, w_gate, w_up)


def _final_norm_kernel(x_ref, g_ref, o_ref):
    x = x_ref[...]
    o_ref[...] = (x * lax.rsqrt(jnp.mean(x * x, axis=-1, keepdims=True) + NORM_EPS)) * g_ref[...]


def _final_norm(x, tok0, n_tok, g):
    d = x.shape[1]
    tm = _pick(math.gcd(tok0, n_tok) if tok0 else n_tok, (512, 256, 128, 64))
    return pl.pallas_call(
        _final_norm_kernel,
        grid=(n_tok // tm,),
        in_specs=[pl.BlockSpec((tm, d), lambda i: (tok0 // tm + i, 0)), pl.BlockSpec((1, d), lambda i: (0, 0))],
        out_specs=pl.BlockSpec((tm, d), lambda i: (i, 0)),
        out_shape=jax.ShapeDtypeStruct((n_tok, d), F32),
        compiler_params=_params(("arbitrary",)),
        name="final_norm",
    )(x, g)


def kernel(x_prompt, x_sample, c, cache_k, cache_v, state_lru, c_ctx, w_mod, b_mod, norm_mix, norm_ffn, w_in, w_gate, att_lambda, att_subln, lru_conv_w, lru_conv_b, lru_wa, lru_ba, lru_wx, lru_bx, lru_lambda, hy_conv_w, hy_conv_b, hy_w1, hy_b1, hy_w2, hy_b2, hy_w3, hy_b3, hy_freq, hy_decay, hy_d, w_br, w_out, w_ff_gate, w_ff_up, w_ff_down, final_norm):
    b_c, l_c, d = x_prompt.shape
    b_s, l_s, _ = x_sample.shape
    depth = w_in.shape[0]
    n_heads, d_v = cache_v.shape[3], cache_v.shape[4]
    w_att = n_heads * d_v
    w_lru = lru_lambda.shape[-1]
    n_order, w_hy = hy_d.shape[1], hy_d.shape[2]
    m_ctx, m_s = b_c * l_c, b_s * l_s
    past = cache_k.shape[2]
    col_xl, col_gl, col_hy = 3 * w_att, 3 * w_att + w_lru, 3 * w_att + 2 * w_lru

    x = jnp.concatenate([x_prompt.reshape(m_ctx, d), x_sample.reshape(m_s, d)], axis=0)

    w_in_b, w_gate_b, w_br_b, w_out_b = (w.astype(BF16) for w in (w_in, w_gate, w_br, w_out))
    w_ffg_b, w_ffu_b, w_ffd_b = (w.astype(BF16) for w in (w_ff_gate, w_ff_up, w_ff_down))
    lru_wg = _lru_gate_weights(lru_wa, lru_wx)
    lru_bg = jnp.stack([lru_ba[:, 0], lru_bx[:, 0], lru_ba[:, 1], lru_bx[:, 1]], axis=1)
    norm_mix3, norm_ffn3 = norm_mix.reshape(depth, 1, d), norm_ffn.reshape(depth, 1, d)
    att_subln3 = att_subln.reshape(depth, 1, d_v)
    lru_conv_b3 = lru_conv_b.reshape(depth, 1, w_lru)
    hy_conv_b3 = hy_conv_b.reshape(depth, 1, -1)
    hy_d4 = hy_d.reshape(depth, n_order, 1, w_hy)

    n_rows = 1 + b_s
    assert n_rows <= SUBLANES
    c8 = jnp.zeros((SUBLANES, d), F32).at[0].set(c_ctx).at[1:n_rows].set(c)
    mod_all = _modulation(c8, w_mod, b_mod)[:, :n_rows].reshape(depth, n_rows, 6, d)

    rope = _rope_tables(l_s, d_v // 2)
    dft = {l: (_dft_matrix(l), _dft_matrix(l).T) for l in sorted({l_c, l_s})}
    h0_ctx = jnp.zeros((b_c, 2, w_lru), F32)

    ks, vs, ss = [], [], []
    for layer in range(depth):
        lam_init = 0.8 - 0.6 * math.exp(-0.3 * layer)
        mod = mod_all[layer]
        proj = _norm_proj(x, mod, norm_mix3, w_in_b, layer, m_ctx, l_s)

        ck = cache_k[:, layer].reshape(b_s, past, w_att)
        cv = cache_v[:, layer].reshape(b_s, past, w_att)
        att_args = (n_heads, d_v, lam_init, att_lambda, att_subln3, layer)
        att = jnp.concatenate([
            _attention(proj, 0, b_c, l_c, *att_args),
            _attention(proj, m_ctx, b_s, l_s, *att_args, rope=rope, cache=(ck, cv)),
        ], axis=0)

        lru_args = (col_xl, col_gl, w_lru, lru_conv_w, lru_conv_b3, lru_wg, lru_bg, lru_lambda)
        lru_c, st_c = _lru(proj, 0, b_c, l_c, *lru_args, h0_ctx, layer)
        lru_s, _ = _lru(proj, m_ctx, b_s, l_s, *lru_args, state_lru[:, layer], layer)
        lru = jnp.concatenate([lru_c, lru_s], axis=0)

        hy_parts = []
        for tok0, b, l in ((0, b_c, l_c), (m_ctx, b_s, l_s)):
            a_mat, at_mat = dft[l]
            filt = _hy_filters(l, hy_w1[layer], hy_b1[layer], hy_freq[layer], hy_w2[layer], hy_b2[layer],
                               hy_w3[layer], hy_b3[layer], hy_decay[layer])
            fspec = _filter_spectrum(a_mat, filt, n_order, w_hy)
            hy_parts.append(_hyena(proj, tok0, b, l, col_hy, w_hy, n_order, a_mat, at_mat, fspec,
                                   hy_conv_w, hy_conv_b3, hy_d4, layer))
        hyo = jnp.concatenate(hy_parts, axis=0)

        mixed = _merge(x, mod, norm_mix3, w_gate_b, (att, lru, hyo), w_br_b, layer, m_ctx, l_s)
        x = _proj_residual(mixed, w_out_b, x, mod, 2, layer, m_ctx, l_s)
        u = _ffn_up(x, mod, norm_ffn3, w_ffg_b, w_ffu_b, layer, m_ctx, l_s)
        x = _proj_residual(u, w_ffd_b, x, mod, 5, layer, m_ctx, l_s)

        ks.append(proj[:m_ctx, w_att:2 * w_att].reshape(b_c, l_c, n_heads, d_v))
        vs.append(proj[:m_ctx, 2 * w_att:3 * w_att].reshape(b_c, l_c, n_heads, d_v))
        ss.append(st_c)

    g_fin = final_norm.reshape(1, d)
    y_prompt = _final_norm(x, 0, m_ctx, g_fin).reshape(b_c, l_c, d)
    y_sample = _final_norm(x, m_ctx, m_s, g_fin).reshape(b_s, l_s, d)
    return (y_prompt, y_sample, jnp.stack(ks, axis=1), jnp.stack(vs, axis=1), jnp.stack(ss, axis=1))
```

```python
import functools
import math

import jax
import jax.numpy as jnp
import numpy as np
from jax import lax
from jax.experimental import pallas as pl
from jax.experimental.pallas import tpu as pltpu

F32 = jnp.float32
BF16 = jnp.bfloat16

GRID_W = 64
ROPE_THETA = 10000.0
NORM_EPS = 1e-6
SUBLN_EPS = 1e-5
LRU_C = 8.0
HY_BANDS = 8
LANES = 128
SUBLANES = 8
VMEM_LIMIT_BYTES = 56 * 1024 * 1024


def _pick(n, prefs):
    for p in prefs:
        if n % p == 0:
            return p
    raise ValueError(f"no tile in {prefs} divides {n}")


def _params(sem):
    return pltpu.CompilerParams(dimension_semantics=sem, vmem_limit_bytes=VMEM_LIMIT_BYTES)


def _row_tile(rpm):
    return _pick(rpm, (1024, 512, 256, 128))


def _mod_row(i, tm, rpm):
    return i // (rpm // tm)


def _rms_mod(x, g, sc, sh):
    y = x * lax.rsqrt(jnp.mean(x * x, axis=-1, keepdims=True) + NORM_EPS)
    return (y * g) * (1.0 + sc) + sh


def _mod_kernel(c_ref, w_ref, b_ref, o_ref):
    c = c_ref[...]
    s = (c * jax.nn.sigmoid(c)).astype(BF16)
    o_ref[...] = jnp.dot(s, w_ref[...].astype(BF16), preferred_element_type=F32) + b_ref[...]


def _modulation(c8, w_mod, b_mod):
    depth, d, n = w_mod.shape
    tn = _pick(n, (1024, 512, 256, 128))
    return pl.pallas_call(
        _mod_kernel,
        grid=(depth, n // tn),
        in_specs=[
            pl.BlockSpec((SUBLANES, d), lambda l, j: (0, 0)),
            pl.BlockSpec((None, d, tn), lambda l, j: (l, 0, j)),
            pl.BlockSpec((None, 1, tn), lambda l, j: (l, 0, j)),
        ],
        out_specs=pl.BlockSpec((None, SUBLANES, tn), lambda l, j: (l, 0, j)),
        out_shape=jax.ShapeDtypeStruct((depth, SUBLANES, n), F32),
        compiler_params=_params(("arbitrary", "arbitrary")),
        name="modulation",
    )(c8, w_mod, b_mod.reshape(depth, 1, n))


def _norm_proj_kernel(x_ref, mod_ref, g_ref, w_ref, o_ref, *rest, kv_tiles):
    h_scr = rest[-1]
    j = pl.program_id(1)

    @pl.when(j == 0)
    def _():
        h = _rms_mod(x_ref[...], g_ref[...], mod_ref[1:2, :], mod_ref[0:1, :])
        h_scr[...] = h.astype(BF16)

    y = jnp.dot(h_scr[...], w_ref[...], preferred_element_type=F32)
    o_ref[...] = y
    if kv_tiles:
        k_ref, v_ref = rest[0], rest[1]

        @pl.when((j >= kv_tiles) & (j < 2 * kv_tiles))
        def _():
            k_ref[...] = y

        @pl.when((j >= 2 * kv_tiles) & (j < 3 * kv_tiles))
        def _():
            v_ref[...] = y


def _norm_proj(x, mod, g, w, layer, rpm, w_att=None):
    m, d = x.shape
    n = w.shape[-1]
    tm = _row_tile(rpm)
    tn = _pick(math.gcd(n, w_att or n), (512, 256, 128) if w_att else (1024, 512, 256, 128))
    out_specs = [pl.BlockSpec((tm, tn), lambda i, j: (i, j))]
    out_shape = [jax.ShapeDtypeStruct((m, n), F32)]
    kv_tiles = 0
    if w_att is not None:
        kv_tiles = w_att // tn
        out_specs += [pl.BlockSpec((tm, tn), lambda i, j, t=t: (i, jnp.clip(j - t * kv_tiles, 0, kv_tiles - 1)))
                      for t in (1, 2)]
        out_shape += [jax.ShapeDtypeStruct((m, w_att), F32)] * 2
    return pl.pallas_call(
        functools.partial(_norm_proj_kernel, kv_tiles=kv_tiles),
        grid=(m // tm, n // tn),
        in_specs=[
            pl.BlockSpec((tm, d), lambda i, j: (i, 0)),
            pl.BlockSpec((None, 6, d), lambda i, j: (_mod_row(i, tm, rpm), 0, 0)),
            pl.BlockSpec((None, 1, d), lambda i, j: (layer, 0, 0)),
            pl.BlockSpec((None, d, tn), lambda i, j: (layer, 0, j)),
        ],
        out_specs=out_specs,
        out_shape=out_shape,
        scratch_shapes=[pltpu.VMEM((tm, d), BF16)],
        compiler_params=_params(("arbitrary", "arbitrary")),
        name="norm_proj",
    )(x, mod, g, w)


def _rope(x, cos, sin_signed):
    n = x.shape[-1]
    half = 16
    lane = lax.broadcasted_iota(jnp.int32, x.shape, 1)
    swapped = jnp.where((lane % (2 * half)) < half, pltpu.roll(x, n - half, 1), pltpu.roll(x, half, 1))
    return x * cos + swapped * sin_signed


def _attn_kernel(*refs, lam_init, d_qk, use_rope, use_cache):
    it = iter(refs)
    lam_ref, sub_ref, q_ref, k_ref, v_ref = next(it), next(it), next(it), next(it), next(it)
    if use_rope:
        cq_ref, sq_ref, ck_ref, sk_ref = next(it), next(it), next(it), next(it)
    if use_cache:
        pk_ref, pv_ref = next(it), next(it)
    o_ref = next(it)
    k_scr = next(it)

    @pl.when(pl.program_id(2) == 0)
    def _():
        k = k_ref[...]
        if use_rope:
            k = _rope(k, ck_ref[...], sk_ref[...])
        k_scr[...] = k.astype(BF16)

    a = lam_ref[...]
    lam = (jnp.exp(jnp.sum(a[0:1] * a[1:2], axis=-1, keepdims=True))
           - jnp.exp(jnp.sum(a[2:3] * a[3:4], axis=-1, keepdims=True)) + lam_init)

    q = q_ref[...]
    if use_rope:
        q = _rope(q, cq_ref[...], sq_ref[...])
    lane = lax.broadcasted_iota(jnp.int32, q.shape, 1)
    q1 = jnp.where(lane < d_qk, q, 0.0).astype(BF16)
    q2 = jnp.where(lane >= d_qk, q, 0.0).astype(BF16)
    scale = d_qk ** -0.5
    nt = (((1,), (1,)), ((), ()))
    keys = [k_scr[...]]
    vals = [v_ref[...].astype(BF16)]
    if use_cache:
        keys.append(pk_ref[...].astype(BF16))
        vals.append(pv_ref[...].astype(BF16))

    def probs(qh):
        s = [lax.dot_general(qh, kk, nt, preferred_element_type=F32) * scale for kk in keys]
        mx = functools.reduce(jnp.maximum, [jnp.max(x, axis=-1, keepdims=True) for x in s])
        p = [jnp.exp(x - mx) for x in s]
        den = functools.reduce(jnp.add, [jnp.sum(x, axis=-1, keepdims=True) for x in p])
        return p, 1.0 / den

    p1, r1 = probs(q1)
    p2, r2 = probs(q2)
    r2 = lam * r2
    o = None
    for a1, a2, vv in zip(p1, p2, vals):
        w = (a1 * r1 - a2 * r2).astype(BF16)
        t = jnp.dot(w, vv, preferred_element_type=F32)
        o = t if o is None else o + t
    y = o * lax.rsqrt(jnp.mean(o * o, axis=-1, keepdims=True) + SUBLN_EPS)
    o_ref[...] = ((y * sub_ref[...]) * (1.0 - lam_init)).astype(o_ref.dtype)


def _attention(proj, tok0, b, l, h, d_v, lam_init, att_lambda, att_subln, layer, rope=None, cache=None):
    assert d_v == LANES
    tq = _pick(l, (256, 128, 64))
    assert tok0 % l == 0 and tok0 % tq == 0
    nq = l // tq
    q_map = lambda bi, hi, qi: (tok0 // tq + bi * nq + qi, hi)
    in_specs = [
        pl.BlockSpec((None, 4, att_lambda.shape[-1]), lambda bi, hi, qi: (layer, 0, 0)),
        pl.BlockSpec((None, 1, d_v), lambda bi, hi, qi: (layer, 0, 0)),
        pl.BlockSpec((tq, d_v), q_map),
        pl.BlockSpec((l, d_v), lambda bi, hi, qi: (tok0 // l + bi, h + hi)),
        pl.BlockSpec((l, d_v), lambda bi, hi, qi: (tok0 // l + bi, 2 * h + hi)),
    ]
    args = [att_lambda, att_subln, proj, proj, proj]
    if rope is not None:
        cos, sin = rope
        in_specs += [
            pl.BlockSpec((tq, d_v), lambda bi, hi, qi: (qi, 0)),
            pl.BlockSpec((tq, d_v), lambda bi, hi, qi: (qi, 0)),
            pl.BlockSpec((l, d_v), lambda bi, hi, qi: (0, 0)),
            pl.BlockSpec((l, d_v), lambda bi, hi, qi: (0, 0)),
        ]
        args += [cos, sin, cos, sin]
    if cache is not None:
        ck, cv = cache
        past = ck.shape[1]
        in_specs += [
            pl.BlockSpec((None, past, d_v), lambda bi, hi, qi: (bi, 0, hi)),
            pl.BlockSpec((None, past, d_v), lambda bi, hi, qi: (bi, 0, hi)),
        ]
        args += [ck, cv]
    return pl.pallas_call(
        functools.partial(_attn_kernel, lam_init=lam_init, d_qk=d_v // 2,
                          use_rope=rope is not None, use_cache=cache is not None),
        grid=(b, h, nq),
        in_specs=in_specs,
        out_specs=pl.BlockSpec((tq, d_v), lambda bi, hi, qi: (bi * nq + qi, hi)),
        out_shape=jax.ShapeDtypeStruct((b * l, h * d_v), BF16),
        scratch_shapes=[pltpu.VMEM((l, d_v), BF16)],
        compiler_params=_params(("arbitrary", "arbitrary", "arbitrary")),
        name="diff_attention",
    )(*args)


def _rope_tables(l, d_qk):
    n_freq = d_qk // 4
    lane = jnp.arange(2 * d_qk)
    axis = (lane % d_qk) // (2 * n_freq)
    freq = lane % n_freq
    first = (lane % (2 * n_freq)) < n_freq
    inv_freq = ROPE_THETA ** (-freq.astype(F32) / n_freq)
    t = jnp.arange(l)
    pos = jnp.where(axis[None, :] == 0, (t // GRID_W)[:, None], (t % GRID_W)[:, None]).astype(F32)
    ang = pos * inv_freq[None, :]
    return jnp.cos(ang), jnp.where(first[None, :], -jnp.sin(ang), jnp.sin(ang))


def _shift_rows(x, d):
    n = x.shape[0]
    if d == 0:
        return x
    row = lax.broadcasted_iota(jnp.int32, x.shape, 0)
    rolled = pltpu.roll(x, (-d) % n, 0)
    ok = (row + d >= 0) & (row + d < n)
    return jnp.where(ok, rolled, 0.0)


def _dwconv(x, w, bias, pad_left):
    y = bias + w[0:1] * _shift_rows(x, -pad_left)
    for k in range(1, w.shape[0]):
        y = y + w[k:k + 1] * _shift_rows(x, k - pad_left)
    return y


def _lru_kernel(xl_ref, gl_ref, cw_ref, cb_ref, wg_ref, bg_ref, lam_ref, h0_ref, o_ref, st_ref,
                a_scr, b_scr, hf_scr, *, pad_left):
    l, cb = xl_ref.shape
    n_grp = cb // LANES
    xc = _dwconv(xl_ref[...], cw_ref[...], cb_ref[...], pad_left)
    lam = lam_ref[...]
    neg = -lam
    softplus = jnp.maximum(neg, 0.0) + jnp.log1p(jnp.exp(-jnp.abs(neg)))
    row8 = lax.broadcasted_iota(jnp.int32, (SUBLANES, cb), 0)
    n_chunks = l // SUBLANES

    def fill(direction):
        for g in range(n_grp):
            sl = slice(g * LANES, (g + 1) * LANES)
            xg = xc[:, sl]
            y = jnp.dot(xg.astype(BF16), wg_ref[g, :, 2 * direction * LANES:(2 * direction + 2) * LANES],
                        preferred_element_type=F32)
            r = jax.nn.sigmoid(y[:, :LANES] + bg_ref[2 * direction:2 * direction + 1, sl])
            i = jax.nn.sigmoid(y[:, LANES:] + bg_ref[2 * direction + 1:2 * direction + 2, sl])
            log_a = (-LRU_C * r) * softplus[direction:direction + 1, sl]
            th = jnp.tanh(log_a)
            one_minus_a2 = (-2.0 * th) / (1.0 - th)
            a_scr[:, sl] = jnp.exp(log_a)
            b_scr[:, sl] = jnp.sqrt(one_minus_a2) * (i * xg)

    fill(0)

    def fwd(c, h):
        r0 = pl.multiple_of(c * SUBLANES, SUBLANES)
        a = a_scr[pl.ds(r0, SUBLANES), :]
        b = b_scr[pl.ds(r0, SUBLANES), :]
        for s in (1, 2, 4):
            keep = row8 >= s
            b = jnp.where(keep, a * pltpu.roll(b, s, 0) + b, b)
            a = jnp.where(keep, a * pltpu.roll(a, s, 0), a)
        h8 = a * h + b
        hf_scr[pl.ds(r0, SUBLANES), :] = h8
        return h8[SUBLANES - 1:SUBLANES, :]

    s_f = lax.fori_loop(0, n_chunks, fwd, h0_ref[0:1, :])

    fill(1)

    def bwd(c, h):
        r0 = pl.multiple_of((n_chunks - 1 - c) * SUBLANES, SUBLANES)
        a = a_scr[pl.ds(r0, SUBLANES), :]
        b = b_scr[pl.ds(r0, SUBLANES), :]
        for s in (1, 2, 4):
            keep = row8 < SUBLANES - s
            b = jnp.where(keep, a * pltpu.roll(b, SUBLANES - s, 0) + b, b)
            a = jnp.where(keep, a * pltpu.roll(a, SUBLANES - s, 0), a)
        h8 = a * h + b
        b_scr[pl.ds(r0, SUBLANES), :] = h8
        return h8[0:1, :]

    s_b = lax.fori_loop(0, n_chunks, bwd, h0_ref[1:2, :])

    st_ref[0:1, :] = s_f
    st_ref[1:2, :] = s_b
    o_ref[...] = ((hf_scr[...] + b_scr[...]) * jax.nn.gelu(gl_ref[...])).astype(o_ref.dtype)


def _lru(proj, tok0, b, l, col_x, col_g, w_lru, conv_w, conv_b, wg, bg, lam, h0, layer):
    cb = _pick(w_lru, (256, 128))
    assert tok0 % l == 0 and col_x % cb == 0 and col_g % cb == 0
    k = conv_w.shape[1]
    n_grp = cb // LANES
    out, st = pl.pallas_call(
        functools.partial(_lru_kernel, pad_left=k // 2),
        grid=(b, w_lru // cb),
        in_specs=[
            pl.BlockSpec((l, cb), lambda bi, j: (tok0 // l + bi, col_x // cb + j)),
            pl.BlockSpec((l, cb), lambda bi, j: (tok0 // l + bi, col_g // cb + j)),
            pl.BlockSpec((None, k, cb), lambda bi, j: (layer, 0, j)),
            pl.BlockSpec((None, 1, cb), lambda bi, j: (layer, 0, j)),
            pl.BlockSpec((None, n_grp, LANES, 4 * LANES), lambda bi, j: (layer, j, 0, 0)),
            pl.BlockSpec((None, 4, cb), lambda bi, j: (layer, 0, j)),
            pl.BlockSpec((None, 2, cb), lambda bi, j: (layer, 0, j)),
            pl.BlockSpec((None, 2, cb), lambda bi, j: (bi, 0, j)),
        ],
        out_specs=[
            pl.BlockSpec((l, cb), lambda bi, j: (bi, j)),
            pl.BlockSpec((None, 2, cb), lambda bi, j: (bi, 0, j)),
        ],
        out_shape=[jax.ShapeDtypeStruct((b * l, w_lru), BF16), jax.ShapeDtypeStruct((b, 2, w_lru), F32)],
        scratch_shapes=[pltpu.VMEM((l, cb), F32)] * 3,
        compiler_params=_params(("arbitrary", "arbitrary")),
        name="rglru",
    )(proj, proj, conv_w, conv_b, wg, bg, lam, h0)
    return out, st


def _lru_gate_weights(wa, wx):
    depth, _, nb, bs, _ = wa.shape
    per = LANES // bs
    eye = jnp.eye(per, dtype=wa.dtype)

    def bd(w):
        w = w.reshape(depth, nb // per, per, bs, bs)
        return jnp.einsum("lgpde,pq->lgpdqe", w, eye).reshape(depth, nb // per, LANES, LANES)

    return jnp.concatenate([bd(wa[:, 0]), bd(wx[:, 0]), bd(wa[:, 1]), bd(wx[:, 1])], axis=-1).astype(BF16)


def _hy_conv_kernel(x_ref, w_ref, b_ref, o_ref, *, pad_left):
    o_ref[...] = _dwconv(x_ref[...], w_ref[...], b_ref[...], pad_left)


def _hy_conv(proj, tok0, b, l, col0, width, conv_w, conv_b, layer):
    cb = _pick(width, (512, 256, 128))
    assert tok0 % l == 0 and col0 % cb == 0
    k = conv_w.shape[1]
    return pl.pallas_call(
        functools.partial(_hy_conv_kernel, pad_left=k // 2),
        grid=(b, width // cb),
        in_specs=[
            pl.BlockSpec((l, cb), lambda bi, j: (tok0 // l + bi, col0 // cb + j)),
            pl.BlockSpec((None, k, cb), lambda bi, j: (layer, 0, j)),
            pl.BlockSpec((None, 1, cb), lambda bi, j: (layer, 0, j)),
        ],
        out_specs=pl.BlockSpec((l, cb), lambda bi, j: (bi, j)),
        out_shape=jax.ShapeDtypeStruct((b * l, width), F32),
        compiler_params=_params(("arbitrary", "arbitrary")),
        name="hyena_dwconv",
    )(proj, conv_w, conv_b)


def _hy_filter_kernel(z_ref, w1_ref, b1_ref, fr_ref, w2_ref, b2_ref, w3_ref, b3_ref, dec_ref, o_ref):
    hi = lax.Precision.HIGHEST
    z = z_ref[...]
    fr = fr_ref[...]
    h = jnp.sin(fr * (jnp.dot(z, w1_ref[...], precision=hi, preferred_element_type=F32) + b1_ref[...]))
    h = jnp.sin(fr * (jnp.dot(h, w2_ref[...], precision=hi, preferred_element_type=F32) + b2_ref[...]))
    filt = jnp.dot(h, w3_ref[...], precision=hi, preferred_element_type=F32) + b3_ref[...]
    o_ref[...] = filt * jnp.exp(-z[:, 0:1] * jnp.abs(dec_ref[...]))


def _pad2(x, rows, cols):
    return jnp.pad(x, ((0, rows - x.shape[0]), (0, cols - x.shape[1])))


def _hy_filters(l, w1, b1, freq, w2, b2, w3, b3, decay):
    n = w3.shape[-1]
    t = jnp.linspace(0.0, 1.0, l, dtype=F32)[:, None]
    w = 2.0 * math.pi * jnp.arange(l, dtype=F32)[:, None] / l
    f = jnp.linspace(1e-4, HY_BANDS - 1, HY_BANDS, dtype=F32)[None]
    z = _pad2(jnp.concatenate([t, jnp.cos(f * w), -jnp.sin(f * w)], axis=-1), l, LANES)
    tn = _pick(n, (512, 256, 128))
    row = lambda v: _pad2(v.reshape(1, -1), 1, LANES)
    full = lambda shape: pl.BlockSpec(shape, lambda j: (0, 0))
    return pl.pallas_call(
        _hy_filter_kernel,
        grid=(n // tn,),
        in_specs=[full((l, LANES)), full((LANES, LANES)), full((1, LANES)), full((1, LANES)),
                  full((LANES, LANES)), full((1, LANES)),
                  pl.BlockSpec((LANES, tn), lambda j: (0, j)),
                  pl.BlockSpec((1, tn), lambda j: (0, j)),
                  pl.BlockSpec((1, tn), lambda j: (0, j))],
        out_specs=pl.BlockSpec((l, tn), lambda j: (0, j)),
        out_shape=jax.ShapeDtypeStruct((l, n), F32),
        compiler_params=_params(("arbitrary",)),
        name="hyena_filters",
    )(z, _pad2(w1, LANES, LANES), row(b1), row(freq), _pad2(w2, LANES, LANES), row(b2),
      _pad2(w3, LANES, n), b3.reshape(1, n), decay.reshape(1, n))


def _dft_tile(l):
    return _pick(2 * l, (512, 256, 128))


@functools.lru_cache(maxsize=None)
def _dft_matrices(l):
    tile = _dft_tile(l)
    half = tile // 2
    r = np.arange(2 * l)
    k = (r // tile) * half + (r % half)
    is_im = (r % tile) >= half
    t = np.arange(l)
    ph = ((2 * k + 1)[:, None] * t[None, :]) % (4 * l)
    ang = ph.astype(np.float64) * (math.pi / (2 * l))
    return np.where(is_im[:, None], -np.sin(ang), np.cos(ang)).astype(np.float32)


def _cmul_store(o_ref, s, f_re, f_im):
    half = s.shape[0] // 2
    s_re, s_im = s[:half], s[half:]
    o_ref[:half, :] = (s_re * f_re - s_im * f_im).astype(o_ref.dtype)
    o_ref[half:, :] = (s_re * f_im + s_im * f_re).astype(o_ref.dtype)


def _filter_spec_kernel(a_ref, hf_ref, hb_ref, o_ref, hf_scr, hb_scr):
    @pl.when(pl.program_id(2) == 0)
    def _():
        hf_scr[...] = hf_ref[...].astype(BF16)
        hb = hb_ref[...]
        row = lax.broadcasted_iota(jnp.int32, hb.shape, 0)
        hb_scr[...] = jnp.where(row == 0, 0.0, hb).astype(BF16)

    a = a_ref[...]
    sf = jnp.dot(a, hf_scr[...], preferred_element_type=F32)
    sb = jnp.dot(a, hb_scr[...], preferred_element_type=F32)
    half = a.shape[0] // 2
    o_ref[:half, :] = sf[:half] + sb[:half]
    o_ref[half:, :] = sf[half:] - sb[half:]


def _filter_spectrum(a_mat, filt, n_order, w_hy):
    n2, l = a_mat.shape
    tm = _dft_tile(l)
    tn = _pick(w_hy, (512, 256, 128))
    nj = w_hy // tn
    return pl.pallas_call(
        _filter_spec_kernel,
        grid=(n_order, nj, n2 // tm),
        in_specs=[
            pl.BlockSpec((tm, l), lambda o, j, i: (i, 0)),
            pl.BlockSpec((l, tn), lambda o, j, i: (0, (2 * o) * nj + j)),
            pl.BlockSpec((l, tn), lambda o, j, i: (0, (2 * o + 1) * nj + j)),
        ],
        out_specs=pl.BlockSpec((None, tm, tn), lambda o, j, i: (o, i, j)),
        out_shape=jax.ShapeDtypeStruct((n_order, n2, w_hy), F32),
        scratch_shapes=[pltpu.VMEM((l, tn), BF16)] * 2,
        compiler_params=_params(("arbitrary", "arbitrary", "arbitrary")),
        name="hyena_filter_spectrum",
    )(a_mat, filt, filt)


def _dft_fwd_kernel(a_ref, z_ref, f_ref, o_ref, z_scr):
    @pl.when(pl.program_id(2) == 0)
    def _():
        z_scr[...] = z_ref[...].astype(BF16)

    s = jnp.dot(a_ref[...], z_scr[...], preferred_element_type=F32)
    half = s.shape[0] // 2
    _cmul_store(o_ref, s, f_ref[:half, :], f_ref[half:, :])


def _dft_fwd(a_mat, z, z_col0, fspec, order, b, l, w_hy):
    n2 = a_mat.shape[0]
    tm = _dft_tile(l)
    tn = _pick(w_hy, (1024, 512, 256, 128))
    assert z_col0 % tn == 0
    return pl.pallas_call(
        _dft_fwd_kernel,
        grid=(b, w_hy // tn, n2 // tm),
        in_specs=[
            pl.BlockSpec((tm, l), lambda bi, j, i: (i, 0)),
            pl.BlockSpec((l, tn), lambda bi, j, i: (bi, z_col0 // tn + j)),
            pl.BlockSpec((None, tm, tn), lambda bi, j, i: (order, i, j)),
        ],
        out_specs=pl.BlockSpec((None, tm, tn), lambda bi, j, i: (bi, i, j)),
        out_shape=jax.ShapeDtypeStruct((b, n2, w_hy), BF16),
        scratch_shapes=[pltpu.VMEM((l, tn), BF16)],
        compiler_params=_params(("arbitrary", "arbitrary", "arbitrary")),
        name="hyena_dft_fwd",
    )(a_mat, z, fspec)


def _dft_inv_kernel(at_ref, y_ref, z_ref, g_ref, d_ref, o_ref, *, inv_l):
    y = jnp.dot(at_ref[...], y_ref[...], preferred_element_type=F32) * inv_l
    z = z_ref[...]
    o_ref[...] = (g_ref[...] * (y + d_ref[...] * z)).astype(o_ref.dtype)


def _dft_inv(at_mat, y, z, z_col0, gate, gate_col0, d, layer, order, b, l, w_hy, out_dtype):
    n2 = at_mat.shape[1]
    tm = _pick(l, (512, 256, 128, 64))
    tn = _pick(w_hy, (1024, 512, 256, 128))
    nt = l // tm
    assert z_col0 % tn == 0 and gate_col0 % tn == 0
    return pl.pallas_call(
        functools.partial(_dft_inv_kernel, inv_l=1.0 / l),
        grid=(b, w_hy // tn, nt),
        in_specs=[
            pl.BlockSpec((tm, n2), lambda bi, j, i: (i, 0)),
            pl.BlockSpec((None, n2, tn), lambda bi, j, i: (bi, 0, j)),
            pl.BlockSpec((tm, tn), lambda bi, j, i: (bi * nt + i, z_col0 // tn + j)),
            pl.BlockSpec((tm, tn), lambda bi, j, i: (bi * nt + i, gate_col0 // tn + j)),
            pl.BlockSpec((None, None, 1, tn), lambda bi, j, i: (layer, order, 0, j)),
        ],
        out_specs=pl.BlockSpec((tm, tn), lambda bi, j, i: (bi * nt + i, j)),
        out_shape=jax.ShapeDtypeStruct((b * l, w_hy), out_dtype),
        compiler_params=_params(("arbitrary", "arbitrary", "arbitrary")),
        name="hyena_dft_inv",
    )(at_mat, y, z, gate, d)


def _hyena_fused_kernel(*refs, n_order, pad_left, tile, inv_l):
    it = iter(refs)
    a_ref, at_ref, f_ref, d_ref = next(it), next(it), next(it), next(it)
    x_refs = [next(it) for _ in range(n_order + 1)]
    w_refs = [next(it) for _ in range(n_order + 1)]
    b_refs = [next(it) for _ in range(n_order + 1)]
    o_ref = next(it)
    half = tile // 2
    n_tiles = a_ref.shape[0] // tile
    conv = lambda n: _dwconv(x_refs[n][...], w_refs[n][...], b_refs[n][...], pad_left)
    z = conv(0)
    for o in range(n_order):
        zb = z.astype(BF16)
        y = None
        for ti in range(n_tiles):
            r0 = ti * tile
            s = jnp.dot(a_ref[r0:r0 + tile, :], zb, preferred_element_type=F32)
            s_re, s_im = s[:half], s[half:]
            f_re, f_im = f_ref[o, r0:r0 + half, :], f_ref[o, r0 + half:r0 + tile, :]
            spec = jnp.concatenate([s_re * f_re - s_im * f_im, s_re * f_im + s_im * f_re], axis=0).astype(BF16)
            t = jnp.dot(at_ref[:, r0:r0 + tile], spec, preferred_element_type=F32)
            y = t if y is None else y + t
        z = conv(o + 1) * (y * inv_l + d_ref[o] * z)
    o_ref[...] = z.astype(o_ref.dtype)


def _hyena_fused(proj, tok0, b, l, col0, w_hy, n_order, a_mat, at_mat, fspec, conv_w, conv_b, hy_d, layer):
    n2 = a_mat.shape[0]
    tn = _pick(w_hy, (512, 256, 128))
    nj = w_hy // tn
    k = conv_w.shape[1]
    assert tok0 % l == 0 and col0 % tn == 0
    col = lambda n: (col0 + n * w_hy) // tn
    x_specs = [pl.BlockSpec((l, tn), lambda j, bi, n=n: (tok0 // l + bi, col(n) + j)) for n in range(n_order + 1)]
    w_specs = [pl.BlockSpec((None, k, tn), lambda j, bi, n=n: (layer, 0, n * nj + j)) for n in range(n_order + 1)]
    b_specs = [pl.BlockSpec((None, 1, tn), lambda j, bi, n=n: (layer, 0, n * nj + j)) for n in range(n_order + 1)]
    return pl.pallas_call(
        functools.partial(_hyena_fused_kernel, n_order=n_order, pad_left=k // 2, tile=_dft_tile(l), inv_l=1.0 / l),
        grid=(nj, b),
        in_specs=[
            pl.BlockSpec((n2, l), lambda j, bi: (0, 0)),
            pl.BlockSpec((l, n2), lambda j, bi: (0, 0)),
            pl.BlockSpec((n_order, n2, tn), lambda j, bi: (0, 0, j)),
            pl.BlockSpec((None, n_order, 1, tn), lambda j, bi: (layer, 0, 0, j)),
            *x_specs, *w_specs, *b_specs,
        ],
        out_specs=pl.BlockSpec((l, tn), lambda j, bi: (bi, j)),
        out_shape=jax.ShapeDtypeStruct((b * l, w_hy), BF16),
        compiler_params=_params(("arbitrary", "arbitrary")),
        name="hyena_fused",
    )(a_mat, at_mat, fspec, hy_d, *([proj] * (n_order + 1)), *([conv_w] * (n_order + 1)),
      *([conv_b] * (n_order + 1)))


HYENA_FUSED_MAX_L = 512


def _hyena(proj, tok0, b, l, col0, w_hy, n_order, a_mat, at_mat, fspec, conv_w, conv_b, hy_d, layer):
    if l <= HYENA_FUSED_MAX_L:
        return _hyena_fused(proj, tok0, b, l, col0, w_hy, n_order, a_mat, at_mat, fspec, conv_w, conv_b, hy_d, layer)
    hyc = _hy_conv(proj, tok0, b, l, col0, (n_order + 1) * w_hy, conv_w, conv_b, layer)
    z, z_col0 = hyc, 0
    for o in range(n_order):
        y = _dft_fwd(a_mat, z, z_col0, fspec, o, b, l, w_hy)
        last = o == n_order - 1
        z = _dft_inv(at_mat, y, z, z_col0, hyc, (o + 1) * w_hy, hy_d, layer, o, b, l, w_hy, BF16 if last else F32)
        z_col0 = 0
    return z


def _merge_kernel(x_ref, mod_ref, g_ref, wg0_ref, wg1_ref, wg2_ref, b0_ref, b1_ref, b2_ref,
                  wb0_ref, wb1_ref, wb2_ref, o_ref, h_scr):
    @pl.when(pl.program_id(1) == 0)
    def _():
        h = _rms_mod(x_ref[...], g_ref[...], mod_ref[1:2, :], mod_ref[0:1, :])
        h_scr[...] = h.astype(BF16)

    h = h_scr[...]
    acc = None
    for wg_ref, b_ref, wb_ref in ((wg0_ref, b0_ref, wb0_ref), (wg1_ref, b1_ref, wb1_ref), (wg2_ref, b2_ref, wb2_ref)):
        gate = jax.nn.sigmoid(jnp.dot(h, wg_ref[...], preferred_element_type=F32))
        t = gate * jnp.dot(b_ref[...], wb_ref[...], preferred_element_type=F32)
        acc = t if acc is None else acc + t
    o_ref[...] = acc.astype(o_ref.dtype)


def _merge(x, mod, g, w_gate, branches, w_br, layer, rpm):
    m, d = x.shape
    wb = w_br.shape[2]
    tm = _row_tile(rpm)
    tn = _pick(d, (256, 128))
    nj = d // tn
    gate_spec = lambda n: pl.BlockSpec((None, d, tn), lambda i, j: (layer, 0, n * nj + j))
    br_spec = pl.BlockSpec((tm, wb), lambda i, j: (i, 0))
    wbr_spec = lambda n: pl.BlockSpec((None, None, wb, tn), lambda i, j: (layer, n, 0, j))
    return pl.pallas_call(
        _merge_kernel,
        grid=(m // tm, nj),
        in_specs=[
            pl.BlockSpec((tm, d), lambda i, j: (i, 0)),
            pl.BlockSpec((None, 6, d), lambda i, j: (_mod_row(i, tm, rpm), 0, 0)),
            pl.BlockSpec((None, 1, d), lambda i, j: (layer, 0, 0)),
            gate_spec(0), gate_spec(1), gate_spec(2),
            br_spec, br_spec, br_spec,
            wbr_spec(0), wbr_spec(1), wbr_spec(2),
        ],
        out_specs=pl.BlockSpec((tm, tn), lambda i, j: (i, j)),
        out_shape=jax.ShapeDtypeStruct((m, d), BF16),
        scratch_shapes=[pltpu.VMEM((tm, d), BF16)],
        compiler_params=_params(("arbitrary", "arbitrary")),
        name="gated_merge",
    )(x, mod, g, w_gate, w_gate, w_gate, *branches, w_br, w_br, w_br)


def _proj_residual_kernel(a_ref, w_ref, x_ref, mod_ref, o_ref, *, gate_row):
    gate = mod_ref[gate_row:gate_row + 1, :]
    o_ref[...] = x_ref[...] + gate * jnp.dot(a_ref[...], w_ref[...], preferred_element_type=F32)


def _proj_residual(a, w, x, mod, gate_row, layer, rpm):
    m, k = a.shape
    d = x.shape[1]
    tm = _row_tile(rpm)
    tn = _pick(d, (512, 256, 128))
    return pl.pallas_call(
        functools.partial(_proj_residual_kernel, gate_row=gate_row),
        grid=(m // tm, d // tn),
        in_specs=[
            pl.BlockSpec((tm, k), lambda i, j: (i, 0)),
            pl.BlockSpec((None, k, tn), lambda i, j: (layer, 0, j)),
            pl.BlockSpec((tm, tn), lambda i, j: (i, j)),
            pl.BlockSpec((None, 6, tn), lambda i, j: (_mod_row(i, tm, rpm), 0, j)),
        ],
        out_specs=pl.BlockSpec((tm, tn), lambda i, j: (i, j)),
        out_shape=jax.ShapeDtypeStruct((m, d), F32),
        compiler_params=_params(("arbitrary", "arbitrary")),
        name="proj_residual",
    )(a, w, x, mod)


def _ffn_up_kernel(x_ref, mod_ref, g_ref, wg_ref, wu_ref, o_ref, h_scr):
    @pl.when(pl.program_id(1) == 0)
    def _():
        h = _rms_mod(x_ref[...], g_ref[...], mod_ref[4:5, :], mod_ref[3:4, :])
        h_scr[...] = h.astype(BF16)

    h = h_scr[...]
    a = jnp.dot(h, wg_ref[...], preferred_element_type=F32)
    u = jnp.dot(h, wu_ref[...], preferred_element_type=F32)
    o_ref[...] = ((a * jax.nn.sigmoid(a)) * u).astype(o_ref.dtype)


def _ffn_up(x, mod, g, w_gate, w_up, layer, rpm):
    m, d = x.shape
    n = w_gate.shape[-1]
    tm = _row_tile(rpm)
    tn = _pick(n, (512, 256, 128))
    w_spec = pl.BlockSpec((None, d, tn), lambda i, j: (layer, 0, j))
    return pl.pallas_call(
        _ffn_up_kernel,
        grid=(m // tm, n // tn),
        in_specs=[
            pl.BlockSpec((tm, d), lambda i, j: (i, 0)),
            pl.BlockSpec((None, 6, d), lambda i, j: (_mod_row(i, tm, rpm), 0, 0)),
            pl.BlockSpec((None, 1, d), lambda i, j: (layer, 0, 0)),
            w_spec, w_spec,
        ],
        out_specs=pl.BlockSpec((tm, tn), lambda i, j: (i, j)),
        out_shape=jax.ShapeDtypeStruct((m, n), BF16),
        scratch_shapes=[pltpu.VMEM((tm, d), BF16)],
        compiler_params=_params(("arbitrary", "arbitrary")),
        name="ffn_up",
    )(x, mod, g, w_gate, w_up)


def _final_norm_kernel(x_ref, g_ref, o_ref):
    x = x_ref[...]
    o_ref[...] = (x * lax.rsqrt(jnp.mean(x * x, axis=-1, keepdims=True) + NORM_EPS)) * g_ref[...]


def _final_norm(x, tok0, n_tok, g):
    d = x.shape[1]
    tm = _pick(math.gcd(tok0, n_tok) if tok0 else n_tok, (512, 256, 128, 64))
    return pl.pallas_call(
        _final_norm_kernel,
        grid=(n_tok // tm,),
        in_specs=[pl.BlockSpec((tm, d), lambda i: (tok0 // tm + i, 0)), pl.BlockSpec((1, d), lambda i: (0, 0))],
        out_specs=pl.BlockSpec((tm, d), lambda i: (i, 0)),
        out_shape=jax.ShapeDtypeStruct((n_tok, d), F32),
        compiler_params=_params(("arbitrary",)),
        name="final_norm",
    )(x, g)


def kernel(x_prompt, x_sample, c, cache_k, cache_v, state_lru, c_ctx, w_mod, b_mod, norm_mix, norm_ffn, w_in, w_gate, att_lambda, att_subln, lru_conv_w, lru_conv_b, lru_wa, lru_ba, lru_wx, lru_bx, lru_lambda, hy_conv_w, hy_conv_b, hy_w1, hy_b1, hy_w2, hy_b2, hy_w3, hy_b3, hy_freq, hy_decay, hy_d, w_br, w_out, w_ff_gate, w_ff_up, w_ff_down, final_norm):
    b_c, l_c, d = x_prompt.shape
    b_s, l_s, _ = x_sample.shape
    depth = w_in.shape[0]
    n_heads, d_v = cache_v.shape[3], cache_v.shape[4]
    w_att = n_heads * d_v
    w_lru = lru_lambda.shape[-1]
    n_order, w_hy = hy_d.shape[1], hy_d.shape[2]
    m_ctx, m_s = b_c * l_c, b_s * l_s
    past = cache_k.shape[2]
    col_xl, col_gl, col_hy = 3 * w_att, 3 * w_att + w_lru, 3 * w_att + 2 * w_lru

    w_in_b, w_gate_b, w_br_b, w_out_b = (w.astype(BF16) for w in (w_in, w_gate, w_br, w_out))
    w_ffg_b, w_ffu_b, w_ffd_b = (w.astype(BF16) for w in (w_ff_gate, w_ff_up, w_ff_down))
    lru_wg = _lru_gate_weights(lru_wa, lru_wx)
    lru_bg = jnp.stack([lru_ba[:, 0], lru_bx[:, 0], lru_ba[:, 1], lru_bx[:, 1]], axis=1)
    norm_mix3, norm_ffn3 = norm_mix.reshape(depth, 1, d), norm_ffn.reshape(depth, 1, d)
    att_subln3 = att_subln.reshape(depth, 1, d_v)
    lru_conv_b3 = lru_conv_b.reshape(depth, 1, w_lru)
    hy_conv_b3 = hy_conv_b.reshape(depth, 1, -1)
    hy_d4 = hy_d.reshape(depth, n_order, 1, w_hy)

    n_rows = 1 + b_s
    assert n_rows <= SUBLANES
    c8 = jnp.zeros((SUBLANES, d), F32).at[0].set(c_ctx).at[1:n_rows].set(c)
    mod_all = _modulation(c8, w_mod, b_mod)[:, :n_rows].reshape(depth, n_rows, 6, d)

    rope = _rope_tables(l_s, d_v // 2)
    dft = {}
    for l in sorted({l_c, l_s}):
        a_mat = jnp.asarray(_dft_matrices(l)).astype(BF16)
        dft[l] = (a_mat, a_mat.T)
    h0_ctx = jnp.zeros((b_c, 2, w_lru), F32)

    def trunk_layer(x, mod, rpm, layer, b, l, rope, cache, h0, want_kv):
        lam_init = 0.8 - 0.6 * math.exp(-0.3 * layer)
        proj, *kv = _norm_proj(x, mod, norm_mix3, w_in_b, layer, rpm, w_att if want_kv else None)
        att = _attention(proj, 0, b, l, n_heads, d_v, lam_init, att_lambda, att_subln3, layer, rope=rope, cache=cache)
        lru, st = _lru(proj, 0, b, l, col_xl, col_gl, w_lru, lru_conv_w, lru_conv_b3, lru_wg, lru_bg, lru_lambda,
                       h0, layer)
        a_mat, at_mat = dft[l]
        filt = _hy_filters(l, hy_w1[layer], hy_b1[layer], hy_freq[layer], hy_w2[layer], hy_b2[layer],
                           hy_w3[layer], hy_b3[layer], hy_decay[layer])
        fspec = _filter_spectrum(a_mat, filt, n_order, w_hy)
        hyo = _hyena(proj, 0, b, l, col_hy, w_hy, n_order, a_mat, at_mat, fspec, hy_conv_w, hy_conv_b3, hy_d4, layer)
        mixed = _merge(x, mod, norm_mix3, w_gate_b, (att, lru, hyo), w_br_b, layer, rpm)
        x = _proj_residual(mixed, w_out_b, x, mod, 2, layer, rpm)
        u = _ffn_up(x, mod, norm_ffn3, w_ffg_b, w_ffu_b, layer, rpm)
        x = _proj_residual(u, w_ffd_b, x, mod, 5, layer, rpm)
        return x, kv, st

    xp = x_prompt.reshape(m_ctx, d)
    ks, vs, ss = [], [], []
    for layer in range(depth):
        xp, (k_l, v_l), s_l = trunk_layer(xp, mod_all[layer, :1], m_ctx, layer, b_c, l_c, None, None, h0_ctx, True)
        ks.append(k_l.reshape(b_c, l_c, n_heads, d_v))
        vs.append(v_l.reshape(b_c, l_c, n_heads, d_v))
        ss.append(s_l)

    xs = x_sample.reshape(m_s, d)
    for layer in range(depth):
        ck = cache_k[:, layer].reshape(b_s, past, w_att)
        cv = cache_v[:, layer].reshape(b_s, past, w_att)
        xs, _, _ = trunk_layer(xs, mod_all[layer, 1:], l_s, layer, b_s, l_s, rope, (ck, cv), state_lru[:, layer], False)

    g_fin = final_norm.reshape(1, d)
    y_prompt = _final_norm(xp, 0, m_ctx, g_fin).reshape(b_c, l_c, d)
    y_sample = _final_norm(xs, 0, m_s, g_fin).reshape(b_s, l_s, d)
    return (y_prompt, y_sample, jnp.stack(ks, axis=1), jnp.stack(vs, axis=1), jnp.stack(ss, axis=1))
```

```python
import functools
import math

import jax
import jax.numpy as jnp
import numpy as np
from jax import lax
from jax.experimental import pallas as pl
from jax.experimental.pallas import tpu as pltpu

F32 = jnp.float32
BF16 = jnp.bfloat16

GRID_W = 64
ROPE_THETA = 10000.0
NORM_EPS = 1e-6
SUBLN_EPS = 1e-5
LRU_C = 8.0
HY_BANDS = 8
LANES = 128
SUBLANES = 8
VMEM_LIMIT_BYTES = 56 * 1024 * 1024


def _pick(n, prefs):
    for p in prefs:
        if n % p == 0:
            return p
    raise ValueError(f"no tile in {prefs} divides {n}")


def _params(sem):
    return pltpu.CompilerParams(dimension_semantics=sem, vmem_limit_bytes=VMEM_LIMIT_BYTES)


def _row_tile(rpm):
    return _pick(rpm, (1024, 512, 256, 128))


def _mod_row(i, tm, rpm):
    return i // (rpm // tm)


def _rms_mod(x, g, sc, sh):
    y = x * lax.rsqrt(jnp.mean(x * x, axis=-1, keepdims=True) + NORM_EPS)
    return (y * g) * (1.0 + sc) + sh


def _mod_kernel(c_ref, w_ref, b_ref, o_ref):
    c = c_ref[...]
    s = (c * jax.nn.sigmoid(c)).astype(BF16)
    o_ref[...] = jnp.dot(s, w_ref[...].astype(BF16), preferred_element_type=F32) + b_ref[...]


def _modulation(c8, w_mod, b_mod):
    depth, d, n = w_mod.shape
    tn = _pick(n, (1024, 512, 256, 128))
    return pl.pallas_call(
        _mod_kernel,
        grid=(depth, n // tn),
        in_specs=[
            pl.BlockSpec((SUBLANES, d), lambda l, j: (0, 0)),
            pl.BlockSpec((None, d, tn), lambda l, j: (l, 0, j)),
            pl.BlockSpec((None, 1, tn), lambda l, j: (l, 0, j)),
        ],
        out_specs=pl.BlockSpec((None, SUBLANES, tn), lambda l, j: (l, 0, j)),
        out_shape=jax.ShapeDtypeStruct((depth, SUBLANES, n), F32),
        compiler_params=_params(("arbitrary", "arbitrary")),
        name="modulation",
    )(c8, w_mod, b_mod.reshape(depth, 1, n))


def _norm_proj_kernel(x_ref, mod_ref, g_ref, w_ref, o_ref, *rest, kv_tiles):
    h_scr = rest[-1]
    j = pl.program_id(1)

    @pl.when(j == 0)
    def _():
        h = _rms_mod(x_ref[...], g_ref[...], mod_ref[1:2, :], mod_ref[0:1, :])
        h_scr[...] = h.astype(BF16)

    y = jnp.dot(h_scr[...], w_ref[...], preferred_element_type=F32)
    o_ref[...] = y
    if kv_tiles:
        k_ref, v_ref = rest[0], rest[1]

        @pl.when((j >= kv_tiles) & (j < 2 * kv_tiles))
        def _():
            k_ref[...] = y

        @pl.when((j >= 2 * kv_tiles) & (j < 3 * kv_tiles))
        def _():
            v_ref[...] = y


def _norm_proj(x, mod, g, w, layer, rpm, w_att=None):
    m, d = x.shape
    n = w.shape[-1]
    tm = _row_tile(rpm)
    tn = _pick(math.gcd(n, w_att or n), (512, 256, 128) if w_att else (1024, 512, 256, 128))
    out_specs = [pl.BlockSpec((tm, tn), lambda i, j: (i, j))]
    out_shape = [jax.ShapeDtypeStruct((m, n), F32)]
    kv_tiles = 0
    if w_att is not None:
        kv_tiles = w_att // tn
        out_specs += [pl.BlockSpec((tm, tn), lambda i, j, t=t: (i, jnp.clip(j - t * kv_tiles, 0, kv_tiles - 1)))
                      for t in (1, 2)]
        out_shape += [jax.ShapeDtypeStruct((m, w_att), F32)] * 2
    return pl.pallas_call(
        functools.partial(_norm_proj_kernel, kv_tiles=kv_tiles),
        grid=(m // tm, n // tn),
        in_specs=[
            pl.BlockSpec((tm, d), lambda i, j: (i, 0)),
            pl.BlockSpec((None, 6, d), lambda i, j: (_mod_row(i, tm, rpm), 0, 0)),
            pl.BlockSpec((None, 1, d), lambda i, j: (layer, 0, 0)),
            pl.BlockSpec((None, d, tn), lambda i, j: (layer, 0, j)),
        ],
        out_specs=out_specs,
        out_shape=out_shape,
        scratch_shapes=[pltpu.VMEM((tm, d), BF16)],
        compiler_params=_params(("arbitrary", "arbitrary")),
        name="norm_proj",
    )(x, mod, g, w)


def _rope(x, cos, sin_signed):
    n = x.shape[-1]
    half = 16
    lane = lax.broadcasted_iota(jnp.int32, x.shape, 1)
    swapped = jnp.where((lane % (2 * half)) < half, pltpu.roll(x, n - half, 1), pltpu.roll(x, half, 1))
    return x * cos + swapped * sin_signed


def _attn_kernel(*refs, lam_init, d_qk, n_hb, use_rope, use_cache):
    it = iter(refs)
    lam_ref, sub_ref, q_ref, k_ref, v_ref = next(it), next(it), next(it), next(it), next(it)
    if use_rope:
        cq_ref, sq_ref, ck_ref, sk_ref = next(it), next(it), next(it), next(it)
    if use_cache:
        pk_ref, pv_ref = next(it), next(it)
    o_ref = next(it)
    k_scr = next(it)
    d_v = 2 * d_qk
    heads = [slice(hh * d_v, (hh + 1) * d_v) for hh in range(n_hb)]

    @pl.when(pl.program_id(2) == 0)
    def _():
        for sl in heads:
            k = k_ref[:, sl]
            if use_rope:
                k = _rope(k, ck_ref[...], sk_ref[...])
            k_scr[:, sl] = k.astype(BF16)

    a = lam_ref[...]
    lam = (jnp.exp(jnp.sum(a[0:1] * a[1:2], axis=-1, keepdims=True))
           - jnp.exp(jnp.sum(a[2:3] * a[3:4], axis=-1, keepdims=True)) + lam_init)
    scale = d_qk ** -0.5
    nt = (((1,), (1,)), ((), ()))

    for sl in heads:
        q = q_ref[:, sl]
        if use_rope:
            q = _rope(q, cq_ref[...], sq_ref[...])
        q = q * scale
        lane = lax.broadcasted_iota(jnp.int32, q.shape, 1)
        q1 = jnp.where(lane < d_qk, q, 0.0).astype(BF16)
        q2 = jnp.where(lane >= d_qk, q, 0.0).astype(BF16)
        keys = [k_scr[:, sl]]
        vals = [v_ref[:, sl].astype(BF16)]
        if use_cache:
            keys.append(pk_ref[:, sl].astype(BF16))
            vals.append(pv_ref[:, sl].astype(BF16))

        def softmax_v(qh):
            s = [lax.dot_general(qh, kk, nt, preferred_element_type=F32) for kk in keys]
            mx = functools.reduce(jnp.maximum, [jnp.max(x, axis=-1, keepdims=True) for x in s])
            p = [jnp.exp(x - mx) for x in s]
            den = functools.reduce(jnp.add, [jnp.sum(x, axis=-1, keepdims=True) for x in p])
            pv = functools.reduce(jnp.add, [jnp.dot(x.astype(BF16), vv, preferred_element_type=F32)
                                            for x, vv in zip(p, vals)])
            return pv, 1.0 / den

        o1, r1 = softmax_v(q1)
        o2, r2 = softmax_v(q2)
        o = o1 * r1 - o2 * (lam * r2)
        y = o * lax.rsqrt(jnp.mean(o * o, axis=-1, keepdims=True) + SUBLN_EPS)
        o_ref[:, sl] = ((y * sub_ref[...]) * (1.0 - lam_init)).astype(o_ref.dtype)


def _attention(proj, b, l, h, d_v, lam_init, att_lambda, att_subln, layer, rope=None, cache=None):
    assert d_v == LANES
    tq = _pick(l, (256, 128, 64))
    nq = l // tq
    n_hb = max(n for n in range(1, h + 1) if h % n == 0 and n * l <= max(l, 1024))
    wb = n_hb * d_v
    hg = h // n_hb
    in_specs = [
        pl.BlockSpec((None, 4, att_lambda.shape[-1]), lambda bi, hi, qi: (layer, 0, 0)),
        pl.BlockSpec((None, 1, d_v), lambda bi, hi, qi: (layer, 0, 0)),
        pl.BlockSpec((tq, wb), lambda bi, hi, qi: (bi * nq + qi, hi)),
        pl.BlockSpec((l, wb), lambda bi, hi, qi: (bi, hg + hi)),
        pl.BlockSpec((l, wb), lambda bi, hi, qi: (bi, 2 * hg + hi)),
    ]
    args = [att_lambda, att_subln, proj, proj, proj]
    if rope is not None:
        cos, sin = rope
        in_specs += [
            pl.BlockSpec((tq, d_v), lambda bi, hi, qi: (qi, 0)),
            pl.BlockSpec((tq, d_v), lambda bi, hi, qi: (qi, 0)),
            pl.BlockSpec((l, d_v), lambda bi, hi, qi: (0, 0)),
            pl.BlockSpec((l, d_v), lambda bi, hi, qi: (0, 0)),
        ]
        args += [cos, sin, cos, sin]
    if cache is not None:
        ck, cv = cache
        past = ck.shape[1]
        in_specs += [
            pl.BlockSpec((None, past, wb), lambda bi, hi, qi: (bi, 0, hi)),
            pl.BlockSpec((None, past, wb), lambda bi, hi, qi: (bi, 0, hi)),
        ]
        args += [ck, cv]
    return pl.pallas_call(
        functools.partial(_attn_kernel, lam_init=lam_init, d_qk=d_v // 2, n_hb=n_hb,
                          use_rope=rope is not None, use_cache=cache is not None),
        grid=(b, hg, nq),
        in_specs=in_specs,
        out_specs=pl.BlockSpec((tq, wb), lambda bi, hi, qi: (bi * nq + qi, hi)),
        out_shape=jax.ShapeDtypeStruct((b * l, h * d_v), BF16),
        scratch_shapes=[pltpu.VMEM((l, wb), BF16)],
        compiler_params=_params(("arbitrary", "arbitrary", "arbitrary")),
        name="diff_attention",
    )(*args)


def _rope_tables(l, d_qk):
    n_freq = d_qk // 4
    lane = jnp.arange(2 * d_qk)
    axis = (lane % d_qk) // (2 * n_freq)
    freq = lane % n_freq
    first = (lane % (2 * n_freq)) < n_freq
    inv_freq = ROPE_THETA ** (-freq.astype(F32) / n_freq)
    t = jnp.arange(l)
    pos = jnp.where(axis[None, :] == 0, (t // GRID_W)[:, None], (t % GRID_W)[:, None]).astype(F32)
    ang = pos * inv_freq[None, :]
    return jnp.cos(ang), jnp.where(first[None, :], -jnp.sin(ang), jnp.sin(ang))


def _shift_rows(x, d):
    n = x.shape[0]
    if d == 0:
        return x
    row = lax.broadcasted_iota(jnp.int32, x.shape, 0)
    rolled = pltpu.roll(x, (-d) % n, 0)
    ok = (row + d >= 0) & (row + d < n)
    return jnp.where(ok, rolled, 0.0)


def _dwconv(x, w, bias, pad_left):
    y = bias + w[0:1] * _shift_rows(x, -pad_left)
    for k in range(1, w.shape[0]):
        y = y + w[k:k + 1] * _shift_rows(x, k - pad_left)
    return y


def _sigmoid(x):
    return 0.5 * jnp.tanh(0.5 * x) + 0.5


SCAN_UNROLL = 8


def _lru_kernel(xl_ref, gl_ref, cw_ref, cb_ref, wg_ref, bg_ref, lam_ref, h0_ref, o_ref, st_ref,
                a_scr, b_scr, hf_scr, *, pad_left):
    l, cb = xl_ref.shape
    n_grp = cb // LANES
    xc = _dwconv(xl_ref[...], cw_ref[...], cb_ref[...], pad_left)
    lam = lam_ref[...]
    neg = -lam
    softplus = jnp.maximum(neg, 0.0) + jnp.log1p(jnp.exp(-jnp.abs(neg)))
    row8 = lax.broadcasted_iota(jnp.int32, (SUBLANES, cb), 0)
    n_chunks = l // SUBLANES

    def fill(direction):
        for g in range(n_grp):
            sl = slice(g * LANES, (g + 1) * LANES)
            xg = xc[:, sl]
            y = jnp.dot(xg.astype(BF16), wg_ref[g, :, 2 * direction * LANES:(2 * direction + 2) * LANES],
                        preferred_element_type=F32)
            r = _sigmoid(y[:, :LANES] + bg_ref[2 * direction:2 * direction + 1, sl])
            i = _sigmoid(y[:, LANES:] + bg_ref[2 * direction + 1:2 * direction + 2, sl])
            log_a = (-LRU_C * r) * softplus[direction:direction + 1, sl]
            th = jnp.tanh(log_a)
            one_minus_a2 = (-2.0 * th) / (1.0 - th)
            a_scr[:, sl] = jnp.exp(log_a)
            b_scr[:, sl] = jnp.sqrt(one_minus_a2) * (i * xg)

    fill(0)

    def fwd(c, h):
        r0 = pl.multiple_of(c * SUBLANES, SUBLANES)
        a = a_scr[pl.ds(r0, SUBLANES), :]
        b = b_scr[pl.ds(r0, SUBLANES), :]
        for s in (1, 2, 4):
            keep = row8 >= s
            b = jnp.where(keep, a * pltpu.roll(b, s, 0) + b, b)
            a = jnp.where(keep, a * pltpu.roll(a, s, 0), a)
        h8 = a * h + b
        hf_scr[pl.ds(r0, SUBLANES), :] = h8
        return h8[SUBLANES - 1:SUBLANES, :]

    s_f = lax.fori_loop(0, n_chunks, fwd, h0_ref[0:1, :], unroll=SCAN_UNROLL)

    fill(1)

    def bwd(c, h):
        r0 = pl.multiple_of((n_chunks - 1 - c) * SUBLANES, SUBLANES)
        a = a_scr[pl.ds(r0, SUBLANES), :]
        b = b_scr[pl.ds(r0, SUBLANES), :]
        for s in (1, 2, 4):
            keep = row8 < SUBLANES - s
            b = jnp.where(keep, a * pltpu.roll(b, SUBLANES - s, 0) + b, b)
            a = jnp.where(keep, a * pltpu.roll(a, SUBLANES - s, 0), a)
        h8 = a * h + b
        b_scr[pl.ds(r0, SUBLANES), :] = h8
        return h8[0:1, :]

    s_b = lax.fori_loop(0, n_chunks, bwd, h0_ref[1:2, :], unroll=SCAN_UNROLL)

    st_ref[0:1, :] = s_f
    st_ref[1:2, :] = s_b
    o_ref[...] = ((hf_scr[...] + b_scr[...]) * jax.nn.gelu(gl_ref[...])).astype(o_ref.dtype)


def _lru(proj, tok0, b, l, col_x, col_g, w_lru, conv_w, conv_b, wg, bg, lam, h0, layer):
    cb = _pick(w_lru, tuple(c for c in (1024, 512, 256, 128) if c == 128 or l * c <= 512 * 1024))
    assert tok0 % l == 0 and col_x % cb == 0 and col_g % cb == 0
    k = conv_w.shape[1]
    n_grp = cb // LANES
    out, st = pl.pallas_call(
        functools.partial(_lru_kernel, pad_left=k // 2),
        grid=(b, w_lru // cb),
        in_specs=[
            pl.BlockSpec((l, cb), lambda bi, j: (tok0 // l + bi, col_x // cb + j)),
            pl.BlockSpec((l, cb), lambda bi, j: (tok0 // l + bi, col_g // cb + j)),
            pl.BlockSpec((None, k, cb), lambda bi, j: (layer, 0, j)),
            pl.BlockSpec((None, 1, cb), lambda bi, j: (layer, 0, j)),
            pl.BlockSpec((None, n_grp, LANES, 4 * LANES), lambda bi, j: (layer, j, 0, 0)),
            pl.BlockSpec((None, 4, cb), lambda bi, j: (layer, 0, j)),
            pl.BlockSpec((None, 2, cb), lambda bi, j: (layer, 0, j)),
            pl.BlockSpec((None, 2, cb), lambda bi, j: (bi, 0, j)),
        ],
        out_specs=[
            pl.BlockSpec((l, cb), lambda bi, j: (bi, j)),
            pl.BlockSpec((None, 2, cb), lambda bi, j: (bi, 0, j)),
        ],
        out_shape=[jax.ShapeDtypeStruct((b * l, w_lru), BF16), jax.ShapeDtypeStruct((b, 2, w_lru), F32)],
        scratch_shapes=[pltpu.VMEM((l, cb), F32)] * 3,
        compiler_params=_params(("arbitrary", "arbitrary")),
        name="rglru",
    )(proj, proj, conv_w, conv_b, wg, bg, lam, h0)
    return out, st


def _lru_gate_weights(wa, wx):
    depth, _, nb, bs, _ = wa.shape
    per = LANES // bs
    eye = jnp.eye(per, dtype=wa.dtype)

    def bd(w):
        w = w.reshape(depth, nb // per, per, bs, bs)
        return jnp.einsum("lgpde,pq->lgpdqe", w, eye).reshape(depth, nb // per, LANES, LANES)

    return jnp.concatenate([bd(wa[:, 0]), bd(wx[:, 0]), bd(wa[:, 1]), bd(wx[:, 1])], axis=-1).astype(BF16)


def _hy_conv_kernel(x_ref, w_ref, b_ref, o_ref, *, pad_left):
    o_ref[...] = _dwconv(x_ref[...], w_ref[...], b_ref[...], pad_left)


def _hy_conv(proj, tok0, b, l, col0, width, conv_w, conv_b, layer):
    cb = _pick(width, (512, 256, 128))
    assert tok0 % l == 0 and col0 % cb == 0
    k = conv_w.shape[1]
    return pl.pallas_call(
        functools.partial(_hy_conv_kernel, pad_left=k // 2),
        grid=(b, width // cb),
        in_specs=[
            pl.BlockSpec((l, cb), lambda bi, j: (tok0 // l + bi, col0 // cb + j)),
            pl.BlockSpec((None, k, cb), lambda bi, j: (layer, 0, j)),
            pl.BlockSpec((None, 1, cb), lambda bi, j: (layer, 0, j)),
        ],
        out_specs=pl.BlockSpec((l, cb), lambda bi, j: (bi, j)),
        out_shape=jax.ShapeDtypeStruct((b * l, width), F32),
        compiler_params=_params(("arbitrary", "arbitrary")),
        name="hyena_dwconv",
    )(proj, conv_w, conv_b)


def _hy_filter_kernel(z_ref, w1_ref, b1_ref, fr_ref, w2_ref, b2_ref, w3_ref, b3_ref, dec_ref, o_ref, h_scr):
    hi = lax.Precision.HIGHEST

    @pl.when(pl.program_id(0) == 0)
    def _():
        fr = fr_ref[...]
        h = jnp.sin(fr * (jnp.dot(z_ref[...], w1_ref[...], precision=hi, preferred_element_type=F32) + b1_ref[...]))
        h_scr[...] = jnp.sin(fr * (jnp.dot(h, w2_ref[...], precision=hi, preferred_element_type=F32) + b2_ref[...]))

    filt = jnp.dot(h_scr[...], w3_ref[...], precision=hi, preferred_element_type=F32) + b3_ref[...]
    o_ref[...] = filt * jnp.exp(-z_ref[:, 0:1] * jnp.abs(dec_ref[...]))


def _pad2(x, rows, cols):
    return jnp.pad(x, ((0, rows - x.shape[0]), (0, cols - x.shape[1])))


def _hy_filters(l, w1, b1, freq, w2, b2, w3, b3, decay):
    n = w3.shape[-1]
    t = jnp.linspace(0.0, 1.0, l, dtype=F32)[:, None]
    w = 2.0 * math.pi * jnp.arange(l, dtype=F32)[:, None] / l
    f = jnp.linspace(1e-4, HY_BANDS - 1, HY_BANDS, dtype=F32)[None]
    z = _pad2(jnp.concatenate([t, jnp.cos(f * w), -jnp.sin(f * w)], axis=-1), l, LANES)
    tn = _pick(n, (512, 256, 128))
    row = lambda v: _pad2(v.reshape(1, -1), 1, LANES)
    full = lambda shape: pl.BlockSpec(shape, lambda j: (0, 0))
    return pl.pallas_call(
        _hy_filter_kernel,
        grid=(n // tn,),
        in_specs=[full((l, LANES)), full((LANES, LANES)), full((1, LANES)), full((1, LANES)),
                  full((LANES, LANES)), full((1, LANES)),
                  pl.BlockSpec((LANES, tn), lambda j: (0, j)),
                  pl.BlockSpec((1, tn), lambda j: (0, j)),
                  pl.BlockSpec((1, tn), lambda j: (0, j))],
        out_specs=pl.BlockSpec((l, tn), lambda j: (0, j)),
        out_shape=jax.ShapeDtypeStruct((l, n), F32),
        scratch_shapes=[pltpu.VMEM((l, LANES), F32)],
        compiler_params=_params(("arbitrary",)),
        name="hyena_filters",
    )(z, _pad2(w1, LANES, LANES), row(b1), row(freq), _pad2(w2, LANES, LANES), row(b2),
      _pad2(w3, LANES, n), b3.reshape(1, n), decay.reshape(1, n))


def _dft_tile(l):
    return _pick(2 * l, (512, 256, 128))


@functools.lru_cache(maxsize=None)
def _dft_matrices(l):
    tile = _dft_tile(l)
    half = tile // 2
    r = np.arange(2 * l)
    k = (r // tile) * half + (r % half)
    is_im = (r % tile) >= half
    t = np.arange(l)
    ph = ((2 * k + 1)[:, None] * t[None, :]) % (4 * l)
    ang = ph.astype(np.float64) * (math.pi / (2 * l))
    return np.where(is_im[:, None], -np.sin(ang), np.cos(ang)).astype(np.float32)


def _cmul_store(o_ref, s, f_re, f_im):
    half = s.shape[0] // 2
    s_re, s_im = s[:half], s[half:]
    o_ref[:half, :] = (s_re * f_re - s_im * f_im).astype(o_ref.dtype)
    o_ref[half:, :] = (s_re * f_im + s_im * f_re).astype(o_ref.dtype)


def _filter_spec_kernel(a_ref, hf_ref, hb_ref, o_ref, hf_scr, hb_scr):
    @pl.when(pl.program_id(2) == 0)
    def _():
        hf_scr[...] = hf_ref[...].astype(BF16)
        hb = hb_ref[...]
        row = lax.broadcasted_iota(jnp.int32, hb.shape, 0)
        hb_scr[...] = jnp.where(row == 0, 0.0, hb).astype(BF16)

    a = a_ref[...]
    sf = jnp.dot(a, hf_scr[...], preferred_element_type=F32)
    sb = jnp.dot(a, hb_scr[...], preferred_element_type=F32)
    half = a.shape[0] // 2
    o_ref[:half, :] = sf[:half] + sb[:half]
    o_ref[half:, :] = sf[half:] - sb[half:]


def _filter_spectrum(a_mat, filt, n_order, w_hy):
    n2, l = a_mat.shape
    tm = _dft_tile(l)
    tn = _pick(w_hy, (512, 256, 128))
    nj = w_hy // tn
    return pl.pallas_call(
        _filter_spec_kernel,
        grid=(n_order, nj, n2 // tm),
        in_specs=[
            pl.BlockSpec((tm, l), lambda o, j, i: (i, 0)),
            pl.BlockSpec((l, tn), lambda o, j, i: (0, (2 * o) * nj + j)),
            pl.BlockSpec((l, tn), lambda o, j, i: (0, (2 * o + 1) * nj + j)),
        ],
        out_specs=pl.BlockSpec((None, tm, tn), lambda o, j, i: (o, i, j)),
        out_shape=jax.ShapeDtypeStruct((n_order, n2, w_hy), F32),
        scratch_shapes=[pltpu.VMEM((l, tn), BF16)] * 2,
        compiler_params=_params(("arbitrary", "arbitrary", "arbitrary")),
        name="hyena_filter_spectrum",
    )(a_mat, filt, filt)


def _dft_fwd_kernel(a_ref, z_ref, f_ref, o_ref, z_scr):
    @pl.when(pl.program_id(2) == 0)
    def _():
        z_scr[...] = z_ref[...].astype(BF16)

    s = jnp.dot(a_ref[...], z_scr[...], preferred_element_type=F32)
    half = s.shape[0] // 2
    _cmul_store(o_ref, s, f_ref[:half, :], f_ref[half:, :])


def _dft_fwd(a_mat, z, z_col0, fspec, order, b, l, w_hy):
    n2 = a_mat.shape[0]
    tm = _dft_tile(l)
    tn = _pick(w_hy, (1024, 512, 256, 128))
    assert z_col0 % tn == 0
    return pl.pallas_call(
        _dft_fwd_kernel,
        grid=(b, w_hy // tn, n2 // tm),
        in_specs=[
            pl.BlockSpec((tm, l), lambda bi, j, i: (i, 0)),
            pl.BlockSpec((l, tn), lambda bi, j, i: (bi, z_col0 // tn + j)),
            pl.BlockSpec((None, tm, tn), lambda bi, j, i: (order, i, j)),
        ],
        out_specs=pl.BlockSpec((None, tm, tn), lambda bi, j, i: (bi, i, j)),
        out_shape=jax.ShapeDtypeStruct((b, n2, w_hy), BF16),
        scratch_shapes=[pltpu.VMEM((l, tn), BF16)],
        compiler_params=_params(("arbitrary", "arbitrary", "arbitrary")),
        name="hyena_dft_fwd",
    )(a_mat, z, fspec)


def _dft_inv_kernel(at_ref, y_ref, z_ref, g_ref, d_ref, o_ref, *, inv_l):
    y = jnp.dot(at_ref[...], y_ref[...], preferred_element_type=F32) * inv_l
    z = z_ref[...]
    o_ref[...] = (g_ref[...] * (y + d_ref[...] * z)).astype(o_ref.dtype)


def _dft_inv(at_mat, y, z, z_col0, gate, gate_col0, d, layer, order, b, l, w_hy, out_dtype):
    n2 = at_mat.shape[1]
    tm = _pick(l, (512, 256, 128, 64))
    tn = _pick(w_hy, (1024, 512, 256, 128))
    nt = l // tm
    assert z_col0 % tn == 0 and gate_col0 % tn == 0
    return pl.pallas_call(
        functools.partial(_dft_inv_kernel, inv_l=1.0 / l),
        grid=(b, w_hy // tn, nt),
        in_specs=[
            pl.BlockSpec((tm, n2), lambda bi, j, i: (i, 0)),
            pl.BlockSpec((None, n2, tn), lambda bi, j, i: (bi, 0, j)),
            pl.BlockSpec((tm, tn), lambda bi, j, i: (bi * nt + i, z_col0 // tn + j)),
            pl.BlockSpec((tm, tn), lambda bi, j, i: (bi * nt + i, gate_col0 // tn + j)),
            pl.BlockSpec((None, None, 1, tn), lambda bi, j, i: (layer, order, 0, j)),
        ],
        out_specs=pl.BlockSpec((tm, tn), lambda bi, j, i: (bi * nt + i, j)),
        out_shape=jax.ShapeDtypeStruct((b * l, w_hy), out_dtype),
        compiler_params=_params(("arbitrary", "arbitrary", "arbitrary")),
        name="hyena_dft_inv",
    )(at_mat, y, z, gate, d)


def _hyena_fused_kernel(*refs, n_order, pad_left, tile, inv_l):
    it = iter(refs)
    a_ref, at_ref, f_ref, d_ref = next(it), next(it), next(it), next(it)
    x_refs = [next(it) for _ in range(n_order + 1)]
    w_refs = [next(it) for _ in range(n_order + 1)]
    b_refs = [next(it) for _ in range(n_order + 1)]
    o_ref = next(it)
    half = tile // 2
    n_tiles = a_ref.shape[0] // tile
    conv = lambda n: _dwconv(x_refs[n][...], w_refs[n][...], b_refs[n][...], pad_left)
    z = conv(0)
    for o in range(n_order):
        zb = z.astype(BF16)
        y = None
        for ti in range(n_tiles):
            r0 = ti * tile
            s = jnp.dot(a_ref[r0:r0 + tile, :], zb, preferred_element_type=F32)
            s_re, s_im = s[:half], s[half:]
            f_re, f_im = f_ref[o, r0:r0 + half, :], f_ref[o, r0 + half:r0 + tile, :]
            spec = jnp.concatenate([s_re * f_re - s_im * f_im, s_re * f_im + s_im * f_re], axis=0).astype(BF16)
            t = jnp.dot(at_ref[:, r0:r0 + tile], spec, preferred_element_type=F32)
            y = t if y is None else y + t
        z = conv(o + 1) * (y * inv_l + d_ref[o] * z)
    o_ref[...] = z.astype(o_ref.dtype)


def _hyena_fused(proj, tok0, b, l, col0, w_hy, n_order, a_mat, at_mat, fspec, conv_w, conv_b, hy_d, layer):
    n2 = a_mat.shape[0]
    tn = _pick(w_hy, (512, 256, 128))
    nj = w_hy // tn
    k = conv_w.shape[1]
    assert tok0 % l == 0 and col0 % tn == 0
    col = lambda n: (col0 + n * w_hy) // tn
    x_specs = [pl.BlockSpec((l, tn), lambda j, bi, n=n: (tok0 // l + bi, col(n) + j)) for n in range(n_order + 1)]
    w_specs = [pl.BlockSpec((None, k, tn), lambda j, bi, n=n: (layer, 0, n * nj + j)) for n in range(n_order + 1)]
    b_specs = [pl.BlockSpec((None, 1, tn), lambda j, bi, n=n: (layer, 0, n * nj + j)) for n in range(n_order + 1)]
    return pl.pallas_call(
        functools.partial(_hyena_fused_kernel, n_order=n_order, pad_left=k // 2, tile=_dft_tile(l), inv_l=1.0 / l),
        grid=(nj, b),
        in_specs=[
            pl.BlockSpec((n2, l), lambda j, bi: (0, 0)),
            pl.BlockSpec((l, n2), lambda j, bi: (0, 0)),
            pl.BlockSpec((n_order, n2, tn), lambda j, bi: (0, 0, j)),
            pl.BlockSpec((None, n_order, 1, tn), lambda j, bi: (layer, 0, 0, j)),
            *x_specs, *w_specs, *b_specs,
        ],
        out_specs=pl.BlockSpec((l, tn), lambda j, bi: (bi, j)),
        out_shape=jax.ShapeDtypeStruct((b * l, w_hy), BF16),
        compiler_params=_params(("arbitrary", "arbitrary")),
        name="hyena_fused",
    )(a_mat, at_mat, fspec, hy_d, *([proj] * (n_order + 1)), *([conv_w] * (n_order + 1)),
      *([conv_b] * (n_order + 1)))


HYENA_FUSED_MAX_L = 512


def _hyena(proj, tok0, b, l, col0, w_hy, n_order, a_mat, at_mat, fspec, conv_w, conv_b, hy_d, layer):
    if l <= HYENA_FUSED_MAX_L:
        return _hyena_fused(proj, tok0, b, l, col0, w_hy, n_order, a_mat, at_mat, fspec, conv_w, conv_b, hy_d, layer)
    hyc = _hy_conv(proj, tok0, b, l, col0, (n_order + 1) * w_hy, conv_w, conv_b, layer)
    z, z_col0 = hyc, 0
    for o in range(n_order):
        y = _dft_fwd(a_mat, z, z_col0, fspec, o, b, l, w_hy)
        last = o == n_order - 1
        z = _dft_inv(at_mat, y, z, z_col0, hyc, (o + 1) * w_hy, hy_d, layer, o, b, l, w_hy, BF16 if last else F32)
        z_col0 = 0
    return z


def _merge_kernel(x_ref, mod_ref, g_ref, wg0_ref, wg1_ref, wg2_ref, b0_ref, b1_ref, b2_ref,
                  wb0_ref, wb1_ref, wb2_ref, o_ref, h_scr):
    @pl.when(pl.program_id(1) == 0)
    def _():
        h = _rms_mod(x_ref[...], g_ref[...], mod_ref[1:2, :], mod_ref[0:1, :])
        h_scr[...] = h.astype(BF16)

    h = h_scr[...]
    acc = None
    for wg_ref, b_ref, wb_ref in ((wg0_ref, b0_ref, wb0_ref), (wg1_ref, b1_ref, wb1_ref), (wg2_ref, b2_ref, wb2_ref)):
        gate = jax.nn.sigmoid(jnp.dot(h, wg_ref[...], preferred_element_type=F32))
        t = gate * jnp.dot(b_ref[...], wb_ref[...], preferred_element_type=F32)
        acc = t if acc is None else acc + t
    o_ref[...] = acc.astype(o_ref.dtype)


def _merge(x, mod, g, w_gate, branches, w_br, layer, rpm):
    m, d = x.shape
    wb = w_br.shape[2]
    tm = _row_tile(rpm)
    tn = _pick(d, (256, 128))
    nj = d // tn
    gate_spec = lambda n: pl.BlockSpec((None, d, tn), lambda i, j: (layer, 0, n * nj + j))
    br_spec = pl.BlockSpec((tm, wb), lambda i, j: (i, 0))
    wbr_spec = lambda n: pl.BlockSpec((None, None, wb, tn), lambda i, j: (layer, n, 0, j))
    return pl.pallas_call(
        _merge_kernel,
        grid=(m // tm, nj),
        in_specs=[
            pl.BlockSpec((tm, d), lambda i, j: (i, 0)),
            pl.BlockSpec((None, 6, d), lambda i, j: (_mod_row(i, tm, rpm), 0, 0)),
            pl.BlockSpec((None, 1, d), lambda i, j: (layer, 0, 0)),
            gate_spec(0), gate_spec(1), gate_spec(2),
            br_spec, br_spec, br_spec,
            wbr_spec(0), wbr_spec(1), wbr_spec(2),
        ],
        out_specs=pl.BlockSpec((tm, tn), lambda i, j: (i, j)),
        out_shape=jax.ShapeDtypeStruct((m, d), BF16),
        scratch_shapes=[pltpu.VMEM((tm, d), BF16)],
        compiler_params=_params(("arbitrary", "arbitrary")),
        name="gated_merge",
    )(x, mod, g, w_gate, w_gate, w_gate, *branches, w_br, w_br, w_br)


def _proj_residual_kernel(a_ref, w_ref, x_ref, mod_ref, o_ref, *, gate_row):
    gate = mod_ref[gate_row:gate_row + 1, :]
    o_ref[...] = x_ref[...] + gate * jnp.dot(a_ref[...], w_ref[...], preferred_element_type=F32)


def _proj_residual(a, w, x, mod, gate_row, layer, rpm):
    m, k = a.shape
    d = x.shape[1]
    tm = _row_tile(rpm)
    tn = _pick(d, (512, 256, 128))
    return pl.pallas_call(
        functools.partial(_proj_residual_kernel, gate_row=gate_row),
        grid=(m // tm, d // tn),
        in_specs=[
            pl.BlockSpec((tm, k), lambda i, j: (i, 0)),
            pl.BlockSpec((None, k, tn), lambda i, j: (layer, 0, j)),
            pl.BlockSpec((tm, tn), lambda i, j: (i, j)),
            pl.BlockSpec((None, 6, tn), lambda i, j: (_mod_row(i, tm, rpm), 0, j)),
        ],
        out_specs=pl.BlockSpec((tm, tn), lambda i, j: (i, j)),
        out_shape=jax.ShapeDtypeStruct((m, d), F32),
        compiler_params=_params(("arbitrary", "arbitrary")),
        name="proj_residual",
    )(a, w, x, mod)


def _ffn_up_kernel(x_ref, mod_ref, g_ref, wg_ref, wu_ref, o_ref, h_scr):
    @pl.when(pl.program_id(1) == 0)
    def _():
        h = _rms_mod(x_ref[...], g_ref[...], mod_ref[4:5, :], mod_ref[3:4, :])
        h_scr[...] = h.astype(BF16)

    h = h_scr[...]
    a = jnp.dot(h, wg_ref[...], preferred_element_type=F32)
    u = jnp.dot(h, wu_ref[...], preferred_element_type=F32)
    o_ref[...] = ((a * jax.nn.sigmoid(a)) * u).astype(o_ref.dtype)


def _ffn_up(x, mod, g, w_gate, w_up, layer, rpm):
    m, d = x.shape
    n = w_gate.shape[-1]
    tm = _row_tile(rpm)
    tn = _pick(n, (512, 256, 128))
    w_spec = pl.BlockSpec((None, d, tn), lambda i, j: (layer, 0, j))
    return pl.pallas_call(
        _ffn_up_kernel,
        grid=(m // tm, n // tn),
        in_specs=[
            pl.BlockSpec((tm, d), lambda i, j: (i, 0)),
            pl.BlockSpec((None, 6, d), lambda i, j: (_mod_row(i, tm, rpm), 0, 0)),
            pl.BlockSpec((None, 1, d), lambda i, j: (layer, 0, 0)),
            w_spec, w_spec,
        ],
        out_specs=pl.BlockSpec((tm, tn), lambda i, j: (i, j)),
        out_shape=jax.ShapeDtypeStruct((m, n), BF16),
        scratch_shapes=[pltpu.VMEM((tm, d), BF16)],
        compiler_params=_params(("arbitrary", "arbitrary")),
        name="ffn_up",
    )(x, mod, g, w_gate, w_up)


def _final_norm_kernel(x_ref, g_ref, o_ref):
    x = x_ref[...]
    o_ref[...] = (x * lax.rsqrt(jnp.mean(x * x, axis=-1, keepdims=True) + NORM_EPS)) * g_ref[...]


def _final_norm(x, tok0, n_tok, g):
    d = x.shape[1]
    tm = _pick(math.gcd(tok0, n_tok) if tok0 else n_tok, (512, 256, 128, 64))
    return pl.pallas_call(
        _final_norm_kernel,
        grid=(n_tok // tm,),
        in_specs=[pl.BlockSpec((tm, d), lambda i: (tok0 // tm + i, 0)), pl.BlockSpec((1, d), lambda i: (0, 0))],
        out_specs=pl.BlockSpec((tm, d), lambda i: (i, 0)),
        out_shape=jax.ShapeDtypeStruct((n_tok, d), F32),
        compiler_params=_params(("arbitrary",)),
        name="final_norm",
    )(x, g)


def kernel(x_prompt, x_sample, c, cache_k, cache_v, state_lru, c_ctx, w_mod, b_mod, norm_mix, norm_ffn, w_in, w_gate, att_lambda, att_subln, lru_conv_w, lru_conv_b, lru_wa, lru_ba, lru_wx, lru_bx, lru_lambda, hy_conv_w, hy_conv_b, hy_w1, hy_b1, hy_w2, hy_b2, hy_w3, hy_b3, hy_freq, hy_decay, hy_d, w_br, w_out, w_ff_gate, w_ff_up, w_ff_down, final_norm):
    b_c, l_c, d = x_prompt.shape
    b_s, l_s, _ = x_sample.shape
    depth = w_in.shape[0]
    n_heads, d_v = cache_v.shape[3], cache_v.shape[4]
    w_att = n_heads * d_v
    w_lru = lru_lambda.shape[-1]
    n_order, w_hy = hy_d.shape[1], hy_d.shape[2]
    m_ctx, m_s = b_c * l_c, b_s * l_s
    past = cache_k.shape[2]
    col_xl, col_gl, col_hy = 3 * w_att, 3 * w_att + w_lru, 3 * w_att + 2 * w_lru

    w_in_b, w_gate_b, w_br_b, w_out_b = (w.astype(BF16) for w in (w_in, w_gate, w_br, w_out))
    w_ffg_b, w_ffu_b, w_ffd_b = (w.astype(BF16) for w in (w_ff_gate, w_ff_up, w_ff_down))
    lru_wg = _lru_gate_weights(lru_wa, lru_wx)
    lru_bg = jnp.stack([lru_ba[:, 0], lru_bx[:, 0], lru_ba[:, 1], lru_bx[:, 1]], axis=1)
    norm_mix3, norm_ffn3 = norm_mix.reshape(depth, 1, d), norm_ffn.reshape(depth, 1, d)
    att_subln3 = att_subln.reshape(depth, 1, d_v)
    lru_conv_b3 = lru_conv_b.reshape(depth, 1, w_lru)
    hy_conv_b3 = hy_conv_b.reshape(depth, 1, -1)
    hy_d4 = hy_d.reshape(depth, n_order, 1, w_hy)

    n_rows = 1 + b_s
    assert n_rows <= SUBLANES
    c8 = jnp.zeros((SUBLANES, d), F32).at[0].set(c_ctx).at[1:n_rows].set(c)
    mod_all = _modulation(c8, w_mod, b_mod)[:, :n_rows].reshape(depth, n_rows, 6, d)

    rope = _rope_tables(l_s, d_v // 2)
    dft = {}
    for l in sorted({l_c, l_s}):
        a_mat = jnp.asarray(_dft_matrices(l)).astype(BF16)
        dft[l] = (a_mat, a_mat.T)
    h0_ctx = jnp.zeros((b_c, 2, w_lru), F32)

    def trunk_layer(x, mod, rpm, layer, b, l, rope, cache, h0, want_kv):
        lam_init = 0.8 - 0.6 * math.exp(-0.3 * layer)
        proj, *kv = _norm_proj(x, mod, norm_mix3, w_in_b, layer, rpm, w_att if want_kv else None)
        att = _attention(proj, b, l, n_heads, d_v, lam_init, att_lambda, att_subln3, layer, rope=rope, cache=cache)
        lru, st = _lru(proj, 0, b, l, col_xl, col_gl, w_lru, lru_conv_w, lru_conv_b3, lru_wg, lru_bg, lru_lambda,
                       h0, layer)
        a_mat, at_mat = dft[l]
        filt = _hy_filters(l, hy_w1[layer], hy_b1[layer], hy_freq[layer], hy_w2[layer], hy_b2[layer],
                           hy_w3[layer], hy_b3[layer], hy_decay[layer])
        fspec = _filter_spectrum(a_mat, filt, n_order, w_hy)
        hyo = _hyena(proj, 0, b, l, col_hy, w_hy, n_order, a_mat, at_mat, fspec, hy_conv_w, hy_conv_b3, hy_d4, layer)
        mixed = _merge(x, mod, norm_mix3, w_gate_b, (att, lru, hyo), w_br_b, layer, rpm)
        x = _proj_residual(mixed, w_out_b, x, mod, 2, layer, rpm)
        u = _ffn_up(x, mod, norm_ffn3, w_ffg_b, w_ffu_b, layer, rpm)
        x = _proj_residual(u, w_ffd_b, x, mod, 5, layer, rpm)
        return x, kv, st

    xp = x_prompt.reshape(m_ctx, d)
    ks, vs, ss = [], [], []
    for layer in range(depth):
        xp, (k_l, v_l), s_l = trunk_layer(xp, mod_all[layer, :1], m_ctx, layer, b_c, l_c, None, None, h0_ctx, True)
        ks.append(k_l.reshape(b_c, l_c, n_heads, d_v))
        vs.append(v_l.reshape(b_c, l_c, n_heads, d_v))
        ss.append(s_l)

    xs = x_sample.reshape(m_s, d)
    for layer in range(depth):
        ck = cache_k[:, layer].reshape(b_s, past, w_att)
        cv = cache_v[:, layer].reshape(b_s, past, w_att)
        xs, _, _ = trunk_layer(xs, mod_all[layer, 1:], l_s, layer, b_s, l_s, rope, (ck, cv), state_lru[:, layer], False)

    g_fin = final_norm.reshape(1, d)
    y_prompt = _final_norm(xp, 0, m_ctx, g_fin).reshape(b_c, l_c, d)
    y_sample = _final_norm(xs, 0, m_s, g_fin).reshape(b_s, l_s, d)
    return (y_prompt, y_sample, jnp.stack(ks, axis=1), jnp.stack(vs, axis=1), jnp.stack(ss, axis=1))
```

```python
import functools
import math

import jax
import jax.numpy as jnp
import numpy as np
from jax import lax
from jax.experimental import pallas as pl
from jax.experimental.pallas import tpu as pltpu

F32 = jnp.float32
BF16 = jnp.bfloat16

GRID_W = 64
ROPE_THETA = 10000.0
NORM_EPS = 1e-6
SUBLN_EPS = 1e-5
LRU_C = 8.0
HY_BANDS = 8
LANES = 128
SUBLANES = 8
BF16_ROWS = 16
VMEM_LIMIT_BYTES = 56 * 1024 * 1024


def _pick(n, prefs):
    for p in prefs:
        if n % p == 0:
            return p
    raise ValueError(f"no tile in {prefs} divides {n}")


def _params(sem):
    return pltpu.CompilerParams(dimension_semantics=sem, vmem_limit_bytes=VMEM_LIMIT_BYTES)


def _row_tile(rpm):
    return _pick(rpm, (1024, 512, 256, 128))


def _mod_row(i, tm, rpm):
    return i // (rpm // tm)


def _rms_mod(x, g, sc, sh):
    y = x * lax.rsqrt(jnp.mean(x * x, axis=-1, keepdims=True) + NORM_EPS)
    return (y * g) * (1.0 + sc) + sh


def _mod_kernel(c_ref, w_ref, b_ref, o_ref):
    c = c_ref[...]
    s = (c * jax.nn.sigmoid(c)).astype(BF16)
    o_ref[...] = jnp.dot(s, w_ref[...].astype(BF16), preferred_element_type=F32) + b_ref[...]


def _modulation(c8, w_mod, b_mod):
    depth, d, n = w_mod.shape
    tn = _pick(n, (1024, 512, 256, 128))
    return pl.pallas_call(
        _mod_kernel,
        grid=(depth, n // tn),
        in_specs=[
            pl.BlockSpec((SUBLANES, d), lambda l, j: (0, 0)),
            pl.BlockSpec((None, d, tn), lambda l, j: (l, 0, j)),
            pl.BlockSpec((None, 1, tn), lambda l, j: (l, 0, j)),
        ],
        out_specs=pl.BlockSpec((None, SUBLANES, tn), lambda l, j: (l, 0, j)),
        out_shape=jax.ShapeDtypeStruct((depth, SUBLANES, n), F32),
        compiler_params=_params(("arbitrary", "arbitrary")),
        name="modulation",
    )(c8, w_mod, b_mod.reshape(depth, 1, n))


PROLOGUE_ROWS = 128


def _norm_tile(x_ref, mod_ref, g_ref, h_ref, sc_row, sh_row):
    tm = x_ref.shape[0]
    ch = math.gcd(tm, PROLOGUE_ROWS)

    def body(c, carry):
        r0 = pl.multiple_of(c * ch, ch)
        h = _rms_mod(x_ref[pl.ds(r0, ch), :], g_ref[...], mod_ref[sc_row:sc_row + 1, :], mod_ref[sh_row:sh_row + 1, :])
        h_ref[pl.ds(r0, ch), :] = h.astype(h_ref.dtype)
        return carry

    lax.fori_loop(0, tm // ch, body, 0)


def _norm_proj_kernel(x_ref, mod_ref, g_ref, w_ref, o_ref, h_ref, *kv_refs, kv_tiles):
    j = pl.program_id(1)

    @pl.when(j == 0)
    def _():
        _norm_tile(x_ref, mod_ref, g_ref, h_ref, 1, 0)

    y = jnp.dot(h_ref[...], w_ref[...], preferred_element_type=F32)
    o_ref[...] = y
    if kv_tiles:
        k_ref, v_ref = kv_refs

        @pl.when((j >= kv_tiles) & (j < 2 * kv_tiles))
        def _():
            k_ref[...] = y

        @pl.when((j >= 2 * kv_tiles) & (j < 3 * kv_tiles))
        def _():
            v_ref[...] = y


def _norm_proj(x, mod, g, w, layer, rpm, w_att=None):
    m, d = x.shape
    n = w.shape[-1]
    tm = _row_tile(rpm)
    tn = _pick(math.gcd(n, w_att or n), (512, 256, 128) if w_att else (1024, 512, 256, 128))
    out_specs = [pl.BlockSpec((tm, tn), lambda i, j: (i, j)), pl.BlockSpec((tm, d), lambda i, j: (i, 0))]
    out_shape = [jax.ShapeDtypeStruct((m, n), F32), jax.ShapeDtypeStruct((m, d), BF16)]
    kv_tiles = 0
    if w_att is not None:
        kv_tiles = w_att // tn
        out_specs += [pl.BlockSpec((tm, tn), lambda i, j, t=t: (i, jnp.clip(j - t * kv_tiles, 0, kv_tiles - 1)))
                      for t in (1, 2)]
        out_shape += [jax.ShapeDtypeStruct((m, w_att), F32)] * 2
    return pl.pallas_call(
        functools.partial(_norm_proj_kernel, kv_tiles=kv_tiles),
        grid=(m // tm, n // tn),
        in_specs=[
            pl.BlockSpec((tm, d), lambda i, j: (i, 0)),
            pl.BlockSpec((None, 6, d), lambda i, j: (_mod_row(i, tm, rpm), 0, 0)),
            pl.BlockSpec((None, 1, d), lambda i, j: (layer, 0, 0)),
            pl.BlockSpec((None, d, tn), lambda i, j: (layer, 0, j)),
        ],
        out_specs=out_specs,
        out_shape=out_shape,
        compiler_params=_params(("arbitrary", "arbitrary")),
        name="norm_proj",
    )(x, mod, g, w)


def _rope(x, cos, sin_signed):
    n = x.shape[-1]
    half = 16
    lane = lax.broadcasted_iota(jnp.int32, x.shape, 1)
    swapped = jnp.where((lane % (2 * half)) < half, pltpu.roll(x, n - half, 1), pltpu.roll(x, half, 1))
    return x * cos + swapped * sin_signed


def _attn_kernel(*refs, lam_init, d_qk, n_hb, use_rope, use_cache):
    it = iter(refs)
    lam_ref, sub_ref, q_ref, k_ref, v_ref = next(it), next(it), next(it), next(it), next(it)
    if use_rope:
        cq_ref, sq_ref, ck_ref, sk_ref = next(it), next(it), next(it), next(it)
    if use_cache:
        pk_ref, pv_ref = next(it), next(it)
    o_ref = next(it)
    k_scr = next(it)
    d_v = 2 * d_qk
    heads = [slice(hh * d_v, (hh + 1) * d_v) for hh in range(n_hb)]

    @pl.when(pl.program_id(2) == 0)
    def _():
        for sl in heads:
            k = k_ref[:, sl]
            if use_rope:
                k = _rope(k, ck_ref[...], sk_ref[...])
            k_scr[:, sl] = k.astype(BF16)

    a = lam_ref[...]
    lam = (jnp.exp(jnp.sum(a[0:1] * a[1:2], axis=-1, keepdims=True))
           - jnp.exp(jnp.sum(a[2:3] * a[3:4], axis=-1, keepdims=True)) + lam_init)
    scale = d_qk ** -0.5
    nt = (((1,), (1,)), ((), ()))

    for sl in heads:
        q = q_ref[:, sl]
        if use_rope:
            q = _rope(q, cq_ref[...], sq_ref[...])
        q = q * scale
        lane = lax.broadcasted_iota(jnp.int32, q.shape, 1)
        q1 = jnp.where(lane < d_qk, q, 0.0).astype(BF16)
        q2 = jnp.where(lane >= d_qk, q, 0.0).astype(BF16)
        keys = [k_scr[:, sl]]
        vals = [v_ref[:, sl].astype(BF16)]
        if use_cache:
            keys.append(pk_ref[:, sl].astype(BF16))
            vals.append(pv_ref[:, sl].astype(BF16))

        def softmax_v(qh):
            s = [lax.dot_general(qh, kk, nt, preferred_element_type=F32) for kk in keys]
            mx = functools.reduce(jnp.maximum, [jnp.max(x, axis=-1, keepdims=True) for x in s])
            p = [jnp.exp(x - mx) for x in s]
            den = functools.reduce(jnp.add, [jnp.sum(x, axis=-1, keepdims=True) for x in p])
            pv = functools.reduce(jnp.add, [jnp.dot(x.astype(BF16), vv, preferred_element_type=F32)
                                            for x, vv in zip(p, vals)])
            return pv, 1.0 / den

        o1, r1 = softmax_v(q1)
        o2, r2 = softmax_v(q2)
        o = o1 * r1 - o2 * (lam * r2)
        y = o * lax.rsqrt(jnp.mean(o * o, axis=-1, keepdims=True) + SUBLN_EPS)
        o_ref[:, sl] = ((y * sub_ref[...]) * (1.0 - lam_init)).astype(o_ref.dtype)


def _attention(proj, b, l, h, d_v, lam_init, att_lambda, att_subln, layer, rope=None, cache=None):
    assert d_v == LANES
    tq = _pick(l, (256, 128, 64))
    nq = l // tq
    n_hb = max(n for n in range(1, h + 1) if h % n == 0 and n * l <= max(l, 2048))
    wb = n_hb * d_v
    hg = h // n_hb
    in_specs = [
        pl.BlockSpec((None, 4, att_lambda.shape[-1]), lambda bi, hi, qi: (layer, 0, 0)),
        pl.BlockSpec((None, 1, d_v), lambda bi, hi, qi: (layer, 0, 0)),
        pl.BlockSpec((tq, wb), lambda bi, hi, qi: (bi * nq + qi, hi)),
        pl.BlockSpec((l, wb), lambda bi, hi, qi: (bi, hg + hi)),
        pl.BlockSpec((l, wb), lambda bi, hi, qi: (bi, 2 * hg + hi)),
    ]
    args = [att_lambda, att_subln, proj, proj, proj]
    if rope is not None:
        cos, sin = rope
        in_specs += [
            pl.BlockSpec((tq, d_v), lambda bi, hi, qi: (qi, 0)),
            pl.BlockSpec((tq, d_v), lambda bi, hi, qi: (qi, 0)),
            pl.BlockSpec((l, d_v), lambda bi, hi, qi: (0, 0)),
            pl.BlockSpec((l, d_v), lambda bi, hi, qi: (0, 0)),
        ]
        args += [cos, sin, cos, sin]
    if cache is not None:
        ck, cv = cache
        past = ck.shape[1]
        in_specs += [
            pl.BlockSpec((None, past, wb), lambda bi, hi, qi: (bi, 0, hi)),
            pl.BlockSpec((None, past, wb), lambda bi, hi, qi: (bi, 0, hi)),
        ]
        args += [ck, cv]
    return pl.pallas_call(
        functools.partial(_attn_kernel, lam_init=lam_init, d_qk=d_v // 2, n_hb=n_hb,
                          use_rope=rope is not None, use_cache=cache is not None),
        grid=(b, hg, nq),
        in_specs=in_specs,
        out_specs=pl.BlockSpec((tq, wb), lambda bi, hi, qi: (bi * nq + qi, hi)),
        out_shape=jax.ShapeDtypeStruct((b * l, h * d_v), BF16),
        scratch_shapes=[pltpu.VMEM((l, wb), BF16)],
        compiler_params=_params(("arbitrary", "arbitrary", "arbitrary")),
        name="diff_attention",
    )(*args)


def _rope_tables(l, d_qk):
    n_freq = d_qk // 4
    lane = jnp.arange(2 * d_qk)
    axis = (lane % d_qk) // (2 * n_freq)
    freq = lane % n_freq
    first = (lane % (2 * n_freq)) < n_freq
    inv_freq = ROPE_THETA ** (-freq.astype(F32) / n_freq)
    t = jnp.arange(l)
    pos = jnp.where(axis[None, :] == 0, (t // GRID_W)[:, None], (t % GRID_W)[:, None]).astype(F32)
    ang = pos * inv_freq[None, :]
    return jnp.cos(ang), jnp.where(first[None, :], -jnp.sin(ang), jnp.sin(ang))


def _shift_rows(x, d):
    n = x.shape[0]
    if d == 0:
        return x
    row = lax.broadcasted_iota(jnp.int32, x.shape, 0)
    rolled = pltpu.roll(x, (-d) % n, 0)
    ok = (row + d >= 0) & (row + d < n)
    return jnp.where(ok, rolled, 0.0)


def _dwconv(x, w, bias, pad_left):
    y = bias + w[0:1] * _shift_rows(x, -pad_left)
    for k in range(1, w.shape[0]):
        y = y + w[k:k + 1] * _shift_rows(x, k - pad_left)
    return y


def _sigmoid(x):
    return 0.5 * jnp.tanh(0.5 * x) + 0.5


SCAN_UNROLL = 8


def _lru_kernel(xl_ref, gl_ref, cw_ref, cb_ref, wg_ref, bg_ref, lam_ref, h0_ref, o_ref, st_ref,
                a_scr, b_scr, hf_scr, *, pad_left):
    l, cb = xl_ref.shape
    n_grp = cb // LANES
    xc = _dwconv(xl_ref[...], cw_ref[...], cb_ref[...], pad_left)
    lam = lam_ref[...]
    neg = -lam
    softplus = jnp.maximum(neg, 0.0) + jnp.log1p(jnp.exp(-jnp.abs(neg)))
    row8 = lax.broadcasted_iota(jnp.int32, (SUBLANES, cb), 0)
    n_chunks = l // SUBLANES

    def fill(direction):
        for g in range(n_grp):
            sl = slice(g * LANES, (g + 1) * LANES)
            xg = xc[:, sl]
            y = jnp.dot(xg.astype(BF16), wg_ref[g, :, 2 * direction * LANES:(2 * direction + 2) * LANES],
                        preferred_element_type=F32)
            r = _sigmoid(y[:, :LANES] + bg_ref[2 * direction:2 * direction + 1, sl])
            i = _sigmoid(y[:, LANES:] + bg_ref[2 * direction + 1:2 * direction + 2, sl])
            log_a = (-LRU_C * r) * softplus[direction:direction + 1, sl]
            th = jnp.tanh(log_a)
            one_minus_a2 = (-2.0 * th) / (1.0 - th)
            a_scr[:, sl] = jnp.exp(log_a)
            b_scr[:, sl] = jnp.sqrt(one_minus_a2) * (i * xg)

    fill(0)

    def fwd(c, h):
        r0 = pl.multiple_of(c * SUBLANES, SUBLANES)
        a = a_scr[pl.ds(r0, SUBLANES), :]
        b = b_scr[pl.ds(r0, SUBLANES), :]
        for s in (1, 2, 4):
            keep = row8 >= s
            b = jnp.where(keep, a * pltpu.roll(b, s, 0) + b, b)
            a = jnp.where(keep, a * pltpu.roll(a, s, 0), a)
        h8 = a * h + b
        hf_scr[pl.ds(r0, SUBLANES), :] = h8
        return h8[SUBLANES - 1:SUBLANES, :]

    s_f = lax.fori_loop(0, n_chunks, fwd, h0_ref[0:1, :], unroll=SCAN_UNROLL)

    fill(1)

    def bwd(c, h):
        r0 = pl.multiple_of((n_chunks - 1 - c) * SUBLANES, SUBLANES)
        a = a_scr[pl.ds(r0, SUBLANES), :]
        b = b_scr[pl.ds(r0, SUBLANES), :]
        for s in (1, 2, 4):
            keep = row8 < SUBLANES - s
            b = jnp.where(keep, a * pltpu.roll(b, SUBLANES - s, 0) + b, b)
            a = jnp.where(keep, a * pltpu.roll(a, SUBLANES - s, 0), a)
        h8 = a * h + b
        b_scr[pl.ds(r0, SUBLANES), :] = h8
        return h8[0:1, :]

    s_b = lax.fori_loop(0, n_chunks, bwd, h0_ref[1:2, :], unroll=SCAN_UNROLL)

    st_ref[0:1, :] = s_f
    st_ref[1:2, :] = s_b
    o_ref[...] = ((hf_scr[...] + b_scr[...]) * jax.nn.gelu(gl_ref[...])).astype(o_ref.dtype)


def _lru(proj, tok0, b, l, col_x, col_g, w_lru, conv_w, conv_b, wg, bg, lam, h0, layer):
    cb = _pick(w_lru, tuple(c for c in (1024, 512, 256, 128) if c == 128 or l * c <= 512 * 1024))
    assert tok0 % l == 0 and col_x % cb == 0 and col_g % cb == 0
    k = conv_w.shape[1]
    n_grp = cb // LANES
    out, st = pl.pallas_call(
        functools.partial(_lru_kernel, pad_left=k // 2),
        grid=(b, w_lru // cb),
        in_specs=[
            pl.BlockSpec((l, cb), lambda bi, j: (tok0 // l + bi, col_x // cb + j)),
            pl.BlockSpec((l, cb), lambda bi, j: (tok0 // l + bi, col_g // cb + j)),
            pl.BlockSpec((None, k, cb), lambda bi, j: (layer, 0, j)),
            pl.BlockSpec((None, 1, cb), lambda bi, j: (layer, 0, j)),
            pl.BlockSpec((None, n_grp, LANES, 4 * LANES), lambda bi, j: (layer, j, 0, 0)),
            pl.BlockSpec((None, 4, cb), lambda bi, j: (layer, 0, j)),
            pl.BlockSpec((None, 2, cb), lambda bi, j: (layer, 0, j)),
            pl.BlockSpec((None, 2, cb), lambda bi, j: (bi, 0, j)),
        ],
        out_specs=[
            pl.BlockSpec((l, cb), lambda bi, j: (bi, j)),
            pl.BlockSpec((None, 2, cb), lambda bi, j: (bi, 0, j)),
        ],
        out_shape=[jax.ShapeDtypeStruct((b * l, w_lru), BF16), jax.ShapeDtypeStruct((b, 2, w_lru), F32)],
        scratch_shapes=[pltpu.VMEM((l, cb), F32)] * 3,
        compiler_params=_params(("arbitrary", "arbitrary")),
        name="rglru",
    )(proj, proj, conv_w, conv_b, wg, bg, lam, h0)
    return out, st


def _lru_gate_weights(wa, wx):
    depth, _, nb, bs, _ = wa.shape
    per = LANES // bs
    eye = jnp.eye(per, dtype=wa.dtype)

    def bd(w):
        w = w.reshape(depth, nb // per, per, bs, bs)
        return jnp.einsum("lgpde,pq->lgpdqe", w, eye).reshape(depth, nb // per, LANES, LANES)

    return jnp.concatenate([bd(wa[:, 0]), bd(wx[:, 0]), bd(wa[:, 1]), bd(wx[:, 1])], axis=-1).astype(BF16)


def _hy_conv_kernel(x_ref, w_ref, b_ref, o_ref, *, pad_left):
    o_ref[...] = _dwconv(x_ref[...], w_ref[...], b_ref[...], pad_left)


def _hy_conv(proj, tok0, b, l, col0, width, conv_w, conv_b, layer):
    cb = _pick(width, (512, 256, 128))
    assert tok0 % l == 0 and col0 % cb == 0
    k = conv_w.shape[1]
    return pl.pallas_call(
        functools.partial(_hy_conv_kernel, pad_left=k // 2),
        grid=(b, width // cb),
        in_specs=[
            pl.BlockSpec((l, cb), lambda bi, j: (tok0 // l + bi, col0 // cb + j)),
            pl.BlockSpec((None, k, cb), lambda bi, j: (layer, 0, j)),
            pl.BlockSpec((None, 1, cb), lambda bi, j: (layer, 0, j)),
        ],
        out_specs=pl.BlockSpec((l, cb), lambda bi, j: (bi, j)),
        out_shape=jax.ShapeDtypeStruct((b * l, width), F32),
        compiler_params=_params(("arbitrary", "arbitrary")),
        name="hyena_dwconv",
    )(proj, conv_w, conv_b)


def _hy_filter_kernel(z_ref, w1_ref, b1_ref, fr_ref, w2_ref, b2_ref, w3_ref, b3_ref, dec_ref, o_ref, h_scr):
    hi = lax.Precision.HIGHEST

    @pl.when(pl.program_id(0) == 0)
    def _():
        fr = fr_ref[...]
        h = jnp.sin(fr * (jnp.dot(z_ref[...], w1_ref[...], precision=hi, preferred_element_type=F32) + b1_ref[...]))
        h_scr[...] = jnp.sin(fr * (jnp.dot(h, w2_ref[...], precision=hi, preferred_element_type=F32) + b2_ref[...]))

    filt = jnp.dot(h_scr[...], w3_ref[...], precision=hi, preferred_element_type=F32) + b3_ref[...]
    o_ref[...] = filt * jnp.exp(-z_ref[:, 0:1] * jnp.abs(dec_ref[...]))


def _pad2(x, rows, cols):
    return jnp.pad(x, ((0, rows - x.shape[0]), (0, cols - x.shape[1])))


def _hy_filters(l, w1, b1, freq, w2, b2, w3, b3, decay):
    n = w3.shape[-1]
    t = jnp.linspace(0.0, 1.0, l, dtype=F32)[:, None]
    w = 2.0 * math.pi * jnp.arange(l, dtype=F32)[:, None] / l
    f = jnp.linspace(1e-4, HY_BANDS - 1, HY_BANDS, dtype=F32)[None]
    z = _pad2(jnp.concatenate([t, jnp.cos(f * w), -jnp.sin(f * w)], axis=-1), l, LANES)
    tn = _pick(n, (512, 256, 128))
    row = lambda v: _pad2(v.reshape(1, -1), 1, LANES)
    full = lambda shape: pl.BlockSpec(shape, lambda j: (0, 0))
    return pl.pallas_call(
        _hy_filter_kernel,
        grid=(n // tn,),
        in_specs=[full((l, LANES)), full((LANES, LANES)), full((1, LANES)), full((1, LANES)),
                  full((LANES, LANES)), full((1, LANES)),
                  pl.BlockSpec((LANES, tn), lambda j: (0, j)),
                  pl.BlockSpec((1, tn), lambda j: (0, j)),
                  pl.BlockSpec((1, tn), lambda j: (0, j))],
        out_specs=pl.BlockSpec((l, tn), lambda j: (0, j)),
        out_shape=jax.ShapeDtypeStruct((l, n), F32),
        scratch_shapes=[pltpu.VMEM((l, LANES), F32)],
        compiler_params=_params(("arbitrary",)),
        name="hyena_filters",
    )(z, _pad2(w1, LANES, LANES), row(b1), row(freq), _pad2(w2, LANES, LANES), row(b2),
      _pad2(w3, LANES, n), b3.reshape(1, n), decay.reshape(1, n))


def _dft_tile(l):
    return _pick(2 * l, (512, 256, 128))


@functools.lru_cache(maxsize=None)
def _dft_matrices(l):
    tile = _dft_tile(l)
    half = tile // 2
    r = np.arange(2 * l)
    k = (r // tile) * half + (r % half)
    is_im = (r % tile) >= half
    t = np.arange(l)
    ph = ((2 * k + 1)[:, None] * t[None, :]) % (4 * l)
    ang = ph.astype(np.float64) * (math.pi / (2 * l))
    return np.where(is_im[:, None], -np.sin(ang), np.cos(ang)).astype(np.float32)


def _cmul_store(o_ref, s, f_re, f_im):
    half = s.shape[0] // 2
    s_re, s_im = s[:half], s[half:]
    o_ref[:half, :] = (s_re * f_re - s_im * f_im).astype(o_ref.dtype)
    o_ref[half:, :] = (s_re * f_im + s_im * f_re).astype(o_ref.dtype)


def _filter_spec_kernel(a_ref, hf_ref, hb_ref, o_ref, hf_scr, hb_scr):
    @pl.when(pl.program_id(2) == 0)
    def _():
        hf_scr[...] = hf_ref[...].astype(BF16)
        hb = hb_ref[...]
        row = lax.broadcasted_iota(jnp.int32, hb.shape, 0)
        hb_scr[...] = jnp.where(row == 0, 0.0, hb).astype(BF16)

    a = a_ref[...]
    sf = jnp.dot(a, hf_scr[...], preferred_element_type=F32)
    sb = jnp.dot(a, hb_scr[...], preferred_element_type=F32)
    half = a.shape[0] // 2
    o_ref[:half, :] = sf[:half] + sb[:half]
    o_ref[half:, :] = sf[half:] - sb[half:]


def _filter_spectrum(a_mat, filt, n_order, w_hy):
    n2, l = a_mat.shape
    tm = _dft_tile(l)
    tn = _pick(w_hy, (512, 256, 128))
    nj = w_hy // tn
    return pl.pallas_call(
        _filter_spec_kernel,
        grid=(n_order, nj, n2 // tm),
        in_specs=[
            pl.BlockSpec((tm, l), lambda o, j, i: (i, 0)),
            pl.BlockSpec((l, tn), lambda o, j, i: (0, (2 * o) * nj + j)),
            pl.BlockSpec((l, tn), lambda o, j, i: (0, (2 * o + 1) * nj + j)),
        ],
        out_specs=pl.BlockSpec((None, tm, tn), lambda o, j, i: (o, i, j)),
        out_shape=jax.ShapeDtypeStruct((n_order, n2, w_hy), F32),
        scratch_shapes=[pltpu.VMEM((l, tn), BF16)] * 2,
        compiler_params=_params(("arbitrary", "arbitrary", "arbitrary")),
        name="hyena_filter_spectrum",
    )(a_mat, filt, filt)


def _dft_fwd_kernel(a_ref, z_ref, f_ref, o_ref, z_scr):
    @pl.when(pl.program_id(2) == 0)
    def _():
        z_scr[...] = z_ref[...].astype(BF16)

    s = jnp.dot(a_ref[...], z_scr[...], preferred_element_type=F32)
    half = s.shape[0] // 2
    _cmul_store(o_ref, s, f_ref[:half, :], f_ref[half:, :])


def _dft_fwd(a_mat, z, z_col0, fspec, order, b, l, w_hy):
    n2 = a_mat.shape[0]
    tm = _dft_tile(l)
    tn = _pick(w_hy, (1024, 512, 256, 128))
    assert z_col0 % tn == 0
    return pl.pallas_call(
        _dft_fwd_kernel,
        grid=(b, w_hy // tn, n2 // tm),
        in_specs=[
            pl.BlockSpec((tm, l), lambda bi, j, i: (i, 0)),
            pl.BlockSpec((l, tn), lambda bi, j, i: (bi, z_col0 // tn + j)),
            pl.BlockSpec((None, tm, tn), lambda bi, j, i: (order, i, j)),
        ],
        out_specs=pl.BlockSpec((None, tm, tn), lambda bi, j, i: (bi, i, j)),
        out_shape=jax.ShapeDtypeStruct((b, n2, w_hy), BF16),
        scratch_shapes=[pltpu.VMEM((l, tn), BF16)],
        compiler_params=_params(("arbitrary", "arbitrary", "arbitrary")),
        name="hyena_dft_fwd",
    )(a_mat, z, fspec)


def _dft_inv_kernel(at_ref, y_ref, z_ref, g_ref, d_ref, o_ref, *, inv_l):
    y = jnp.dot(at_ref[...], y_ref[...], preferred_element_type=F32) * inv_l
    z = z_ref[...]
    o_ref[...] = (g_ref[...] * (y + d_ref[...] * z)).astype(o_ref.dtype)


def _dft_inv(at_mat, y, z, z_col0, gate, gate_col0, d, layer, order, b, l, w_hy, out_dtype):
    n2 = at_mat.shape[1]
    tm = _pick(l, (512, 256, 128, 64))
    tn = _pick(w_hy, (1024, 512, 256, 128))
    nt = l // tm
    assert z_col0 % tn == 0 and gate_col0 % tn == 0
    return pl.pallas_call(
        functools.partial(_dft_inv_kernel, inv_l=1.0 / l),
        grid=(b, w_hy // tn, nt),
        in_specs=[
            pl.BlockSpec((tm, n2), lambda bi, j, i: (i, 0)),
            pl.BlockSpec((None, n2, tn), lambda bi, j, i: (bi, 0, j)),
            pl.BlockSpec((tm, tn), lambda bi, j, i: (bi * nt + i, z_col0 // tn + j)),
            pl.BlockSpec((tm, tn), lambda bi, j, i: (bi * nt + i, gate_col0 // tn + j)),
            pl.BlockSpec((None, None, 1, tn), lambda bi, j, i: (layer, order, 0, j)),
        ],
        out_specs=pl.BlockSpec((tm, tn), lambda bi, j, i: (bi * nt + i, j)),
        out_shape=jax.ShapeDtypeStruct((b * l, w_hy), out_dtype),
        compiler_params=_params(("arbitrary", "arbitrary", "arbitrary")),
        name="hyena_dft_inv",
    )(at_mat, y, z, gate, d)


def _hyena_fused_kernel(*refs, n_order, pad_left, tile, inv_l):
    it = iter(refs)
    a_ref, at_ref, f_ref, d_ref = next(it), next(it), next(it), next(it)
    x_refs = [next(it) for _ in range(n_order + 1)]
    w_refs = [next(it) for _ in range(n_order + 1)]
    b_refs = [next(it) for _ in range(n_order + 1)]
    o_ref = next(it)
    half = tile // 2
    n_tiles = a_ref.shape[0] // tile
    conv = lambda n: _dwconv(x_refs[n][...], w_refs[n][...], b_refs[n][...], pad_left)
    z = conv(0)
    for o in range(n_order):
        zb = z.astype(BF16)
        y = None
        for ti in range(n_tiles):
            r0 = ti * tile
            s = jnp.dot(a_ref[r0:r0 + tile, :], zb, preferred_element_type=F32)
            s_re, s_im = s[:half], s[half:]
            f_re, f_im = f_ref[o, r0:r0 + half, :], f_ref[o, r0 + half:r0 + tile, :]
            spec = jnp.concatenate([s_re * f_re - s_im * f_im, s_re * f_im + s_im * f_re], axis=0).astype(BF16)
            t = jnp.dot(at_ref[:, r0:r0 + tile], spec, preferred_element_type=F32)
            y = t if y is None else y + t
        z = conv(o + 1) * (y * inv_l + d_ref[o] * z)
    o_ref[...] = z.astype(o_ref.dtype)


def _hyena_fused(proj, tok0, b, l, col0, w_hy, n_order, a_mat, at_mat, fspec, conv_w, conv_b, hy_d, layer):
    n2 = a_mat.shape[0]
    tn = _pick(w_hy, (1024, 512, 256, 128))
    nj = w_hy // tn
    k = conv_w.shape[1]
    assert tok0 % l == 0 and col0 % tn == 0
    col = lambda n: (col0 + n * w_hy) // tn
    x_specs = [pl.BlockSpec((l, tn), lambda j, bi, n=n: (tok0 // l + bi, col(n) + j)) for n in range(n_order + 1)]
    w_specs = [pl.BlockSpec((None, k, tn), lambda j, bi, n=n: (layer, 0, n * nj + j)) for n in range(n_order + 1)]
    b_specs = [pl.BlockSpec((None, 1, tn), lambda j, bi, n=n: (layer, 0, n * nj + j)) for n in range(n_order + 1)]
    return pl.pallas_call(
        functools.partial(_hyena_fused_kernel, n_order=n_order, pad_left=k // 2, tile=_dft_tile(l), inv_l=1.0 / l),
        grid=(nj, b),
        in_specs=[
            pl.BlockSpec((n2, l), lambda j, bi: (0, 0)),
            pl.BlockSpec((l, n2), lambda j, bi: (0, 0)),
            pl.BlockSpec((n_order, n2, tn), lambda j, bi: (0, 0, j)),
            pl.BlockSpec((None, n_order, 1, tn), lambda j, bi: (layer, 0, 0, j)),
            *x_specs, *w_specs, *b_specs,
        ],
        out_specs=pl.BlockSpec((l, tn), lambda j, bi: (bi, j)),
        out_shape=jax.ShapeDtypeStruct((b * l, w_hy), BF16),
        compiler_params=_params(("arbitrary", "arbitrary")),
        name="hyena_fused",
    )(a_mat, at_mat, fspec, hy_d, *([proj] * (n_order + 1)), *([conv_w] * (n_order + 1)),
      *([conv_b] * (n_order + 1)))


HYENA_FUSED_MAX_L = 512


def _hyena(proj, tok0, b, l, col0, w_hy, n_order, a_mat, at_mat, fspec, conv_w, conv_b, hy_d, layer):
    if l <= HYENA_FUSED_MAX_L:
        return _hyena_fused(proj, tok0, b, l, col0, w_hy, n_order, a_mat, at_mat, fspec, conv_w, conv_b, hy_d, layer)
    hyc = _hy_conv(proj, tok0, b, l, col0, (n_order + 1) * w_hy, conv_w, conv_b, layer)
    z, z_col0 = hyc, 0
    for o in range(n_order):
        y = _dft_fwd(a_mat, z, z_col0, fspec, o, b, l, w_hy)
        last = o == n_order - 1
        z = _dft_inv(at_mat, y, z, z_col0, hyc, (o + 1) * w_hy, hy_d, layer, o, b, l, w_hy, BF16 if last else F32)
        z_col0 = 0
    return z


def _merge_kernel(h_ref, wg0_ref, wg1_ref, wg2_ref, b0_ref, b1_ref, b2_ref, wb0_ref, wb1_ref, wb2_ref, o_ref):
    h = h_ref[...]
    acc = None
    for wg_ref, b_ref, wb_ref in ((wg0_ref, b0_ref, wb0_ref), (wg1_ref, b1_ref, wb1_ref), (wg2_ref, b2_ref, wb2_ref)):
        gate = jax.nn.sigmoid(jnp.dot(h, wg_ref[...], preferred_element_type=F32))
        t = gate * jnp.dot(b_ref[...], wb_ref[...], preferred_element_type=F32)
        acc = t if acc is None else acc + t
    o_ref[...] = acc.astype(o_ref.dtype)


def _merge(h, w_gate, branches, w_br, layer):
    m, d = h.shape
    wb = w_br.shape[2]
    tm = _pick(m, (1024, 512, 256, 128))
    tn = _pick(d, (512, 256, 128))
    nj = d // tn
    gate_spec = lambda n: pl.BlockSpec((None, d, tn), lambda i, j: (layer, 0, n * nj + j))
    br_spec = pl.BlockSpec((tm, wb), lambda i, j: (i, 0))
    wbr_spec = lambda n: pl.BlockSpec((None, None, wb, tn), lambda i, j: (layer, n, 0, j))
    return pl.pallas_call(
        _merge_kernel,
        grid=(m // tm, nj),
        in_specs=[
            pl.BlockSpec((tm, d), lambda i, j: (i, 0)),
            gate_spec(0), gate_spec(1), gate_spec(2),
            br_spec, br_spec, br_spec,
            wbr_spec(0), wbr_spec(1), wbr_spec(2),
        ],
        out_specs=pl.BlockSpec((tm, tn), lambda i, j: (i, j)),
        out_shape=jax.ShapeDtypeStruct((m, d), BF16),
        compiler_params=_params(("arbitrary", "arbitrary")),
        name="gated_merge",
    )(h, w_gate, w_gate, w_gate, *branches, w_br, w_br, w_br)


def _proj_residual_kernel(a_ref, w_ref, x_ref, mod_ref, o_ref, *, gate_row):
    gate = mod_ref[gate_row:gate_row + 1, :]
    o_ref[...] = x_ref[...] + gate * jnp.dot(a_ref[...], w_ref[...], preferred_element_type=F32)


def _proj_residual(a, w, x, mod, gate_row, layer, rpm):
    m, k = a.shape
    d = x.shape[1]
    tm = _row_tile(rpm)
    tn = _pick(d, (512, 256, 128))
    return pl.pallas_call(
        functools.partial(_proj_residual_kernel, gate_row=gate_row),
        grid=(m // tm, d // tn),
        in_specs=[
            pl.BlockSpec((tm, k), lambda i, j: (i, 0)),
            pl.BlockSpec((None, k, tn), lambda i, j: (layer, 0, j)),
            pl.BlockSpec((tm, tn), lambda i, j: (i, j)),
            pl.BlockSpec((None, 6, tn), lambda i, j: (_mod_row(i, tm, rpm), 0, j)),
        ],
        out_specs=pl.BlockSpec((tm, tn), lambda i, j: (i, j)),
        out_shape=jax.ShapeDtypeStruct((m, d), F32),
        compiler_params=_params(("arbitrary", "arbitrary")),
        name="proj_residual",
    )(a, w, x, mod)


def _ffn_up_kernel(x_ref, mod_ref, g_ref, wg_ref, wu_ref, o_ref, h_scr):
    @pl.when(pl.program_id(1) == 0)
    def _():
        _norm_tile(x_ref, mod_ref, g_ref, h_scr, 4, 3)

    h = h_scr[...]
    a = jnp.dot(h, wg_ref[...], preferred_element_type=F32)
    u = jnp.dot(h, wu_ref[...], preferred_element_type=F32)
    o_ref[...] = ((a * jax.nn.sigmoid(a)) * u).astype(o_ref.dtype)


def _ffn_up(x, mod, g, w_gate, w_up, layer, rpm):
    m, d = x.shape
    n = w_gate.shape[-1]
    tm = _row_tile(rpm)
    tn = _pick(n, (512, 256, 128))
    w_spec = pl.BlockSpec((None, d, tn), lambda i, j: (layer, 0, j))
    return pl.pallas_call(
        _ffn_up_kernel,
        grid=(m // tm, n // tn),
        in_specs=[
            pl.BlockSpec((tm, d), lambda i, j: (i, 0)),
            pl.BlockSpec((None, 6, d), lambda i, j: (_mod_row(i, tm, rpm), 0, 0)),
            pl.BlockSpec((None, 1, d), lambda i, j: (layer, 0, 0)),
            w_spec, w_spec,
        ],
        out_specs=pl.BlockSpec((tm, tn), lambda i, j: (i, j)),
        out_shape=jax.ShapeDtypeStruct((m, n), BF16),
        scratch_shapes=[pltpu.VMEM((tm, d), BF16)],
        compiler_params=_params(("arbitrary", "arbitrary")),
        name="ffn_up",
    )(x, mod, g, w_gate, w_up)


def _final_norm_kernel(x_ref, g_ref, o_ref):
    x = x_ref[...]
    o_ref[...] = (x * lax.rsqrt(jnp.mean(x * x, axis=-1, keepdims=True) + NORM_EPS)) * g_ref[...]


def _final_norm(x, tok0, n_tok, g):
    d = x.shape[1]
    tm = _pick(math.gcd(tok0, n_tok) if tok0 else n_tok, (512, 256, 128, 64))
    return pl.pallas_call(
        _final_norm_kernel,
        grid=(n_tok // tm,),
        in_specs=[pl.BlockSpec((tm, d), lambda i: (tok0 // tm + i, 0)), pl.BlockSpec((1, d), lambda i: (0, 0))],
        out_specs=pl.BlockSpec((tm, d), lambda i: (i, 0)),
        out_shape=jax.ShapeDtypeStruct((n_tok, d), F32),
        compiler_params=_params(("arbitrary",)),
        name="final_norm",
    )(x, g)


def kernel(x_prompt, x_sample, c, cache_k, cache_v, state_lru, c_ctx, w_mod, b_mod, norm_mix, norm_ffn, w_in, w_gate, att_lambda, att_subln, lru_conv_w, lru_conv_b, lru_wa, lru_ba, lru_wx, lru_bx, lru_lambda, hy_conv_w, hy_conv_b, hy_w1, hy_b1, hy_w2, hy_b2, hy_w3, hy_b3, hy_freq, hy_decay, hy_d, w_br, w_out, w_ff_gate, w_ff_up, w_ff_down, final_norm):
    b_c, l_c, d = x_prompt.shape
    b_s, l_s, _ = x_sample.shape
    depth = w_in.shape[0]
    n_heads, d_v = cache_v.shape[3], cache_v.shape[4]
    w_att = n_heads * d_v
    w_lru = lru_lambda.shape[-1]
    n_order, w_hy = hy_d.shape[1], hy_d.shape[2]
    m_ctx, m_s = b_c * l_c, b_s * l_s
    past = cache_k.shape[2]
    col_xl, col_gl, col_hy = 3 * w_att, 3 * w_att + w_lru, 3 * w_att + 2 * w_lru

    w_in_b, w_gate_b, w_br_b, w_out_b = (w.astype(BF16) for w in (w_in, w_gate, w_br, w_out))
    w_ffg_b, w_ffu_b, w_ffd_b = (w.astype(BF16) for w in (w_ff_gate, w_ff_up, w_ff_down))
    lru_wg = _lru_gate_weights(lru_wa, lru_wx)
    lru_bg = jnp.stack([lru_ba[:, 0], lru_bx[:, 0], lru_ba[:, 1], lru_bx[:, 1]], axis=1)
    norm_mix3, norm_ffn3 = norm_mix.reshape(depth, 1, d), norm_ffn.reshape(depth, 1, d)
    att_subln3 = att_subln.reshape(depth, 1, d_v)
    lru_conv_b3 = lru_conv_b.reshape(depth, 1, w_lru)
    hy_conv_b3 = hy_conv_b.reshape(depth, 1, -1)
    hy_d4 = hy_d.reshape(depth, n_order, 1, w_hy)

    n_rows = 1 + b_s
    assert n_rows <= SUBLANES
    c8 = jnp.zeros((SUBLANES, d), F32).at[0].set(c_ctx).at[1:n_rows].set(c)
    mod_all = _modulation(c8, w_mod, b_mod)[:, :n_rows].reshape(depth, n_rows, 6, d)

    rope = _rope_tables(l_s, d_v // 2)
    dft = {}
    for l in sorted({l_c, l_s}):
        a_mat = jnp.asarray(_dft_matrices(l)).astype(BF16)
        dft[l] = (a_mat, a_mat.T)
    h0_ctx = jnp.zeros((b_c, 2, w_lru), F32)

    def trunk_layer(x, mod, rpm, layer, b, l, rope, cache, h0, want_kv):
        lam_init = 0.8 - 0.6 * math.exp(-0.3 * layer)
        proj, h, *kv = _norm_proj(x, mod, norm_mix3, w_in_b, layer, rpm, w_att if want_kv else None)
        att = _attention(proj, b, l, n_heads, d_v, lam_init, att_lambda, att_subln3, layer, rope=rope, cache=cache)
        lru, st = _lru(proj, 0, b, l, col_xl, col_gl, w_lru, lru_conv_w, lru_conv_b3, lru_wg, lru_bg, lru_lambda,
                       h0, layer)
        a_mat, at_mat = dft[l]
        filt = _hy_filters(l, hy_w1[layer], hy_b1[layer], hy_freq[layer], hy_w2[layer], hy_b2[layer],
                           hy_w3[layer], hy_b3[layer], hy_decay[layer])
        fspec = _filter_spectrum(a_mat, filt, n_order, w_hy)
        hyo = _hyena(proj, 0, b, l, col_hy, w_hy, n_order, a_mat, at_mat, fspec, hy_conv_w, hy_conv_b3, hy_d4, layer)
        mixed = _merge(h, w_gate_b, (att, lru, hyo), w_br_b, layer)
        x = _proj_residual(mixed, w_out_b, x, mod, 2, layer, rpm)
        u = _ffn_up(x, mod, norm_ffn3, w_ffg_b, w_ffu_b, layer, rpm)
        x = _proj_residual(u, w_ffd_b, x, mod, 5, layer, rpm)
        return x, kv, st

    xp = x_prompt.reshape(m_ctx, d)
    ks, vs, ss = [], [], []
    for layer in range(depth):
        xp, (k_l, v_l), s_l = trunk_layer(xp, mod_all[layer, :1], m_ctx, layer, b_c, l_c, None, None, h0_ctx, True)
        ks.append(k_l.reshape(b_c, l_c, n_heads, d_v))
        vs.append(v_l.reshape(b_c, l_c, n_heads, d_v))
        ss.append(s_l)

    xs = x_sample.reshape(m_s, d)
    for layer in range(depth):
        ck = cache_k[:, layer].reshape(b_s, past, w_att)
        cv = cache_v[:, layer].reshape(b_s, past, w_att)
        xs, _, _ = trunk_layer(xs, mod_all[layer, 1:], l_s, layer, b_s, l_s, rope, (ck, cv), state_lru[:, layer], False)

    g_fin = final_norm.reshape(1, d)
    y_prompt = _final_norm(xp, 0, m_ctx, g_fin).reshape(b_c, l_c, d)
    y_sample = _final_norm(xs, 0, m_s, g_fin).reshape(b_s, l_s, d)
    return (y_prompt, y_sample, jnp.stack(ks, axis=1), jnp.stack(vs, axis=1), jnp.stack(ss, axis=1))
```

```python
import functools
import math

import jax
import jax.numpy as jnp
import numpy as np
from jax import lax
from jax.experimental import pallas as pl
from jax.experimental.pallas import tpu as pltpu

F32 = jnp.float32
BF16 = jnp.bfloat16

GRID_W = 64
ROPE_THETA = 10000.0
NORM_EPS = 1e-6
SUBLN_EPS = 1e-5
LRU_C = 8.0
HY_BANDS = 8
LANES = 128
SUBLANES = 8
BF16_ROWS = 16
VMEM_LIMIT_BYTES = 56 * 1024 * 1024


def _pick(n, prefs):
    for p in prefs:
        if n % p == 0:
            return p
    raise ValueError(f"no tile in {prefs} divides {n}")


def _params(sem):
    return pltpu.CompilerParams(dimension_semantics=sem, vmem_limit_bytes=VMEM_LIMIT_BYTES)


def _row_tile(rpm):
    return _pick(rpm, (1024, 512, 256, 128))


def _mod_row(i, tm, rpm):
    return i // (rpm // tm)


def _rms_mod(x, g, sc, sh):
    y = x * lax.rsqrt(jnp.mean(x * x, axis=-1, keepdims=True) + NORM_EPS)
    return y * (g * (1.0 + sc)) + sh


def _mod_kernel(c_ref, w_ref, b_ref, o_ref):
    c = c_ref[...]
    s = (c * jax.nn.sigmoid(c)).astype(BF16)
    o_ref[...] = jnp.dot(s, w_ref[...].astype(BF16), preferred_element_type=F32) + b_ref[...]


def _modulation(c8, w_mod, b_mod):
    depth, d, n = w_mod.shape
    tn = _pick(n, (1024, 512, 256, 128))
    return pl.pallas_call(
        _mod_kernel,
        grid=(depth, n // tn),
        in_specs=[
            pl.BlockSpec((SUBLANES, d), lambda l, j: (0, 0)),
            pl.BlockSpec((None, d, tn), lambda l, j: (l, 0, j)),
            pl.BlockSpec((None, 1, tn), lambda l, j: (l, 0, j)),
        ],
        out_specs=pl.BlockSpec((None, SUBLANES, tn), lambda l, j: (l, 0, j)),
        out_shape=jax.ShapeDtypeStruct((depth, SUBLANES, n), F32),
        compiler_params=_params(("arbitrary", "arbitrary")),
        name="modulation",
    )(c8, w_mod, b_mod.reshape(depth, 1, n))


PROLOGUE_ROWS = 128


def _norm_tile(x_ref, mod_ref, g_ref, h_ref, sc_row, sh_row):
    tm = x_ref.shape[0]
    ch = math.gcd(tm, PROLOGUE_ROWS)

    def body(c, carry):
        r0 = pl.multiple_of(c * ch, ch)
        h = _rms_mod(x_ref[pl.ds(r0, ch), :], g_ref[...], mod_ref[sc_row:sc_row + 1, :], mod_ref[sh_row:sh_row + 1, :])
        h_ref[pl.ds(r0, ch), :] = h.astype(h_ref.dtype)
        return carry

    lax.fori_loop(0, tm // ch, body, 0)


def _norm_proj_kernel(x_ref, mod_ref, g_ref, w_ref, o_ref, h_ref):
    @pl.when(pl.program_id(1) == 0)
    def _():
        _norm_tile(x_ref, mod_ref, g_ref, h_ref, 1, 0)

    o_ref[...] = jnp.dot(h_ref[...], w_ref[...], preferred_element_type=F32)


def _norm_proj(x, mod, g, w, layer, rpm):
    m, d = x.shape
    n = w.shape[-1]
    tm = _row_tile(rpm)
    tn = _pick(n, (1024, 512, 256, 128))
    return pl.pallas_call(
        _norm_proj_kernel,
        grid=(m // tm, n // tn),
        in_specs=[
            pl.BlockSpec((tm, d), lambda i, j: (i, 0)),
            pl.BlockSpec((None, 6, d), lambda i, j: (_mod_row(i, tm, rpm), 0, 0)),
            pl.BlockSpec((None, 1, d), lambda i, j: (layer, 0, 0)),
            pl.BlockSpec((None, d, tn), lambda i, j: (layer, 0, j)),
        ],
        out_specs=[pl.BlockSpec((tm, tn), lambda i, j: (i, j)), pl.BlockSpec((tm, d), lambda i, j: (i, 0))],
        out_shape=[jax.ShapeDtypeStruct((m, n), F32), jax.ShapeDtypeStruct((m, d), BF16)],
        compiler_params=_params(("arbitrary", "arbitrary")),
        name="norm_proj",
    )(x, mod, g, w)


def _rope(x, cos, sin_signed):
    n = x.shape[-1]
    half = 16
    lane = lax.broadcasted_iota(jnp.int32, x.shape, 1)
    swapped = jnp.where((lane % (2 * half)) < half, pltpu.roll(x, n - half, 1), pltpu.roll(x, half, 1))
    return x * cos + swapped * sin_signed


def _attn_kernel(*refs, lam_init, d_qk, n_hb, use_rope, use_cache, emit_kv):
    it = iter(refs)
    lam_ref, sub_ref, q_ref, k_ref, v_ref = next(it), next(it), next(it), next(it), next(it)
    if use_rope:
        cq_ref, sq_ref, ck_ref, sk_ref = next(it), next(it), next(it), next(it)
    if use_cache:
        pk_ref, pv_ref = next(it), next(it)
    o_ref = next(it)
    if emit_kv:
        ko_ref, vo_ref = next(it), next(it)
    k_scr = next(it)
    d_v = 2 * d_qk
    heads = [slice(hh * d_v, (hh + 1) * d_v) for hh in range(n_hb)]

    @pl.when(pl.program_id(2) == 0)
    def _():
        if emit_kv:
            ko_ref[...] = k_ref[...]
            vo_ref[...] = v_ref[...]
        for sl in heads:
            k = k_ref[:, sl]
            if use_rope:
                k = _rope(k, ck_ref[...], sk_ref[...])
            k_scr[:, sl] = k.astype(BF16)

    a = lam_ref[...]
    lam = (jnp.exp(jnp.sum(a[0:1] * a[1:2], axis=-1, keepdims=True))
           - jnp.exp(jnp.sum(a[2:3] * a[3:4], axis=-1, keepdims=True)) + lam_init)
    scale = d_qk ** -0.5 * math.log2(math.e)
    nt = (((1,), (1,)), ((), ()))

    for sl in heads:
        q = q_ref[:, sl]
        if use_rope:
            q = _rope(q, cq_ref[...], sq_ref[...])
        q = q * scale
        lane = lax.broadcasted_iota(jnp.int32, q.shape, 1)
        q1 = jnp.where(lane < d_qk, q, 0.0).astype(BF16)
        q2 = jnp.where(lane >= d_qk, q, 0.0).astype(BF16)
        keys = [k_scr[:, sl]]
        vals = [v_ref[:, sl].astype(BF16)]
        if use_cache:
            keys.append(pk_ref[:, sl].astype(BF16))
            vals.append(pv_ref[:, sl].astype(BF16))

        def softmax_v(qh):
            s = [lax.dot_general(qh, kk, nt, preferred_element_type=F32) for kk in keys]
            mx = functools.reduce(jnp.maximum, [jnp.max(x, axis=-1, keepdims=True) for x in s])
            p = [jnp.exp2(x - mx) for x in s]
            den = functools.reduce(jnp.add, [jnp.sum(x, axis=-1, keepdims=True) for x in p])
            pv = functools.reduce(jnp.add, [jnp.dot(x.astype(BF16), vv, preferred_element_type=F32)
                                            for x, vv in zip(p, vals)])
            return pv, 1.0 / den

        o1, r1 = softmax_v(q1)
        o2, r2 = softmax_v(q2)
        o = o1 * r1 - o2 * (lam * r2)
        y = o * lax.rsqrt(jnp.mean(o * o, axis=-1, keepdims=True) + SUBLN_EPS)
        o_ref[:, sl] = ((y * sub_ref[...]) * (1.0 - lam_init)).astype(o_ref.dtype)


def _attention(proj, b, l, h, d_v, lam_init, att_lambda, att_subln, layer, rope=None, cache=None, emit_kv=False):
    assert d_v == LANES
    tq = _pick(l, (256, 128, 64))
    nq = l // tq
    n_hb = max(n for n in range(1, h + 1) if h % n == 0 and n * l <= max(l, 2048))
    wb = n_hb * d_v
    hg = h // n_hb
    in_specs = [
        pl.BlockSpec((None, 4, att_lambda.shape[-1]), lambda bi, hi, qi: (layer, 0, 0)),
        pl.BlockSpec((None, 1, d_v), lambda bi, hi, qi: (layer, 0, 0)),
        pl.BlockSpec((tq, wb), lambda bi, hi, qi: (bi * nq + qi, hi)),
        pl.BlockSpec((l, wb), lambda bi, hi, qi: (bi, hg + hi)),
        pl.BlockSpec((l, wb), lambda bi, hi, qi: (bi, 2 * hg + hi)),
    ]
    args = [att_lambda, att_subln, proj, proj, proj]
    if rope is not None:
        cos, sin = rope
        in_specs += [
            pl.BlockSpec((tq, d_v), lambda bi, hi, qi: (qi, 0)),
            pl.BlockSpec((tq, d_v), lambda bi, hi, qi: (qi, 0)),
            pl.BlockSpec((l, d_v), lambda bi, hi, qi: (0, 0)),
            pl.BlockSpec((l, d_v), lambda bi, hi, qi: (0, 0)),
        ]
        args += [cos, sin, cos, sin]
    if cache is not None:
        ck, cv = cache
        past = ck.shape[1]
        in_specs += [
            pl.BlockSpec((None, past, wb), lambda bi, hi, qi: (bi, 0, hi)),
            pl.BlockSpec((None, past, wb), lambda bi, hi, qi: (bi, 0, hi)),
        ]
        args += [ck, cv]
    out_specs = [pl.BlockSpec((tq, wb), lambda bi, hi, qi: (bi * nq + qi, hi))]
    out_shape = [jax.ShapeDtypeStruct((b * l, h * d_v), BF16)]
    if emit_kv:
        out_specs += [pl.BlockSpec((l, wb), lambda bi, hi, qi: (bi, hi))] * 2
        out_shape += [jax.ShapeDtypeStruct((b * l, h * d_v), F32)] * 2
    return pl.pallas_call(
        functools.partial(_attn_kernel, lam_init=lam_init, d_qk=d_v // 2, n_hb=n_hb,
                          use_rope=rope is not None, use_cache=cache is not None, emit_kv=emit_kv),
        grid=(b, hg, nq),
        in_specs=in_specs,
        out_specs=out_specs,
        out_shape=out_shape,
        scratch_shapes=[pltpu.VMEM((l, wb), BF16)],
        compiler_params=_params(("arbitrary", "arbitrary", "arbitrary")),
        name="diff_attention",
    )(*args)


def _rope_tables(l, d_qk):
    n_freq = d_qk // 4
    lane = jnp.arange(2 * d_qk)
    axis = (lane % d_qk) // (2 * n_freq)
    freq = lane % n_freq
    first = (lane % (2 * n_freq)) < n_freq
    inv_freq = ROPE_THETA ** (-freq.astype(F32) / n_freq)
    t = jnp.arange(l)
    pos = jnp.where(axis[None, :] == 0, (t // GRID_W)[:, None], (t % GRID_W)[:, None]).astype(F32)
    ang = pos * inv_freq[None, :]
    return jnp.cos(ang), jnp.where(first[None, :], -jnp.sin(ang), jnp.sin(ang))


def _shift_rows(x, d):
    n = x.shape[0]
    if d == 0:
        return x
    row = lax.broadcasted_iota(jnp.int32, x.shape, 0)
    rolled = pltpu.roll(x, (-d) % n, 0)
    ok = (row + d >= 0) & (row + d < n)
    return jnp.where(ok, rolled, 0.0)


def _dwconv(x, w, bias, pad_left):
    y = bias + w[0:1] * _shift_rows(x, -pad_left)
    for k in range(1, w.shape[0]):
        y = y + w[k:k + 1] * _shift_rows(x, k - pad_left)
    return y


def _sigmoid(x):
    return 0.5 * jnp.tanh(0.5 * x) + 0.5


SCAN_UNROLL = 8


def _lru_kernel(xl_ref, gl_ref, cw_ref, cb_ref, wg_ref, bg_ref, lam_ref, h0_ref, o_ref, st_ref,
                a_scr, b_scr, hf_scr, *, pad_left):
    l, cb = xl_ref.shape
    n_grp = cb // LANES
    xc = _dwconv(xl_ref[...], cw_ref[...], cb_ref[...], pad_left)
    lam = lam_ref[...]
    neg = -lam
    softplus = jnp.maximum(neg, 0.0) + jnp.log1p(jnp.exp(-jnp.abs(neg)))
    row8 = lax.broadcasted_iota(jnp.int32, (SUBLANES, cb), 0)
    n_chunks = l // SUBLANES

    def fill(direction):
        for g in range(n_grp):
            sl = slice(g * LANES, (g + 1) * LANES)
            xg = xc[:, sl]
            y = jnp.dot(xg.astype(BF16), wg_ref[g, :, 2 * direction * LANES:(2 * direction + 2) * LANES],
                        preferred_element_type=F32)
            r = _sigmoid(y[:, :LANES] + bg_ref[2 * direction:2 * direction + 1, sl])
            i = _sigmoid(y[:, LANES:] + bg_ref[2 * direction + 1:2 * direction + 2, sl])
            log_a = (-LRU_C * r) * softplus[direction:direction + 1, sl]
            a = jnp.exp(log_a)
            a_scr[:, sl] = a
            b_scr[:, sl] = jnp.sqrt((1.0 - a) * (1.0 + a)) * (i * xg)

    fill(0)

    def fwd(c, h):
        r0 = pl.multiple_of(c * SUBLANES, SUBLANES)
        a = a_scr[pl.ds(r0, SUBLANES), :]
        b = b_scr[pl.ds(r0, SUBLANES), :]
        for s in (1, 2, 4):
            keep = row8 >= s
            b = jnp.where(keep, a * pltpu.roll(b, s, 0) + b, b)
            a = jnp.where(keep, a * pltpu.roll(a, s, 0), a)
        h8 = a * h + b
        hf_scr[pl.ds(r0, SUBLANES), :] = h8
        return h8[SUBLANES - 1:SUBLANES, :]

    s_f = lax.fori_loop(0, n_chunks, fwd, h0_ref[0:1, :], unroll=SCAN_UNROLL)

    fill(1)

    def bwd(c, h):
        r0 = pl.multiple_of((n_chunks - 1 - c) * SUBLANES, SUBLANES)
        a = a_scr[pl.ds(r0, SUBLANES), :]
        b = b_scr[pl.ds(r0, SUBLANES), :]
        for s in (1, 2, 4):
            keep = row8 < SUBLANES - s
            b = jnp.where(keep, a * pltpu.roll(b, SUBLANES - s, 0) + b, b)
            a = jnp.where(keep, a * pltpu.roll(a, SUBLANES - s, 0), a)
        h8 = a * h + b
        b_scr[pl.ds(r0, SUBLANES), :] = h8
        return h8[0:1, :]

    s_b = lax.fori_loop(0, n_chunks, bwd, h0_ref[1:2, :], unroll=SCAN_UNROLL)

    st_ref[0:1, :] = s_f
    st_ref[1:2, :] = s_b
    o_ref[...] = ((hf_scr[...] + b_scr[...]) * jax.nn.gelu(gl_ref[...])).astype(o_ref.dtype)


def _lru(proj, tok0, b, l, col_x, col_g, w_lru, conv_w, conv_b, wg, bg, lam, h0, layer):
    cb = _pick(w_lru, tuple(c for c in (1024, 512, 256, 128) if c == 128 or l * c <= 512 * 1024))
    assert tok0 % l == 0 and col_x % cb == 0 and col_g % cb == 0
    k = conv_w.shape[1]
    n_grp = cb // LANES
    out, st = pl.pallas_call(
        functools.partial(_lru_kernel, pad_left=k // 2),
        grid=(b, w_lru // cb),
        in_specs=[
            pl.BlockSpec((l, cb), lambda bi, j: (tok0 // l + bi, col_x // cb + j)),
            pl.BlockSpec((l, cb), lambda bi, j: (tok0 // l + bi, col_g // cb + j)),
            pl.BlockSpec((None, k, cb), lambda bi, j: (layer, 0, j)),
            pl.BlockSpec((None, 1, cb), lambda bi, j: (layer, 0, j)),
            pl.BlockSpec((None, n_grp, LANES, 4 * LANES), lambda bi, j: (layer, j, 0, 0)),
            pl.BlockSpec((None, 4, cb), lambda bi, j: (layer, 0, j)),
            pl.BlockSpec((None, 2, cb), lambda bi, j: (layer, 0, j)),
            pl.BlockSpec((None, 2, cb), lambda bi, j: (bi, 0, j)),
        ],
        out_specs=[
            pl.BlockSpec((l, cb), lambda bi, j: (bi, j)),
            pl.BlockSpec((None, 2, cb), lambda bi, j: (bi, 0, j)),
        ],
        out_shape=[jax.ShapeDtypeStruct((b * l, w_lru), BF16), jax.ShapeDtypeStruct((b, 2, w_lru), F32)],
        scratch_shapes=[pltpu.VMEM((l, cb), F32)] * 3,
        compiler_params=_params(("arbitrary", "arbitrary")),
        name="rglru",
    )(proj, proj, conv_w, conv_b, wg, bg, lam, h0)
    return out, st


def _lru_gate_weights(wa, wx):
    depth, _, nb, bs, _ = wa.shape
    per = LANES // bs
    eye = jnp.eye(per, dtype=wa.dtype)

    def bd(w):
        w = w.reshape(depth, nb // per, per, bs, bs)
        return jnp.einsum("lgpde,pq->lgpdqe", w, eye).reshape(depth, nb // per, LANES, LANES)

    return jnp.concatenate([bd(wa[:, 0]), bd(wx[:, 0]), bd(wa[:, 1]), bd(wx[:, 1])], axis=-1).astype(BF16)


def _hy_conv_kernel(x_ref, w_ref, b_ref, o_ref, *, pad_left):
    o_ref[...] = _dwconv(x_ref[...], w_ref[...], b_ref[...], pad_left)


def _hy_conv(proj, tok0, b, l, col0, width, conv_w, conv_b, layer):
    cb = _pick(width, (512, 256, 128))
    assert tok0 % l == 0 and col0 % cb == 0
    k = conv_w.shape[1]
    return pl.pallas_call(
        functools.partial(_hy_conv_kernel, pad_left=k // 2),
        grid=(b, width // cb),
        in_specs=[
            pl.BlockSpec((l, cb), lambda bi, j: (tok0 // l + bi, col0 // cb + j)),
            pl.BlockSpec((None, k, cb), lambda bi, j: (layer, 0, j)),
            pl.BlockSpec((None, 1, cb), lambda bi, j: (layer, 0, j)),
        ],
        out_specs=pl.BlockSpec((l, cb), lambda bi, j: (bi, j)),
        out_shape=jax.ShapeDtypeStruct((b * l, width), F32),
        compiler_params=_params(("arbitrary", "arbitrary")),
        name="hyena_dwconv",
    )(proj, conv_w, conv_b)


def _hy_filter_kernel(z_ref, w1_ref, b1_ref, fr_ref, w2_ref, b2_ref, w3_ref, b3_ref, dec_ref, o_ref, h_scr):
    hi = lax.Precision.HIGHEST

    @pl.when(pl.program_id(0) == 0)
    def _():
        fr = fr_ref[...]
        h = jnp.sin(fr * (jnp.dot(z_ref[...], w1_ref[...], precision=hi, preferred_element_type=F32) + b1_ref[...]))
        h_scr[...] = jnp.sin(fr * (jnp.dot(h, w2_ref[...], precision=hi, preferred_element_type=F32) + b2_ref[...]))

    filt = jnp.dot(h_scr[...], w3_ref[...], precision=hi, preferred_element_type=F32) + b3_ref[...]
    o_ref[...] = filt * jnp.exp(-z_ref[:, 0:1] * jnp.abs(dec_ref[...]))


def _pad2(x, rows, cols):
    return jnp.pad(x, ((0, rows - x.shape[0]), (0, cols - x.shape[1])))


def _hy_filters(l, w1, b1, freq, w2, b2, w3, b3, decay):
    n = w3.shape[-1]
    t = jnp.linspace(0.0, 1.0, l, dtype=F32)[:, None]
    w = 2.0 * math.pi * jnp.arange(l, dtype=F32)[:, None] / l
    f = jnp.linspace(1e-4, HY_BANDS - 1, HY_BANDS, dtype=F32)[None]
    z = _pad2(jnp.concatenate([t, jnp.cos(f * w), -jnp.sin(f * w)], axis=-1), l, LANES)
    tn = _pick(n, (512, 256, 128))
    row = lambda v: _pad2(v.reshape(1, -1), 1, LANES)
    full = lambda shape: pl.BlockSpec(shape, lambda j: (0, 0))
    return pl.pallas_call(
        _hy_filter_kernel,
        grid=(n // tn,),
        in_specs=[full((l, LANES)), full((LANES, LANES)), full((1, LANES)), full((1, LANES)),
                  full((LANES, LANES)), full((1, LANES)),
                  pl.BlockSpec((LANES, tn), lambda j: (0, j)),
                  pl.BlockSpec((1, tn), lambda j: (0, j)),
                  pl.BlockSpec((1, tn), lambda j: (0, j))],
        out_specs=pl.BlockSpec((l, tn), lambda j: (0, j)),
        out_shape=jax.ShapeDtypeStruct((l, n), F32),
        scratch_shapes=[pltpu.VMEM((l, LANES), F32)],
        compiler_params=_params(("arbitrary",)),
        name="hyena_filters",
    )(z, _pad2(w1, LANES, LANES), row(b1), row(freq), _pad2(w2, LANES, LANES), row(b2),
      _pad2(w3, LANES, n), b3.reshape(1, n), decay.reshape(1, n))


def _dft_tile(l):
    return _pick(2 * l, (512, 256, 128))


@functools.lru_cache(maxsize=None)
def _dft_matrices(l):
    tile = _dft_tile(l)
    half = tile // 2
    r = np.arange(2 * l)
    k = (r // tile) * half + (r % half)
    is_im = (r % tile) >= half
    t = np.arange(l)
    ph = ((2 * k + 1)[:, None] * t[None, :]) % (4 * l)
    ang = ph.astype(np.float64) * (math.pi / (2 * l))
    return np.where(is_im[:, None], -np.sin(ang), np.cos(ang)).astype(np.float32)


def _cmul_store(o_ref, s, f_re, f_im):
    half = s.shape[0] // 2
    s_re, s_im = s[:half], s[half:]
    o_ref[:half, :] = (s_re * f_re - s_im * f_im).astype(o_ref.dtype)
    o_ref[half:, :] = (s_re * f_im + s_im * f_re).astype(o_ref.dtype)


def _filter_spec_kernel(a_ref, hf_ref, hb_ref, o_ref, hf_scr, hb_scr):
    @pl.when(pl.program_id(2) == 0)
    def _():
        hf_scr[...] = hf_ref[...].astype(BF16)
        hb = hb_ref[...]
        row = lax.broadcasted_iota(jnp.int32, hb.shape, 0)
        hb_scr[...] = jnp.where(row == 0, 0.0, hb).astype(BF16)

    a = a_ref[...]
    sf = jnp.dot(a, hf_scr[...], preferred_element_type=F32)
    sb = jnp.dot(a, hb_scr[...], preferred_element_type=F32)
    half = a.shape[0] // 2
    o_ref[:half, :] = sf[:half] + sb[:half]
    o_ref[half:, :] = sf[half:] - sb[half:]


def _filter_spectrum(a_mat, filt, n_order, w_hy):
    n2, l = a_mat.shape
    tm = _dft_tile(l)
    tn = _pick(w_hy, (512, 256, 128))
    nj = w_hy // tn
    return pl.pallas_call(
        _filter_spec_kernel,
        grid=(n_order, nj, n2 // tm),
        in_specs=[
            pl.BlockSpec((tm, l), lambda o, j, i: (i, 0)),
            pl.BlockSpec((l, tn), lambda o, j, i: (0, (2 * o) * nj + j)),
            pl.BlockSpec((l, tn), lambda o, j, i: (0, (2 * o + 1) * nj + j)),
        ],
        out_specs=pl.BlockSpec((None, tm, tn), lambda o, j, i: (o, i, j)),
        out_shape=jax.ShapeDtypeStruct((n_order, n2, w_hy), F32),
        scratch_shapes=[pltpu.VMEM((l, tn), BF16)] * 2,
        compiler_params=_params(("arbitrary", "arbitrary", "arbitrary")),
        name="hyena_filter_spectrum",
    )(a_mat, filt, filt)


def _dft_fwd_kernel(a_ref, z_ref, f_ref, o_ref, z_scr):
    @pl.when(pl.program_id(2) == 0)
    def _():
        z_scr[...] = z_ref[...].astype(BF16)

    s = jnp.dot(a_ref[...], z_scr[...], preferred_element_type=F32)
    half = s.shape[0] // 2
    _cmul_store(o_ref, s, f_ref[:half, :], f_ref[half:, :])


def _dft_fwd(a_mat, z, z_col0, fspec, order, b, l, w_hy):
    n2 = a_mat.shape[0]
    tm = _dft_tile(l)
    tn = _pick(w_hy, (1024, 512, 256, 128))
    assert z_col0 % tn == 0
    return pl.pallas_call(
        _dft_fwd_kernel,
        grid=(b, w_hy // tn, n2 // tm),
        in_specs=[
            pl.BlockSpec((tm, l), lambda bi, j, i: (i, 0)),
            pl.BlockSpec((l, tn), lambda bi, j, i: (bi, z_col0 // tn + j)),
            pl.BlockSpec((None, tm, tn), lambda bi, j, i: (order, i, j)),
        ],
        out_specs=pl.BlockSpec((None, tm, tn), lambda bi, j, i: (bi, i, j)),
        out_shape=jax.ShapeDtypeStruct((b, n2, w_hy), BF16),
        scratch_shapes=[pltpu.VMEM((l, tn), BF16)],
        compiler_params=_params(("arbitrary", "arbitrary", "arbitrary")),
        name="hyena_dft_fwd",
    )(a_mat, z, fspec)


def _dft_inv_kernel(at_ref, y_ref, z_ref, g_ref, d_ref, o_ref, *, inv_l):
    y = jnp.dot(at_ref[...], y_ref[...], preferred_element_type=F32) * inv_l
    z = z_ref[...]
    o_ref[...] = (g_ref[...] * (y + d_ref[...] * z)).astype(o_ref.dtype)


def _dft_inv(at_mat, y, z, z_col0, gate, gate_col0, d, layer, order, b, l, w_hy, out_dtype):
    n2 = at_mat.shape[1]
    tm = _pick(l, (512, 256, 128, 64))
    tn = _pick(w_hy, (1024, 512, 256, 128))
    nt = l // tm
    assert z_col0 % tn == 0 and gate_col0 % tn == 0
    return pl.pallas_call(
        functools.partial(_dft_inv_kernel, inv_l=1.0 / l),
        grid=(b, w_hy // tn, nt),
        in_specs=[
            pl.BlockSpec((tm, n2), lambda bi, j, i: (i, 0)),
            pl.BlockSpec((None, n2, tn), lambda bi, j, i: (bi, 0, j)),
            pl.BlockSpec((tm, tn), lambda bi, j, i: (bi * nt + i, z_col0 // tn + j)),
            pl.BlockSpec((tm, tn), lambda bi, j, i: (bi * nt + i, gate_col0 // tn + j)),
            pl.BlockSpec((None, None, 1, tn), lambda bi, j, i: (layer, order, 0, j)),
        ],
        out_specs=pl.BlockSpec((tm, tn), lambda bi, j, i: (bi * nt + i, j)),
        out_shape=jax.ShapeDtypeStruct((b * l, w_hy), out_dtype),
        compiler_params=_params(("arbitrary", "arbitrary", "arbitrary")),
        name="hyena_dft_inv",
    )(at_mat, y, z, gate, d)


def _hyena_fused_kernel(*refs, n_order, pad_left, tile, inv_l):
    it = iter(refs)
    a_ref, at_ref, f_ref, d_ref = next(it), next(it), next(it), next(it)
    x_refs = [next(it) for _ in range(n_order + 1)]
    w_refs = [next(it) for _ in range(n_order + 1)]
    b_refs = [next(it) for _ in range(n_order + 1)]
    o_ref = next(it)
    half = tile // 2
    n_tiles = a_ref.shape[0] // tile
    conv = lambda n: _dwconv(x_refs[n][...], w_refs[n][...], b_refs[n][...], pad_left)
    z = conv(0)
    for o in range(n_order):
        zb = z.astype(BF16)
        y = None
        for ti in range(n_tiles):
            r0 = ti * tile
            s = jnp.dot(a_ref[r0:r0 + tile, :], zb, preferred_element_type=F32)
            s_re, s_im = s[:half], s[half:]
            f_re, f_im = f_ref[o, r0:r0 + half, :], f_ref[o, r0 + half:r0 + tile, :]
            spec = jnp.concatenate([s_re * f_re - s_im * f_im, s_re * f_im + s_im * f_re], axis=0).astype(BF16)
            t = jnp.dot(at_ref[:, r0:r0 + tile], spec, preferred_element_type=F32)
            y = t if y is None else y + t
        z = conv(o + 1) * (y * inv_l + d_ref[o] * z)
    o_ref[...] = z.astype(o_ref.dtype)


def _hyena_fused(proj, tok0, b, l, col0, w_hy, n_order, a_mat, at_mat, fspec, conv_w, conv_b, hy_d, layer):
    n2 = a_mat.shape[0]
    tn = _pick(w_hy, (1024, 512, 256, 128))
    nj = w_hy // tn
    k = conv_w.shape[1]
    assert tok0 % l == 0 and col0 % tn == 0
    col = lambda n: (col0 + n * w_hy) // tn
    x_specs = [pl.BlockSpec((l, tn), lambda j, bi, n=n: (tok0 // l + bi, col(n) + j)) for n in range(n_order + 1)]
    w_specs = [pl.BlockSpec((None, k, tn), lambda j, bi, n=n: (layer, 0, n * nj + j)) for n in range(n_order + 1)]
    b_specs = [pl.BlockSpec((None, 1, tn), lambda j, bi, n=n: (layer, 0, n * nj + j)) for n in range(n_order + 1)]
    return pl.pallas_call(
        functools.partial(_hyena_fused_kernel, n_order=n_order, pad_left=k // 2, tile=_dft_tile(l), inv_l=1.0 / l),
        grid=(nj, b),
        in_specs=[
            pl.BlockSpec((n2, l), lambda j, bi: (0, 0)),
            pl.BlockSpec((l, n2), lambda j, bi: (0, 0)),
            pl.BlockSpec((n_order, n2, tn), lambda j, bi: (0, 0, j)),
            pl.BlockSpec((None, n_order, 1, tn), lambda j, bi: (layer, 0, 0, j)),
            *x_specs, *w_specs, *b_specs,
        ],
        out_specs=pl.BlockSpec((l, tn), lambda j, bi: (bi, j)),
        out_shape=jax.ShapeDtypeStruct((b * l, w_hy), BF16),
        compiler_params=_params(("arbitrary", "arbitrary")),
        name="hyena_fused",
    )(a_mat, at_mat, fspec, hy_d, *([proj] * (n_order + 1)), *([conv_w] * (n_order + 1)),
      *([conv_b] * (n_order + 1)))


HYENA_FUSED_MAX_L = 512


def _hyena(proj, tok0, b, l, col0, w_hy, n_order, a_mat, at_mat, fspec, conv_w, conv_b, hy_d, layer):
    if l <= HYENA_FUSED_MAX_L:
        return _hyena_fused(proj, tok0, b, l, col0, w_hy, n_order, a_mat, at_mat, fspec, conv_w, conv_b, hy_d, layer)
    hyc = _hy_conv(proj, tok0, b, l, col0, (n_order + 1) * w_hy, conv_w, conv_b, layer)
    z, z_col0 = hyc, 0
    for o in range(n_order):
        y = _dft_fwd(a_mat, z, z_col0, fspec, o, b, l, w_hy)
        last = o == n_order - 1
        z = _dft_inv(at_mat, y, z, z_col0, hyc, (o + 1) * w_hy, hy_d, layer, o, b, l, w_hy, BF16 if last else F32)
        z_col0 = 0
    return z


def _merge_kernel(h_ref, wg0_ref, wg1_ref, wg2_ref, b0_ref, b1_ref, b2_ref, wb0_ref, wb1_ref, wb2_ref, o_ref):
    h = h_ref[...]
    acc = None
    for wg_ref, b_ref, wb_ref in ((wg0_ref, b0_ref, wb0_ref), (wg1_ref, b1_ref, wb1_ref), (wg2_ref, b2_ref, wb2_ref)):
        gate = jax.nn.sigmoid(jnp.dot(h, wg_ref[...], preferred_element_type=F32))
        t = gate * jnp.dot(b_ref[...], wb_ref[...], preferred_element_type=F32)
        acc = t if acc is None else acc + t
    o_ref[...] = acc.astype(o_ref.dtype)


def _merge(h, w_gate, branches, w_br, layer):
    m, d = h.shape
    wb = w_br.shape[2]
    tm = _pick(m, (1024, 512, 256, 128))
    tn = _pick(d, (512, 256, 128))
    nj = d // tn
    gate_spec = lambda n: pl.BlockSpec((None, d, tn), lambda i, j: (layer, 0, n * nj + j))
    br_spec = pl.BlockSpec((tm, wb), lambda i, j: (i, 0))
    wbr_spec = lambda n: pl.BlockSpec((None, None, wb, tn), lambda i, j: (layer, n, 0, j))
    return pl.pallas_call(
        _merge_kernel,
        grid=(m // tm, nj),
        in_specs=[
            pl.BlockSpec((tm, d), lambda i, j: (i, 0)),
            gate_spec(0), gate_spec(1), gate_spec(2),
            br_spec, br_spec, br_spec,
            wbr_spec(0), wbr_spec(1), wbr_spec(2),
        ],
        out_specs=pl.BlockSpec((tm, tn), lambda i, j: (i, j)),
        out_shape=jax.ShapeDtypeStruct((m, d), BF16),
        compiler_params=_params(("arbitrary", "arbitrary")),
        name="gated_merge",
    )(h, w_gate, w_gate, w_gate, *branches, w_br, w_br, w_br)


def _proj_residual_kernel(a_ref, w_ref, x_ref, mod_ref, o_ref, *, gate_row):
    gate = mod_ref[gate_row:gate_row + 1, :]
    o_ref[...] = x_ref[...] + gate * jnp.dot(a_ref[...], w_ref[...], preferred_element_type=F32)


def _proj_residual(a, w, x, mod, gate_row, layer, rpm):
    m, k = a.shape
    d = x.shape[1]
    tm = _row_tile(rpm)
    tn = _pick(d, (512, 256, 128))
    return pl.pallas_call(
        functools.partial(_proj_residual_kernel, gate_row=gate_row),
        grid=(m // tm, d // tn),
        in_specs=[
            pl.BlockSpec((tm, k), lambda i, j: (i, 0)),
            pl.BlockSpec((None, k, tn), lambda i, j: (layer, 0, j)),
            pl.BlockSpec((tm, tn), lambda i, j: (i, j)),
            pl.BlockSpec((None, 6, tn), lambda i, j: (_mod_row(i, tm, rpm), 0, j)),
        ],
        out_specs=pl.BlockSpec((tm, tn), lambda i, j: (i, j)),
        out_shape=jax.ShapeDtypeStruct((m, d), F32),
        compiler_params=_params(("arbitrary", "arbitrary")),
        name="proj_residual",
    )(a, w, x, mod)


def _ffn_up_kernel(x_ref, mod_ref, g_ref, wg_ref, wu_ref, o_ref, h_scr):
    @pl.when(pl.program_id(1) == 0)
    def _():
        _norm_tile(x_ref, mod_ref, g_ref, h_scr, 4, 3)

    h = h_scr[...]
    a = jnp.dot(h, wg_ref[...], preferred_element_type=F32)
    u = jnp.dot(h, wu_ref[...], preferred_element_type=F32)
    o_ref[...] = ((a * jax.nn.sigmoid(a)) * u).astype(o_ref.dtype)


def _ffn_up(x, mod, g, w_gate, w_up, layer, rpm):
    m, d = x.shape
    n = w_gate.shape[-1]
    tm = _row_tile(rpm)
    tn = _pick(n, (512, 256, 128))
    w_spec = pl.BlockSpec((None, d, tn), lambda i, j: (layer, 0, j))
    return pl.pallas_call(
        _ffn_up_kernel,
        grid=(m // tm, n // tn),
        in_specs=[
            pl.BlockSpec((tm, d), lambda i, j: (i, 0)),
            pl.BlockSpec((None, 6, d), lambda i, j: (_mod_row(i, tm, rpm), 0, 0)),
            pl.BlockSpec((None, 1, d), lambda i, j: (layer, 0, 0)),
            w_spec, w_spec,
        ],
        out_specs=pl.BlockSpec((tm, tn), lambda i, j: (i, j)),
        out_shape=jax.ShapeDtypeStruct((m, n), BF16),
        scratch_shapes=[pltpu.VMEM((tm, d), BF16)],
        compiler_params=_params(("arbitrary", "arbitrary")),
        name="ffn_up",
    )(x, mod, g, w_gate, w_up)


def _final_norm_kernel(x_ref, g_ref, o_ref):
    x = x_ref[...]
    o_ref[...] = (x * lax.rsqrt(jnp.mean(x * x, axis=-1, keepdims=True) + NORM_EPS)) * g_ref[...]


def _final_norm(x, tok0, n_tok, g):
    d = x.shape[1]
    tm = _pick(math.gcd(tok0, n_tok) if tok0 else n_tok, (512, 256, 128, 64))
    return pl.pallas_call(
        _final_norm_kernel,
        grid=(n_tok // tm,),
        in_specs=[pl.BlockSpec((tm, d), lambda i: (tok0 // tm + i, 0)), pl.BlockSpec((1, d), lambda i: (0, 0))],
        out_specs=pl.BlockSpec((tm, d), lambda i: (i, 0)),
        out_shape=jax.ShapeDtypeStruct((n_tok, d), F32),
        compiler_params=_params(("arbitrary",)),
        name="final_norm",
    )(x, g)


def kernel(x_prompt, x_sample, c, cache_k, cache_v, state_lru, c_ctx, w_mod, b_mod, norm_mix, norm_ffn, w_in, w_gate, att_lambda, att_subln, lru_conv_w, lru_conv_b, lru_wa, lru_ba, lru_wx, lru_bx, lru_lambda, hy_conv_w, hy_conv_b, hy_w1, hy_b1, hy_w2, hy_b2, hy_w3, hy_b3, hy_freq, hy_decay, hy_d, w_br, w_out, w_ff_gate, w_ff_up, w_ff_down, final_norm):
    b_c, l_c, d = x_prompt.shape
    b_s, l_s, _ = x_sample.shape
    depth = w_in.shape[0]
    n_heads, d_v = cache_v.shape[3], cache_v.shape[4]
    w_att = n_heads * d_v
    w_lru = lru_lambda.shape[-1]
    n_order, w_hy = hy_d.shape[1], hy_d.shape[2]
    m_ctx, m_s = b_c * l_c, b_s * l_s
    past = cache_k.shape[2]
    col_xl, col_gl, col_hy = 3 * w_att, 3 * w_att + w_lru, 3 * w_att + 2 * w_lru

    w_in_b, w_gate_b, w_br_b, w_out_b = (w.astype(BF16) for w in (w_in, w_gate, w_br, w_out))
    w_ffg_b, w_ffu_b, w_ffd_b = (w.astype(BF16) for w in (w_ff_gate, w_ff_up, w_ff_down))
    lru_wg = _lru_gate_weights(lru_wa, lru_wx)
    lru_bg = jnp.stack([lru_ba[:, 0], lru_bx[:, 0], lru_ba[:, 1], lru_bx[:, 1]], axis=1)
    norm_mix3, norm_ffn3 = norm_mix.reshape(depth, 1, d), norm_ffn.reshape(depth, 1, d)
    att_subln3 = att_subln.reshape(depth, 1, d_v)
    lru_conv_b3 = lru_conv_b.reshape(depth, 1, w_lru)
    hy_conv_b3 = hy_conv_b.reshape(depth, 1, -1)
    hy_d4 = hy_d.reshape(depth, n_order, 1, w_hy)

    n_rows = 1 + b_s
    assert n_rows <= SUBLANES
    c8 = jnp.zeros((SUBLANES, d), F32).at[0].set(c_ctx).at[1:n_rows].set(c)
    mod_all = _modulation(c8, w_mod, b_mod)[:, :n_rows].reshape(depth, n_rows, 6, d)

    rope = _rope_tables(l_s, d_v // 2)
    dft = {}
    for l in sorted({l_c, l_s}):
        a_mat = jnp.asarray(_dft_matrices(l)).astype(BF16)
        dft[l] = (a_mat, a_mat.T)
    h0_ctx = jnp.zeros((b_c, 2, w_lru), F32)

    def trunk_layer(x, mod, rpm, layer, b, l, rope, cache, h0, want_kv):
        lam_init = 0.8 - 0.6 * math.exp(-0.3 * layer)
        proj, h = _norm_proj(x, mod, norm_mix3, w_in_b, layer, rpm)
        att, *kv = _attention(proj, b, l, n_heads, d_v, lam_init, att_lambda, att_subln3, layer, rope=rope,
                              cache=cache, emit_kv=want_kv)
        lru, st = _lru(proj, 0, b, l, col_xl, col_gl, w_lru, lru_conv_w, lru_conv_b3, lru_wg, lru_bg, lru_lambda,
                       h0, layer)
        a_mat, at_mat = dft[l]
        filt = _hy_filters(l, hy_w1[layer], hy_b1[layer], hy_freq[layer], hy_w2[layer], hy_b2[layer],
                           hy_w3[layer], hy_b3[layer], hy_decay[layer])
        fspec = _filter_spectrum(a_mat, filt, n_order, w_hy)
        hyo = _hyena(proj, 0, b, l, col_hy, w_hy, n_order, a_mat, at_mat, fspec, hy_conv_w, hy_conv_b3, hy_d4, layer)
        mixed = _merge(h, w_gate_b, (att, lru, hyo), w_br_b, layer)
        x = _proj_residual(mixed, w_out_b, x, mod, 2, layer, rpm)
        u = _ffn_up(x, mod, norm_ffn3, w_ffg_b, w_ffu_b, layer, rpm)
        x = _proj_residual(u, w_ffd_b, x, mod, 5, layer, rpm)
        return x, kv, st

    xp = x_prompt.reshape(m_ctx, d)
    ks, vs, ss = [], [], []
    for layer in range(depth):
        xp, (k_l, v_l), s_l = trunk_layer(xp, mod_all[layer, :1], m_ctx, layer, b_c, l_c, None, None, h0_ctx, True)
        ks.append(k_l.reshape(b_c, l_c, n_heads, d_v))
        vs.append(v_l.reshape(b_c, l_c, n_heads, d_v))
        ss.append(s_l)

    xs = x_sample.reshape(m_s, d)
    for layer in range(depth):
        ck = cache_k[:, layer].reshape(b_s, past, w_att)
        cv = cache_v[:, layer].reshape(b_s, past, w_att)
        xs, _, _ = trunk_layer(xs, mod_all[layer, 1:], l_s, layer, b_s, l_s, rope, (ck, cv), state_lru[:, layer], False)

    g_fin = final_norm.reshape(1, d)
    y_prompt = _final_norm(xp, 0, m_ctx, g_fin).reshape(b_c, l_c, d)
    y_sample = _final_norm(xs, 0, m_s, g_fin).reshape(b_s, l_s, d)
    return (y_prompt, y_sample, jnp.stack(ks, axis=1), jnp.stack(vs, axis=1), jnp.stack(ss, axis=1))
```

```python
import functools
import math

import jax
import jax.numpy as jnp
import numpy as np
from jax import lax
from jax.experimental import pallas as pl
from jax.experimental.pallas import tpu as pltpu

F32 = jnp.float32
BF16 = jnp.bfloat16

GRID_W = 64
ROPE_THETA = 10000.0
NORM_EPS = 1e-6
SUBLN_EPS = 1e-5
LRU_C = 8.0
HY_BANDS = 8
LANES = 128
SUBLANES = 8
BF16_ROWS = 16
VMEM_LIMIT_BYTES = 56 * 1024 * 1024


def _pick(n, prefs):
    for p in prefs:
        if n % p == 0:
            return p
    raise ValueError(f"no tile in {prefs} divides {n}")


def _params(sem):
    return pltpu.CompilerParams(dimension_semantics=sem, vmem_limit_bytes=VMEM_LIMIT_BYTES)


def _row_tile(rpm):
    return _pick(rpm, (1024, 512, 256, 128))


def _mod_row(i, tm, rpm):
    return i // (rpm // tm)


def _rms_mod(x, g, sc, sh):
    y = x * lax.rsqrt(jnp.mean(x * x, axis=-1, keepdims=True) + NORM_EPS)
    return y * (g * (1.0 + sc)) + sh


def _mod_kernel(c_ref, w_ref, b_ref, o_ref):
    c = c_ref[...]
    s = (c * jax.nn.sigmoid(c)).astype(BF16)
    o_ref[...] = jnp.dot(s, w_ref[...].astype(BF16), preferred_element_type=F32) + b_ref[...]


def _modulation(c8, w_mod, b_mod):
    depth, d, n = w_mod.shape
    tn = _pick(n, (1024, 512, 256, 128))
    return pl.pallas_call(
        _mod_kernel,
        grid=(depth, n // tn),
        in_specs=[
            pl.BlockSpec((SUBLANES, d), lambda l, j: (0, 0)),
            pl.BlockSpec((None, d, tn), lambda l, j: (l, 0, j)),
            pl.BlockSpec((None, 1, tn), lambda l, j: (l, 0, j)),
        ],
        out_specs=pl.BlockSpec((None, SUBLANES, tn), lambda l, j: (l, 0, j)),
        out_shape=jax.ShapeDtypeStruct((depth, SUBLANES, n), F32),
        compiler_params=_params(("arbitrary", "arbitrary")),
        name="modulation",
    )(c8, w_mod, b_mod.reshape(depth, 1, n))


PROLOGUE_ROWS = 128


def _norm_tile(x_ref, mod_ref, g_ref, h_ref, sc_row, sh_row):
    tm = x_ref.shape[0]
    ch = math.gcd(tm, PROLOGUE_ROWS)

    def body(c, carry):
        r0 = pl.multiple_of(c * ch, ch)
        h = _rms_mod(x_ref[pl.ds(r0, ch), :], g_ref[...], mod_ref[sc_row:sc_row + 1, :], mod_ref[sh_row:sh_row + 1, :])
        h_ref[pl.ds(r0, ch), :] = h.astype(h_ref.dtype)
        return carry

    lax.fori_loop(0, tm // ch, body, 0)


def _norm_proj_kernel(x_ref, mod_ref, g_ref, w_ref, o_ref, h_ref):
    @pl.when(pl.program_id(1) == 0)
    def _():
        _norm_tile(x_ref, mod_ref, g_ref, h_ref, 1, 0)

    o_ref[...] = jnp.dot(h_ref[...], w_ref[...], preferred_element_type=F32)


def _norm_proj(x, mod, g, w, layer, rpm):
    m, d = x.shape
    n = w.shape[-1]
    tm = _row_tile(rpm)
    tn = _pick(n, (1024, 512, 256, 128))
    return pl.pallas_call(
        _norm_proj_kernel,
        grid=(m // tm, n // tn),
        in_specs=[
            pl.BlockSpec((tm, d), lambda i, j: (i, 0)),
            pl.BlockSpec((None, 6, d), lambda i, j: (_mod_row(i, tm, rpm), 0, 0)),
            pl.BlockSpec((None, 1, d), lambda i, j: (layer, 0, 0)),
            pl.BlockSpec((None, d, tn), lambda i, j: (layer, 0, j)),
        ],
        out_specs=[pl.BlockSpec((tm, tn), lambda i, j: (i, j)), pl.BlockSpec((tm, d), lambda i, j: (i, 0))],
        out_shape=[jax.ShapeDtypeStruct((m, n), F32), jax.ShapeDtypeStruct((m, d), BF16)],
        compiler_params=_params(("arbitrary", "arbitrary")),
        name="norm_proj",
    )(x, mod, g, w)


def _rope(x, cos, sin_signed):
    n = x.shape[-1]
    half = 16
    lane = lax.broadcasted_iota(jnp.int32, x.shape, 1)
    swapped = jnp.where((lane % (2 * half)) < half, pltpu.roll(x, n - half, 1), pltpu.roll(x, half, 1))
    return x * cos + swapped * sin_signed


def _attn_kernel(*refs, lam_init, d_qk, n_hb, use_rope, use_cache, emit_kv):
    it = iter(refs)
    lam_ref, sub_ref, q_ref, k_ref, v_ref = next(it), next(it), next(it), next(it), next(it)
    if use_rope:
        cq_ref, sq_ref, ck_ref, sk_ref = next(it), next(it), next(it), next(it)
    if use_cache:
        pk_ref, pv_ref = next(it), next(it)
    o_ref = next(it)
    if emit_kv:
        ko_ref, vo_ref = next(it), next(it)
    k_scr = next(it)
    d_v = 2 * d_qk
    heads = [slice(hh * d_v, (hh + 1) * d_v) for hh in range(n_hb)]

    @pl.when(pl.program_id(2) == 0)
    def _():
        if emit_kv:
            ko_ref[...] = k_ref[...]
            vo_ref[...] = v_ref[...]
        for sl in heads:
            k = k_ref[:, sl]
            if use_rope:
                k = _rope(k, ck_ref[...], sk_ref[...])
            k_scr[:, sl] = k.astype(BF16)

    a = lam_ref[...]
    lam = (jnp.exp(jnp.sum(a[0:1] * a[1:2], axis=-1, keepdims=True))
           - jnp.exp(jnp.sum(a[2:3] * a[3:4], axis=-1, keepdims=True)) + lam_init)
    scale = d_qk ** -0.5 * math.log2(math.e)
    nt = (((1,), (1,)), ((), ()))

    for sl in heads:
        q = q_ref[:, sl]
        if use_rope:
            q = _rope(q, cq_ref[...], sq_ref[...])
        q = q * scale
        lane = lax.broadcasted_iota(jnp.int32, q.shape, 1)
        q1 = jnp.where(lane < d_qk, q, 0.0).astype(BF16)
        q2 = jnp.where(lane >= d_qk, q, 0.0).astype(BF16)
        keys = [k_scr[:, sl]]
        vals = [v_ref[:, sl].astype(BF16)]
        if use_cache:
            keys.append(pk_ref[:, sl].astype(BF16))
            vals.append(pv_ref[:, sl].astype(BF16))

        def softmax_v(qh):
            s = [lax.dot_general(qh, kk, nt, preferred_element_type=F32) for kk in keys]
            mx = functools.reduce(jnp.maximum, [jnp.max(x, axis=-1, keepdims=True) for x in s])
            p = [jnp.exp2(x - mx) for x in s]
            den = functools.reduce(jnp.add, [jnp.sum(x, axis=-1, keepdims=True) for x in p])
            pv = functools.reduce(jnp.add, [jnp.dot(x.astype(BF16), vv, preferred_element_type=F32)
                                            for x, vv in zip(p, vals)])
            return pv, 1.0 / den

        o1, r1 = softmax_v(q1)
        o2, r2 = softmax_v(q2)
        o = o1 * r1 - o2 * (lam * r2)
        y = o * lax.rsqrt(jnp.mean(o * o, axis=-1, keepdims=True) + SUBLN_EPS)
        o_ref[:, sl] = ((y * sub_ref[...]) * (1.0 - lam_init)).astype(o_ref.dtype)


def _attention(proj, b, l, h, d_v, lam_init, att_lambda, att_subln, layer, rope=None, cache=None, emit_kv=False):
    assert d_v == LANES
    tq = _pick(l, (256, 128, 64))
    nq = l // tq
    n_hb = max(n for n in range(1, h + 1) if h % n == 0 and n * l <= max(l, 2048))
    wb = n_hb * d_v
    hg = h // n_hb
    in_specs = [
        pl.BlockSpec((None, 4, att_lambda.shape[-1]), lambda bi, hi, qi: (layer, 0, 0)),
        pl.BlockSpec((None, 1, d_v), lambda bi, hi, qi: (layer, 0, 0)),
        pl.BlockSpec((tq, wb), lambda bi, hi, qi: (bi * nq + qi, hi)),
        pl.BlockSpec((l, wb), lambda bi, hi, qi: (bi, hg + hi)),
        pl.BlockSpec((l, wb), lambda bi, hi, qi: (bi, 2 * hg + hi)),
    ]
    args = [att_lambda, att_subln, proj, proj, proj]
    if rope is not None:
        cos, sin = rope
        in_specs += [
            pl.BlockSpec((tq, d_v), lambda bi, hi, qi: (qi, 0)),
            pl.BlockSpec((tq, d_v), lambda bi, hi, qi: (qi, 0)),
            pl.BlockSpec((l, d_v), lambda bi, hi, qi: (0, 0)),
            pl.BlockSpec((l, d_v), lambda bi, hi, qi: (0, 0)),
        ]
        args += [cos, sin, cos, sin]
    if cache is not None:
        ck, cv = cache
        past = ck.shape[1]
        in_specs += [
            pl.BlockSpec((None, past, wb), lambda bi, hi, qi: (bi, 0, hi)),
            pl.BlockSpec((None, past, wb), lambda bi, hi, qi: (bi, 0, hi)),
        ]
        args += [ck, cv]
    out_specs = [pl.BlockSpec((tq, wb), lambda bi, hi, qi: (bi * nq + qi, hi))]
    out_shape = [jax.ShapeDtypeStruct((b * l, h * d_v), BF16)]
    if emit_kv:
        out_specs += [pl.BlockSpec((l, wb), lambda bi, hi, qi: (bi, hi))] * 2
        out_shape += [jax.ShapeDtypeStruct((b * l, h * d_v), F32)] * 2
    return pl.pallas_call(
        functools.partial(_attn_kernel, lam_init=lam_init, d_qk=d_v // 2, n_hb=n_hb,
                          use_rope=rope is not None, use_cache=cache is not None, emit_kv=emit_kv),
        grid=(b, hg, nq),
        in_specs=in_specs,
        out_specs=out_specs,
        out_shape=out_shape,
        scratch_shapes=[pltpu.VMEM((l, wb), BF16)],
        compiler_params=_params(("arbitrary", "arbitrary", "arbitrary")),
        name="diff_attention",
    )(*args)


def _rope_tables(l, d_qk):
    n_freq = d_qk // 4
    lane = jnp.arange(2 * d_qk)
    axis = (lane % d_qk) // (2 * n_freq)
    freq = lane % n_freq
    first = (lane % (2 * n_freq)) < n_freq
    inv_freq = ROPE_THETA ** (-freq.astype(F32) / n_freq)
    t = jnp.arange(l)
    pos = jnp.where(axis[None, :] == 0, (t // GRID_W)[:, None], (t % GRID_W)[:, None]).astype(F32)
    ang = pos * inv_freq[None, :]
    return jnp.cos(ang), jnp.where(first[None, :], -jnp.sin(ang), jnp.sin(ang))


def _shift_rows(x, d):
    n = x.shape[0]
    if d == 0:
        return x
    row = lax.broadcasted_iota(jnp.int32, x.shape, 0)
    rolled = pltpu.roll(x, (-d) % n, 0)
    ok = (row + d >= 0) & (row + d < n)
    return jnp.where(ok, rolled, 0.0)


def _dwconv(x, w, bias, pad_left):
    y = bias + w[0:1] * _shift_rows(x, -pad_left)
    for k in range(1, w.shape[0]):
        y = y + w[k:k + 1] * _shift_rows(x, k - pad_left)
    return y


SCAN_UNROLL = 8


def _lru_kernel(xl_ref, gl_ref, cw_ref, cb_ref, wg_ref, bg_ref, lam_ref, h0_ref, o_ref, st_ref,
                a_scr, b_scr, hf_scr, *, pad_left):
    l, cb = xl_ref.shape
    n_grp = cb // LANES
    xc = _dwconv(xl_ref[...], cw_ref[...], cb_ref[...], pad_left)
    lam = lam_ref[...]
    neg = -lam
    softplus = jnp.maximum(neg, 0.0) + jnp.log1p(jnp.exp(-jnp.abs(neg)))
    row8 = lax.broadcasted_iota(jnp.int32, (SUBLANES, cb), 0)
    n_chunks = l // SUBLANES

    def fill(direction):
        for g in range(n_grp):
            sl = slice(g * LANES, (g + 1) * LANES)
            xg = xc[:, sl]
            y = jnp.dot(xg.astype(BF16), wg_ref[g, :, 2 * direction * LANES:(2 * direction + 2) * LANES],
                        preferred_element_type=F32)
            t_r = jnp.tanh(y[:, :LANES] + bg_ref[2 * direction:2 * direction + 1, sl])
            t_i = jnp.tanh(y[:, LANES:] + bg_ref[2 * direction + 1:2 * direction + 2, sl])
            i = 0.5 * t_i + 0.5
            c = (-0.5 * LRU_C) * softplus[direction:direction + 1, sl]
            log_a = c * t_r + c
            a = jnp.exp(log_a)
            a_scr[:, sl] = a
            b_scr[:, sl] = jnp.sqrt((1.0 - a) * (1.0 + a)) * (i * xg)

    fill(0)

    def fwd(c, h):
        r0 = pl.multiple_of(c * SUBLANES, SUBLANES)
        a = a_scr[pl.ds(r0, SUBLANES), :]
        b = b_scr[pl.ds(r0, SUBLANES), :]
        for s in (1, 2, 4):
            keep = row8 >= s
            b = jnp.where(keep, a * pltpu.roll(b, s, 0) + b, b)
            a = jnp.where(keep, a * pltpu.roll(a, s, 0), a)
        h8 = a * h + b
        hf_scr[pl.ds(r0, SUBLANES), :] = h8
        return h8[SUBLANES - 1:SUBLANES, :]

    s_f = lax.fori_loop(0, n_chunks, fwd, h0_ref[0:1, :], unroll=SCAN_UNROLL)

    fill(1)

    def bwd(c, h):
        r0 = pl.multiple_of((n_chunks - 1 - c) * SUBLANES, SUBLANES)
        a = a_scr[pl.ds(r0, SUBLANES), :]
        b = b_scr[pl.ds(r0, SUBLANES), :]
        for s in (1, 2, 4):
            keep = row8 < SUBLANES - s
            b = jnp.where(keep, a * pltpu.roll(b, SUBLANES - s, 0) + b, b)
            a = jnp.where(keep, a * pltpu.roll(a, SUBLANES - s, 0), a)
        h8 = a * h + b
        b_scr[pl.ds(r0, SUBLANES), :] = h8
        return h8[0:1, :]

    s_b = lax.fori_loop(0, n_chunks, bwd, h0_ref[1:2, :], unroll=SCAN_UNROLL)

    st_ref[0:1, :] = s_f
    st_ref[1:2, :] = s_b
    o_ref[...] = ((hf_scr[...] + b_scr[...]) * jax.nn.gelu(gl_ref[...])).astype(o_ref.dtype)


def _lru(proj, tok0, b, l, col_x, col_g, w_lru, conv_w, conv_b, wg, bg, lam, h0, layer):
    cb = _pick(w_lru, tuple(c for c in (1024, 512, 256, 128) if c == 128 or l * c <= 512 * 1024))
    assert tok0 % l == 0 and col_x % cb == 0 and col_g % cb == 0
    k = conv_w.shape[1]
    n_grp = cb // LANES
    out, st = pl.pallas_call(
        functools.partial(_lru_kernel, pad_left=k // 2),
        grid=(b, w_lru // cb),
        in_specs=[
            pl.BlockSpec((l, cb), lambda bi, j: (tok0 // l + bi, col_x // cb + j)),
            pl.BlockSpec((l, cb), lambda bi, j: (tok0 // l + bi, col_g // cb + j)),
            pl.BlockSpec((None, k, cb), lambda bi, j: (layer, 0, j)),
            pl.BlockSpec((None, 1, cb), lambda bi, j: (layer, 0, j)),
            pl.BlockSpec((None, n_grp, LANES, 4 * LANES), lambda bi, j: (layer, j, 0, 0)),
            pl.BlockSpec((None, 4, cb), lambda bi, j: (layer, 0, j)),
            pl.BlockSpec((None, 2, cb), lambda bi, j: (layer, 0, j)),
            pl.BlockSpec((None, 2, cb), lambda bi, j: (bi, 0, j)),
        ],
        out_specs=[
            pl.BlockSpec((l, cb), lambda bi, j: (bi, j)),
            pl.BlockSpec((None, 2, cb), lambda bi, j: (bi, 0, j)),
        ],
        out_shape=[jax.ShapeDtypeStruct((b * l, w_lru), BF16), jax.ShapeDtypeStruct((b, 2, w_lru), F32)],
        scratch_shapes=[pltpu.VMEM((l, cb), F32)] * 3,
        compiler_params=_params(("arbitrary", "arbitrary")),
        name="rglru",
    )(proj, proj, conv_w, conv_b, wg, bg, lam, h0)
    return out, st


def _lru_gate_weights(wa, wx):
    depth, _, nb, bs, _ = wa.shape
    per = LANES // bs
    eye = jnp.eye(per, dtype=wa.dtype)

    def bd(w):
        w = w.reshape(depth, nb // per, per, bs, bs)
        return jnp.einsum("lgpde,pq->lgpdqe", w, eye).reshape(depth, nb // per, LANES, LANES)

    return (0.5 * jnp.concatenate([bd(wa[:, 0]), bd(wx[:, 0]), bd(wa[:, 1]), bd(wx[:, 1])], axis=-1)).astype(BF16)


def _hy_conv_kernel(x_ref, w_ref, b_ref, o_ref, *, pad_left):
    o_ref[...] = _dwconv(x_ref[...], w_ref[...], b_ref[...], pad_left)


def _hy_conv(proj, tok0, b, l, col0, width, conv_w, conv_b, layer):
    cb = _pick(width, (512, 256, 128))
    assert tok0 % l == 0 and col0 % cb == 0
    k = conv_w.shape[1]
    return pl.pallas_call(
        functools.partial(_hy_conv_kernel, pad_left=k // 2),
        grid=(b, width // cb),
        in_specs=[
            pl.BlockSpec((l, cb), lambda bi, j: (tok0 // l + bi, col0 // cb + j)),
            pl.BlockSpec((None, k, cb), lambda bi, j: (layer, 0, j)),
            pl.BlockSpec((None, 1, cb), lambda bi, j: (layer, 0, j)),
        ],
        out_specs=pl.BlockSpec((l, cb), lambda bi, j: (bi, j)),
        out_shape=jax.ShapeDtypeStruct((b * l, width), F32),
        compiler_params=_params(("arbitrary", "arbitrary")),
        name="hyena_dwconv",
    )(proj, conv_w, conv_b)


def _dot_split(a, b):
    a_hi, b_hi = a.astype(BF16), b.astype(BF16)
    a_lo, b_lo = (a - a_hi.astype(F32)).astype(BF16), (b - b_hi.astype(F32)).astype(BF16)
    dot = functools.partial(jnp.dot, preferred_element_type=F32)
    return dot(a_hi, b_hi) + (dot(a_hi, b_lo) + dot(a_lo, b_hi))


def _hy_filter_kernel(z_ref, w1_ref, b1_ref, fr_ref, w2_ref, b2_ref, w3_ref, b3_ref, dec_ref, o_ref, h_scr):
    @pl.when(pl.program_id(0) == 0)
    def _():
        fr = fr_ref[...]
        h = jnp.sin(fr * (_dot_split(z_ref[...], w1_ref[...]) + b1_ref[...]))
        h_scr[...] = jnp.sin(fr * (_dot_split(h, w2_ref[...]) + b2_ref[...]))

    filt = _dot_split(h_scr[...], w3_ref[...]) + b3_ref[...]
    o_ref[...] = filt * jnp.exp(-z_ref[:, 0:1] * jnp.abs(dec_ref[...]))


def _pad2(x, rows, cols):
    return jnp.pad(x, ((0, rows - x.shape[0]), (0, cols - x.shape[1])))


def _hy_filters(l, w1, b1, freq, w2, b2, w3, b3, decay):
    n = w3.shape[-1]
    t = jnp.linspace(0.0, 1.0, l, dtype=F32)[:, None]
    w = 2.0 * math.pi * jnp.arange(l, dtype=F32)[:, None] / l
    f = jnp.linspace(1e-4, HY_BANDS - 1, HY_BANDS, dtype=F32)[None]
    z = _pad2(jnp.concatenate([t, jnp.cos(f * w), -jnp.sin(f * w)], axis=-1), l, LANES)
    tn = _pick(n, (512, 256, 128))
    row = lambda v: _pad2(v.reshape(1, -1), 1, LANES)
    full = lambda shape: pl.BlockSpec(shape, lambda j: (0, 0))
    return pl.pallas_call(
        _hy_filter_kernel,
        grid=(n // tn,),
        in_specs=[full((l, LANES)), full((LANES, LANES)), full((1, LANES)), full((1, LANES)),
                  full((LANES, LANES)), full((1, LANES)),
                  pl.BlockSpec((LANES, tn), lambda j: (0, j)),
                  pl.BlockSpec((1, tn), lambda j: (0, j)),
                  pl.BlockSpec((1, tn), lambda j: (0, j))],
        out_specs=pl.BlockSpec((l, tn), lambda j: (0, j)),
        out_shape=jax.ShapeDtypeStruct((l, n), F32),
        scratch_shapes=[pltpu.VMEM((l, LANES), F32)],
        compiler_params=_params(("arbitrary",)),
        name="hyena_filters",
    )(z, _pad2(w1, LANES, LANES), row(b1), row(freq), _pad2(w2, LANES, LANES), row(b2),
      _pad2(w3, LANES, n), b3.reshape(1, n), decay.reshape(1, n))


def _dft_tile(l):
    return _pick(2 * l, (512, 256, 128))


@functools.lru_cache(maxsize=None)
def _dft_matrices(l):
    tile = _dft_tile(l)
    half = tile // 2
    r = np.arange(2 * l)
    k = (r // tile) * half + (r % half)
    is_im = (r % tile) >= half
    t = np.arange(l)
    ph = ((2 * k + 1)[:, None] * t[None, :]) % (4 * l)
    ang = ph.astype(np.float64) * (math.pi / (2 * l))
    return np.where(is_im[:, None], -np.sin(ang), np.cos(ang)).astype(np.float32)


def _cmul_store(o_ref, s, f_re, f_im):
    half = s.shape[0] // 2
    s_re, s_im = s[:half], s[half:]
    o_ref[:half, :] = (s_re * f_re - s_im * f_im).astype(o_ref.dtype)
    o_ref[half:, :] = (s_re * f_im + s_im * f_re).astype(o_ref.dtype)


def _filter_spec_kernel(a_ref, hf_ref, hb_ref, o_ref, hf_scr, hb_scr):
    @pl.when(pl.program_id(2) == 0)
    def _():
        hf_scr[...] = hf_ref[...].astype(BF16)
        hb = hb_ref[...]
        row = lax.broadcasted_iota(jnp.int32, hb.shape, 0)
        hb_scr[...] = jnp.where(row == 0, 0.0, hb).astype(BF16)

    a = a_ref[...]
    sf = jnp.dot(a, hf_scr[...], preferred_element_type=F32)
    sb = jnp.dot(a, hb_scr[...], preferred_element_type=F32)
    half = a.shape[0] // 2
    o_ref[:half, :] = sf[:half] + sb[:half]
    o_ref[half:, :] = sf[half:] - sb[half:]


def _filter_spectrum(a_mat, filt, n_order, w_hy):
    n2, l = a_mat.shape
    tm = _dft_tile(l)
    tn = _pick(w_hy, (512, 256, 128))
    nj = w_hy // tn
    return pl.pallas_call(
        _filter_spec_kernel,
        grid=(n_order, nj, n2 // tm),
        in_specs=[
            pl.BlockSpec((tm, l), lambda o, j, i: (i, 0)),
            pl.BlockSpec((l, tn), lambda o, j, i: (0, (2 * o) * nj + j)),
            pl.BlockSpec((l, tn), lambda o, j, i: (0, (2 * o + 1) * nj + j)),
        ],
        out_specs=pl.BlockSpec((None, tm, tn), lambda o, j, i: (o, i, j)),
        out_shape=jax.ShapeDtypeStruct((n_order, n2, w_hy), F32),
        scratch_shapes=[pltpu.VMEM((l, tn), BF16)] * 2,
        compiler_params=_params(("arbitrary", "arbitrary", "arbitrary")),
        name="hyena_filter_spectrum",
    )(a_mat, filt, filt)


def _dft_fwd_kernel(a_ref, z_ref, f_ref, o_ref, z_scr):
    @pl.when(pl.program_id(2) == 0)
    def _():
        z_scr[...] = z_ref[...].astype(BF16)

    s = jnp.dot(a_ref[...], z_scr[...], preferred_element_type=F32)
    half = s.shape[0] // 2
    _cmul_store(o_ref, s, f_ref[:half, :], f_ref[half:, :])


def _dft_fwd(a_mat, z, z_col0, fspec, order, b, l, w_hy):
    n2 = a_mat.shape[0]
    tm = _dft_tile(l)
    tn = _pick(w_hy, (1024, 512, 256, 128))
    assert z_col0 % tn == 0
    return pl.pallas_call(
        _dft_fwd_kernel,
        grid=(b, w_hy // tn, n2 // tm),
        in_specs=[
            pl.BlockSpec((tm, l), lambda bi, j, i: (i, 0)),
            pl.BlockSpec((l, tn), lambda bi, j, i: (bi, z_col0 // tn + j)),
            pl.BlockSpec((None, tm, tn), lambda bi, j, i: (order, i, j)),
        ],
        out_specs=pl.BlockSpec((None, tm, tn), lambda bi, j, i: (bi, i, j)),
        out_shape=jax.ShapeDtypeStruct((b, n2, w_hy), BF16),
        scratch_shapes=[pltpu.VMEM((l, tn), BF16)],
        compiler_params=_params(("arbitrary", "arbitrary", "arbitrary")),
        name="hyena_dft_fwd",
    )(a_mat, z, fspec)


def _dft_inv_kernel(at_ref, y_ref, z_ref, g_ref, d_ref, o_ref, *, inv_l):
    y = jnp.dot(at_ref[...], y_ref[...], preferred_element_type=F32) * inv_l
    z = z_ref[...]
    o_ref[...] = (g_ref[...] * (y + d_ref[...] * z)).astype(o_ref.dtype)


def _dft_inv(at_mat, y, z, z_col0, gate, gate_col0, d, layer, order, b, l, w_hy, out_dtype):
    n2 = at_mat.shape[1]
    tm = _pick(l, (512, 256, 128, 64))
    tn = _pick(w_hy, (1024, 512, 256, 128))
    nt = l // tm
    assert z_col0 % tn == 0 and gate_col0 % tn == 0
    return pl.pallas_call(
        functools.partial(_dft_inv_kernel, inv_l=1.0 / l),
        grid=(b, w_hy // tn, nt),
        in_specs=[
            pl.BlockSpec((tm, n2), lambda bi, j, i: (i, 0)),
            pl.BlockSpec((None, n2, tn), lambda bi, j, i: (bi, 0, j)),
            pl.BlockSpec((tm, tn), lambda bi, j, i: (bi * nt + i, z_col0 // tn + j)),
            pl.BlockSpec((tm, tn), lambda bi, j, i: (bi * nt + i, gate_col0 // tn + j)),
            pl.BlockSpec((None, None, 1, tn), lambda bi, j, i: (layer, order, 0, j)),
        ],
        out_specs=pl.BlockSpec((tm, tn), lambda bi, j, i: (bi * nt + i, j)),
        out_shape=jax.ShapeDtypeStruct((b * l, w_hy), out_dtype),
        compiler_params=_params(("arbitrary", "arbitrary", "arbitrary")),
        name="hyena_dft_inv",
    )(at_mat, y, z, gate, d)


def _hyena_fused_kernel(*refs, n_order, pad_left, tile, inv_l):
    it = iter(refs)
    a_ref, at_ref, f_ref, d_ref = next(it), next(it), next(it), next(it)
    x_refs = [next(it) for _ in range(n_order + 1)]
    w_refs = [next(it) for _ in range(n_order + 1)]
    b_refs = [next(it) for _ in range(n_order + 1)]
    o_ref = next(it)
    half = tile // 2
    n_tiles = a_ref.shape[0] // tile
    conv = lambda n: _dwconv(x_refs[n][...], w_refs[n][...], b_refs[n][...], pad_left)
    z = conv(0)
    for o in range(n_order):
        zb = z.astype(BF16)
        y = None
        for ti in range(n_tiles):
            r0 = ti * tile
            s = jnp.dot(a_ref[r0:r0 + tile, :], zb, preferred_element_type=F32)
            s_re, s_im = s[:half], s[half:]
            f_re, f_im = f_ref[o, r0:r0 + half, :], f_ref[o, r0 + half:r0 + tile, :]
            spec = jnp.concatenate([s_re * f_re - s_im * f_im, s_re * f_im + s_im * f_re], axis=0).astype(BF16)
            t = jnp.dot(at_ref[:, r0:r0 + tile], spec, preferred_element_type=F32)
            y = t if y is None else y + t
        z = conv(o + 1) * (y * inv_l + d_ref[o] * z)
    o_ref[...] = z.astype(o_ref.dtype)


def _hyena_fused(proj, tok0, b, l, col0, w_hy, n_order, a_mat, at_mat, fspec, conv_w, conv_b, hy_d, layer):
    n2 = a_mat.shape[0]
    tn = _pick(w_hy, (1024, 512, 256, 128))
    nj = w_hy // tn
    k = conv_w.shape[1]
    assert tok0 % l == 0 and col0 % tn == 0
    col = lambda n: (col0 + n * w_hy) // tn
    x_specs = [pl.BlockSpec((l, tn), lambda j, bi, n=n: (tok0 // l + bi, col(n) + j)) for n in range(n_order + 1)]
    w_specs = [pl.BlockSpec((None, k, tn), lambda j, bi, n=n: (layer, 0, n * nj + j)) for n in range(n_order + 1)]
    b_specs = [pl.BlockSpec((None, 1, tn), lambda j, bi, n=n: (layer, 0, n * nj + j)) for n in range(n_order + 1)]
    return pl.pallas_call(
        functools.partial(_hyena_fused_kernel, n_order=n_order, pad_left=k // 2, tile=_dft_tile(l), inv_l=1.0 / l),
        grid=(nj, b),
        in_specs=[
            pl.BlockSpec((n2, l), lambda j, bi: (0, 0)),
            pl.BlockSpec((l, n2), lambda j, bi: (0, 0)),
            pl.BlockSpec((n_order, n2, tn), lambda j, bi: (0, 0, j)),
            pl.BlockSpec((None, n_order, 1, tn), lambda j, bi: (layer, 0, 0, j)),
            *x_specs, *w_specs, *b_specs,
        ],
        out_specs=pl.BlockSpec((l, tn), lambda j, bi: (bi, j)),
        out_shape=jax.ShapeDtypeStruct((b * l, w_hy), BF16),
        compiler_params=_params(("arbitrary", "arbitrary")),
        name="hyena_fused",
    )(a_mat, at_mat, fspec, hy_d, *([proj] * (n_order + 1)), *([conv_w] * (n_order + 1)),
      *([conv_b] * (n_order + 1)))


HYENA_FUSED_MAX_L = 512


def _hyena(proj, tok0, b, l, col0, w_hy, n_order, a_mat, at_mat, fspec, conv_w, conv_b, hy_d, layer):
    if l <= HYENA_FUSED_MAX_L:
        return _hyena_fused(proj, tok0, b, l, col0, w_hy, n_order, a_mat, at_mat, fspec, conv_w, conv_b, hy_d, layer)
    hyc = _hy_conv(proj, tok0, b, l, col0, (n_order + 1) * w_hy, conv_w, conv_b, layer)
    z, z_col0 = hyc, 0
    for o in range(n_order):
        y = _dft_fwd(a_mat, z, z_col0, fspec, o, b, l, w_hy)
        last = o == n_order - 1
        z = _dft_inv(at_mat, y, z, z_col0, hyc, (o + 1) * w_hy, hy_d, layer, o, b, l, w_hy, BF16 if last else F32)
        z_col0 = 0
    return z


def _merge_kernel(h_ref, wg0_ref, wg1_ref, wg2_ref, b0_ref, b1_ref, b2_ref, wb0_ref, wb1_ref, wb2_ref, o_ref):
    h = h_ref[...]
    acc = None
    for wg_ref, b_ref, wb_ref in ((wg0_ref, b0_ref, wb0_ref), (wg1_ref, b1_ref, wb1_ref), (wg2_ref, b2_ref, wb2_ref)):
        gate = jax.nn.sigmoid(jnp.dot(h, wg_ref[...], preferred_element_type=F32))
        t = gate * jnp.dot(b_ref[...], wb_ref[...], preferred_element_type=F32)
        acc = t if acc is None else acc + t
    o_ref[...] = acc.astype(o_ref.dtype)


def _merge(h, w_gate, branches, w_br, layer):
    m, d = h.shape
    wb = w_br.shape[2]
    tm = _pick(m, (1024, 512, 256, 128))
    tn = _pick(d, (512, 256, 128))
    nj = d // tn
    gate_spec = lambda n: pl.BlockSpec((None, d, tn), lambda i, j: (layer, 0, n * nj + j))
    br_spec = pl.BlockSpec((tm, wb), lambda i, j: (i, 0))
    wbr_spec = lambda n: pl.BlockSpec((None, None, wb, tn), lambda i, j: (layer, n, 0, j))
    return pl.pallas_call(
        _merge_kernel,
        grid=(m // tm, nj),
        in_specs=[
            pl.BlockSpec((tm, d), lambda i, j: (i, 0)),
            gate_spec(0), gate_spec(1), gate_spec(2),
            br_spec, br_spec, br_spec,
            wbr_spec(0), wbr_spec(1), wbr_spec(2),
        ],
        out_specs=pl.BlockSpec((tm, tn), lambda i, j: (i, j)),
        out_shape=jax.ShapeDtypeStruct((m, d), BF16),
        compiler_params=_params(("arbitrary", "arbitrary")),
        name="gated_merge",
    )(h, w_gate, w_gate, w_gate, *branches, w_br, w_br, w_br)


def _proj_residual_kernel(a_ref, w_ref, x_ref, mod_ref, o_ref, *, gate_row):
    gate = mod_ref[gate_row:gate_row + 1, :]
    o_ref[...] = x_ref[...] + gate * jnp.dot(a_ref[...], w_ref[...], preferred_element_type=F32)


def _proj_residual(a, w, x, mod, gate_row, layer, rpm):
    m, k = a.shape
    d = x.shape[1]
    tm = _row_tile(rpm)
    tn = _pick(d, (1024, 512, 256, 128) if k <= 2048 else (512, 256, 128))
    return pl.pallas_call(
        functools.partial(_proj_residual_kernel, gate_row=gate_row),
        grid=(m // tm, d // tn),
        in_specs=[
            pl.BlockSpec((tm, k), lambda i, j: (i, 0)),
            pl.BlockSpec((None, k, tn), lambda i, j: (layer, 0, j)),
            pl.BlockSpec((tm, tn), lambda i, j: (i, j)),
            pl.BlockSpec((None, 6, tn), lambda i, j: (_mod_row(i, tm, rpm), 0, j)),
        ],
        out_specs=pl.BlockSpec((tm, tn), lambda i, j: (i, j)),
        out_shape=jax.ShapeDtypeStruct((m, d), F32),
        compiler_params=_params(("arbitrary", "arbitrary")),
        name="proj_residual",
    )(a, w, x, mod)


def _ffn_up_kernel(x_ref, mod_ref, g_ref, wg_ref, wu_ref, o_ref, h_scr):
    @pl.when(pl.program_id(1) == 0)
    def _():
        _norm_tile(x_ref, mod_ref, g_ref, h_scr, 4, 3)

    h = h_scr[...]
    a = jnp.dot(h, wg_ref[...], preferred_element_type=F32)
    u = jnp.dot(h, wu_ref[...], preferred_element_type=F32)
    o_ref[...] = ((a * jax.nn.sigmoid(a)) * u).astype(o_ref.dtype)


def _ffn_up(x, mod, g, w_gate, w_up, layer, rpm):
    m, d = x.shape
    n = w_gate.shape[-1]
    tm = _row_tile(rpm)
    tn = _pick(n, (512, 256, 128))
    w_spec = pl.BlockSpec((None, d, tn), lambda i, j: (layer, 0, j))
    return pl.pallas_call(
        _ffn_up_kernel,
        grid=(m // tm, n // tn),
        in_specs=[
            pl.BlockSpec((tm, d), lambda i, j: (i, 0)),
            pl.BlockSpec((None, 6, d), lambda i, j: (_mod_row(i, tm, rpm), 0, 0)),
            pl.BlockSpec((None, 1, d), lambda i, j: (layer, 0, 0)),
            w_spec, w_spec,
        ],
        out_specs=pl.BlockSpec((tm, tn), lambda i, j: (i, j)),
        out_shape=jax.ShapeDtypeStruct((m, n), BF16),
        scratch_shapes=[pltpu.VMEM((tm, d), BF16)],
        compiler_params=_params(("arbitrary", "arbitrary")),
        name="ffn_up",
    )(x, mod, g, w_gate, w_up)


def _final_norm_kernel(x_ref, g_ref, o_ref):
    x = x_ref[...]
    o_ref[...] = (x * lax.rsqrt(jnp.mean(x * x, axis=-1, keepdims=True) + NORM_EPS)) * g_ref[...]


def _final_norm(x, tok0, n_tok, g):
    d = x.shape[1]
    tm = _pick(math.gcd(tok0, n_tok) if tok0 else n_tok, (512, 256, 128, 64))
    return pl.pallas_call(
        _final_norm_kernel,
        grid=(n_tok // tm,),
        in_specs=[pl.BlockSpec((tm, d), lambda i: (tok0 // tm + i, 0)), pl.BlockSpec((1, d), lambda i: (0, 0))],
        out_specs=pl.BlockSpec((tm, d), lambda i: (i, 0)),
        out_shape=jax.ShapeDtypeStruct((n_tok, d), F32),
        compiler_params=_params(("arbitrary",)),
        name="final_norm",
    )(x, g)


def kernel(x_prompt, x_sample, c, cache_k, cache_v, state_lru, c_ctx, w_mod, b_mod, norm_mix, norm_ffn, w_in, w_gate, att_lambda, att_subln, lru_conv_w, lru_conv_b, lru_wa, lru_ba, lru_wx, lru_bx, lru_lambda, hy_conv_w, hy_conv_b, hy_w1, hy_b1, hy_w2, hy_b2, hy_w3, hy_b3, hy_freq, hy_decay, hy_d, w_br, w_out, w_ff_gate, w_ff_up, w_ff_down, final_norm):
    b_c, l_c, d = x_prompt.shape
    b_s, l_s, _ = x_sample.shape
    depth = w_in.shape[0]
    n_heads, d_v = cache_v.shape[3], cache_v.shape[4]
    w_att = n_heads * d_v
    w_lru = lru_lambda.shape[-1]
    n_order, w_hy = hy_d.shape[1], hy_d.shape[2]
    m_ctx, m_s = b_c * l_c, b_s * l_s
    past = cache_k.shape[2]
    col_xl, col_gl, col_hy = 3 * w_att, 3 * w_att + w_lru, 3 * w_att + 2 * w_lru

    w_in_b, w_gate_b, w_br_b, w_out_b = (w.astype(BF16) for w in (w_in, w_gate, w_br, w_out))
    w_ffg_b, w_ffu_b, w_ffd_b = (w.astype(BF16) for w in (w_ff_gate, w_ff_up, w_ff_down))
    lru_wg = _lru_gate_weights(lru_wa, lru_wx)
    lru_bg = 0.5 * jnp.stack([lru_ba[:, 0], lru_bx[:, 0], lru_ba[:, 1], lru_bx[:, 1]], axis=1)
    norm_mix3, norm_ffn3 = norm_mix.reshape(depth, 1, d), norm_ffn.reshape(depth, 1, d)
    att_subln3 = att_subln.reshape(depth, 1, d_v)
    lru_conv_b3 = lru_conv_b.reshape(depth, 1, w_lru)
    hy_conv_b3 = hy_conv_b.reshape(depth, 1, -1)
    hy_d4 = hy_d.reshape(depth, n_order, 1, w_hy)

    n_rows = 1 + b_s
    assert n_rows <= SUBLANES
    c8 = jnp.zeros((SUBLANES, d), F32).at[0].set(c_ctx).at[1:n_rows].set(c)
    mod_all = _modulation(c8, w_mod, b_mod)[:, :n_rows].reshape(depth, n_rows, 6, d)

    rope = _rope_tables(l_s, d_v // 2)
    dft = {}
    for l in sorted({l_c, l_s}):
        a_mat = jnp.asarray(_dft_matrices(l)).astype(BF16)
        dft[l] = (a_mat, a_mat.T)
    h0_ctx = jnp.zeros((b_c, 2, w_lru), F32)

    def trunk_layer(x, mod, rpm, layer, b, l, rope, cache, h0, want_kv):
        lam_init = 0.8 - 0.6 * math.exp(-0.3 * layer)
        proj, h = _norm_proj(x, mod, norm_mix3, w_in_b, layer, rpm)
        att, *kv = _attention(proj, b, l, n_heads, d_v, lam_init, att_lambda, att_subln3, layer, rope=rope,
                              cache=cache, emit_kv=want_kv)
        lru, st = _lru(proj, 0, b, l, col_xl, col_gl, w_lru, lru_conv_w, lru_conv_b3, lru_wg, lru_bg, lru_lambda,
                       h0, layer)
        a_mat, at_mat = dft[l]
        filt = _hy_filters(l, hy_w1[layer], hy_b1[layer], hy_freq[layer], hy_w2[layer], hy_b2[layer],
                           hy_w3[layer], hy_b3[layer], hy_decay[layer])
        fspec = _filter_spectrum(a_mat, filt, n_order, w_hy)
        hyo = _hyena(proj, 0, b, l, col_hy, w_hy, n_order, a_mat, at_mat, fspec, hy_conv_w, hy_conv_b3, hy_d4, layer)
        mixed = _merge(h, w_gate_b, (att, lru, hyo), w_br_b, layer)
        x = _proj_residual(mixed, w_out_b, x, mod, 2, layer, rpm)
        u = _ffn_up(x, mod, norm_ffn3, w_ffg_b, w_ffu_b, layer, rpm)
        x = _proj_residual(u, w_ffd_b, x, mod, 5, layer, rpm)
        return x, kv, st

    xp = x_prompt.reshape(m_ctx, d)
    ks, vs, ss = [], [], []
    for layer in range(depth):
        xp, (k_l, v_l), s_l = trunk_layer(xp, mod_all[layer, :1], m_ctx, layer, b_c, l_c, None, None, h0_ctx, True)
        ks.append(k_l.reshape(b_c, l_c, n_heads, d_v))
        vs.append(v_l.reshape(b_c, l_c, n_heads, d_v))
        ss.append(s_l)

    xs = x_sample.reshape(m_s, d)
    for layer in range(depth):
        ck = cache_k[:, layer].reshape(b_s, past, w_att)
        cv = cache_v[:, layer].reshape(b_s, past, w_att)
        xs, _, _ = trunk_layer(xs, mod_all[layer, 1:], l_s, layer, b_s, l_s, rope, (ck, cv), state_lru[:, layer], False)

    g_fin = final_norm.reshape(1, d)
    y_prompt = _final_norm(xp, 0, m_ctx, g_fin).reshape(b_c, l_c, d)
    y_sample = _final_norm(xs, 0, m_s, g_fin).reshape(b_s, l_s, d)
    return (y_prompt, y_sample, jnp.stack(ks, axis=1), jnp.stack(vs, axis=1), jnp.stack(ss, axis=1))
```

```python
import functools
import math

import jax
import jax.numpy as jnp
import numpy as np
from jax import lax
from jax.experimental import pallas as pl
from jax.experimental.pallas import tpu as pltpu

F32 = jnp.float32
BF16 = jnp.bfloat16

GRID_W = 64
ROPE_THETA = 10000.0
NORM_EPS = 1e-6
SUBLN_EPS = 1e-5
LRU_C = 8.0
HY_BANDS = 8
LANES = 128
SUBLANES = 8
BF16_ROWS = 16
VMEM_LIMIT_BYTES = 56 * 1024 * 1024


def _pick(n, prefs):
    for p in prefs:
        if n % p == 0:
            return p
    raise ValueError(f"no tile in {prefs} divides {n}")


def _params(sem):
    return pltpu.CompilerParams(dimension_semantics=sem, vmem_limit_bytes=VMEM_LIMIT_BYTES)


def _row_tile(rpm):
    return _pick(rpm, (1024, 512, 256, 128))


def _mod_row(i, tm, rpm):
    return i // (rpm // tm)


def _rms_mod(x, g, sc, sh):
    y = x * lax.rsqrt(jnp.mean(x * x, axis=-1, keepdims=True) + NORM_EPS)
    return y * (g * (1.0 + sc)) + sh


def _mod_kernel(c_ref, w_ref, b_ref, o_ref):
    c = c_ref[...]
    s = (c * jax.nn.sigmoid(c)).astype(BF16)
    o_ref[...] = jnp.dot(s, w_ref[...].astype(BF16), preferred_element_type=F32) + b_ref[...]


def _modulation(c8, w_mod, b_mod):
    depth, d, n = w_mod.shape
    tn = _pick(n, (1024, 512, 256, 128))
    return pl.pallas_call(
        _mod_kernel,
        grid=(depth, n // tn),
        in_specs=[
            pl.BlockSpec((SUBLANES, d), lambda l, j: (0, 0)),
            pl.BlockSpec((None, d, tn), lambda l, j: (l, 0, j)),
            pl.BlockSpec((None, 1, tn), lambda l, j: (l, 0, j)),
        ],
        out_specs=pl.BlockSpec((None, SUBLANES, tn), lambda l, j: (l, 0, j)),
        out_shape=jax.ShapeDtypeStruct((depth, SUBLANES, n), F32),
        compiler_params=_params(("arbitrary", "arbitrary")),
        name="modulation",
    )(c8, w_mod, b_mod.reshape(depth, 1, n))


PROLOGUE_ROWS = 128


def _norm_tile(x_ref, mod_ref, g_ref, h_ref, sc_row, sh_row):
    tm = x_ref.shape[0]
    ch = math.gcd(tm, PROLOGUE_ROWS)

    def body(c, carry):
        r0 = pl.multiple_of(c * ch, ch)
        h = _rms_mod(x_ref[pl.ds(r0, ch), :], g_ref[...], mod_ref[sc_row:sc_row + 1, :], mod_ref[sh_row:sh_row + 1, :])
        h_ref[pl.ds(r0, ch), :] = h.astype(h_ref.dtype)
        return carry

    lax.fori_loop(0, tm // ch, body, 0)


def _norm_proj_kernel(x_ref, mod_ref, g_ref, w_ref, o_ref, h_ref):
    @pl.when(pl.program_id(1) == 0)
    def _():
        _norm_tile(x_ref, mod_ref, g_ref, h_ref, 1, 0)

    o_ref[...] = jnp.dot(h_ref[...], w_ref[...], preferred_element_type=F32)


def _norm_proj(x, mod, g, w, layer, rpm):
    m, d = x.shape
    n = w.shape[-1]
    tm = _row_tile(rpm)
    tn = _pick(n, (1024, 512, 256, 128))
    return pl.pallas_call(
        _norm_proj_kernel,
        grid=(m // tm, n // tn),
        in_specs=[
            pl.BlockSpec((tm, d), lambda i, j: (i, 0)),
            pl.BlockSpec((None, 6, d), lambda i, j: (_mod_row(i, tm, rpm), 0, 0)),
            pl.BlockSpec((None, 1, d), lambda i, j: (layer, 0, 0)),
            pl.BlockSpec((None, d, tn), lambda i, j: (layer, 0, j)),
        ],
        out_specs=[pl.BlockSpec((tm, tn), lambda i, j: (i, j)), pl.BlockSpec((tm, d), lambda i, j: (i, 0))],
        out_shape=[jax.ShapeDtypeStruct((m, n), F32), jax.ShapeDtypeStruct((m, d), BF16)],
        compiler_params=_params(("arbitrary", "arbitrary")),
        name="norm_proj",
    )(x, mod, g, w)


def _rope(x, cos, sin_signed):
    n = x.shape[-1]
    half = 16
    lane = lax.broadcasted_iota(jnp.int32, x.shape, 1)
    swapped = jnp.where((lane % (2 * half)) < half, pltpu.roll(x, n - half, 1), pltpu.roll(x, half, 1))
    return x * cos + swapped * sin_signed


def _attn_kernel(*refs, lam_init, d_qk, n_hb, use_rope, use_cache, emit_kv, n_prev):
    it = iter(refs)
    lam_ref, sub_ref, q_ref, k_ref, v_ref = next(it), next(it), next(it), next(it), next(it)
    if use_rope:
        cq_ref, sq_ref, ck_ref, sk_ref = next(it), next(it), next(it), next(it)
    if use_cache:
        pk_ref, pv_ref = next(it), next(it)
    prev_refs = [(next(it), next(it)) for _ in range(n_prev)]
    o_ref = next(it)
    if emit_kv:
        ko_ref, vo_ref = next(it), next(it)
    k_scr = next(it)
    d_v = 2 * d_qk
    heads = [slice(hh * d_v, (hh + 1) * d_v) for hh in range(n_hb)]

    @pl.when(pl.program_id(2) == 0)
    def _():
        if emit_kv and n_prev:
            for n, (pk, pv) in enumerate(prev_refs):
                ko_ref[n] = pk[...]
                vo_ref[n] = pv[...]
            ko_ref[n_prev] = k_ref[...]
            vo_ref[n_prev] = v_ref[...]
        elif emit_kv:
            ko_ref[...] = k_ref[...]
            vo_ref[...] = v_ref[...]
        for sl in heads:
            k = k_ref[:, sl]
            if use_rope:
                k = _rope(k, ck_ref[...], sk_ref[...])
            k_scr[:, sl] = k.astype(BF16)

    a = lam_ref[...]
    lam = (jnp.exp(jnp.sum(a[0:1] * a[1:2], axis=-1, keepdims=True))
           - jnp.exp(jnp.sum(a[2:3] * a[3:4], axis=-1, keepdims=True)) + lam_init)
    scale = d_qk ** -0.5 * math.log2(math.e)
    nt = (((1,), (1,)), ((), ()))

    for sl in heads:
        q = q_ref[:, sl]
        if use_rope:
            q = _rope(q, cq_ref[...], sq_ref[...])
        q = q * scale
        lane = lax.broadcasted_iota(jnp.int32, q.shape, 1)
        q1 = jnp.where(lane < d_qk, q, 0.0).astype(BF16)
        q2 = jnp.where(lane >= d_qk, q, 0.0).astype(BF16)
        keys = [k_scr[:, sl]]
        vals = [v_ref[:, sl].astype(BF16)]
        if use_cache:
            keys.append(pk_ref[:, sl].astype(BF16))
            vals.append(pv_ref[:, sl].astype(BF16))

        def softmax_v(qh):
            s = [lax.dot_general(qh, kk, nt, preferred_element_type=F32) for kk in keys]
            mx = functools.reduce(jnp.maximum, [jnp.max(x, axis=-1, keepdims=True) for x in s])
            p = [jnp.exp2(x - mx) for x in s]
            den = functools.reduce(jnp.add, [jnp.sum(x, axis=-1, keepdims=True) for x in p])
            pv = functools.reduce(jnp.add, [jnp.dot(x.astype(BF16), vv, preferred_element_type=F32)
                                            for x, vv in zip(p, vals)])
            return pv, 1.0 / den

        o1, r1 = softmax_v(q1)
        o2, r2 = softmax_v(q2)
        o = o1 * r1 - o2 * (lam * r2)
        y = o * lax.rsqrt(jnp.mean(o * o, axis=-1, keepdims=True) + SUBLN_EPS)
        o_ref[:, sl] = ((y * sub_ref[...]) * (1.0 - lam_init)).astype(o_ref.dtype)


def _attention(proj, b, l, h, d_v, lam_init, att_lambda, att_subln, layer, rope=None, cache=None, emit_kv=False,
               prev_kv=()):
    assert d_v == LANES
    tq = _pick(l, (512, 256, 128, 64))
    nq = l // tq
    n_hb = max(n for n in range(1, h + 1) if h % n == 0 and n * l <= max(l, 2048))
    wb = n_hb * d_v
    hg = h // n_hb
    in_specs = [
        pl.BlockSpec((None, 4, att_lambda.shape[-1]), lambda bi, hi, qi: (layer, 0, 0)),
        pl.BlockSpec((None, 1, d_v), lambda bi, hi, qi: (layer, 0, 0)),
        pl.BlockSpec((tq, wb), lambda bi, hi, qi: (bi * nq + qi, hi)),
        pl.BlockSpec((l, wb), lambda bi, hi, qi: (bi, hg + hi)),
        pl.BlockSpec((l, wb), lambda bi, hi, qi: (bi, 2 * hg + hi)),
    ]
    args = [att_lambda, att_subln, proj, proj, proj]
    if rope is not None:
        cos, sin = rope
        in_specs += [
            pl.BlockSpec((tq, d_v), lambda bi, hi, qi: (qi, 0)),
            pl.BlockSpec((tq, d_v), lambda bi, hi, qi: (qi, 0)),
            pl.BlockSpec((l, d_v), lambda bi, hi, qi: (0, 0)),
            pl.BlockSpec((l, d_v), lambda bi, hi, qi: (0, 0)),
        ]
        args += [cos, sin, cos, sin]
    if cache is not None:
        ck, cv = cache
        past = ck.shape[1]
        in_specs += [
            pl.BlockSpec((None, past, wb), lambda bi, hi, qi: (bi, 0, hi)),
            pl.BlockSpec((None, past, wb), lambda bi, hi, qi: (bi, 0, hi)),
        ]
        args += [ck, cv]
    for pk, pv in prev_kv:
        in_specs += [pl.BlockSpec((l, wb), lambda bi, hi, qi: (bi, hi))] * 2
        args += [pk, pv]
    out_specs = [pl.BlockSpec((tq, wb), lambda bi, hi, qi: (bi * nq + qi, hi))]
    out_shape = [jax.ShapeDtypeStruct((b * l, h * d_v), BF16)]
    if emit_kv and prev_kv:
        n_layers = len(prev_kv) + 1
        out_specs += [pl.BlockSpec((None, n_layers, l, wb), lambda bi, hi, qi: (bi, 0, 0, hi))] * 2
        out_shape += [jax.ShapeDtypeStruct((b, n_layers, l, h * d_v), F32)] * 2
    elif emit_kv:
        out_specs += [pl.BlockSpec((l, wb), lambda bi, hi, qi: (bi, hi))] * 2
        out_shape += [jax.ShapeDtypeStruct((b * l, h * d_v), F32)] * 2
    return pl.pallas_call(
        functools.partial(_attn_kernel, lam_init=lam_init, d_qk=d_v // 2, n_hb=n_hb, use_rope=rope is not None,
                          use_cache=cache is not None, emit_kv=emit_kv, n_prev=len(prev_kv)),
        grid=(b, hg, nq),
        in_specs=in_specs,
        out_specs=out_specs,
        out_shape=out_shape,
        scratch_shapes=[pltpu.VMEM((l, wb), BF16)],
        compiler_params=_params(("arbitrary", "arbitrary", "arbitrary")),
        name="diff_attention",
    )(*args)


def _rope_tables(l, d_qk):
    n_freq = d_qk // 4
    lane = jnp.arange(2 * d_qk)
    axis = (lane % d_qk) // (2 * n_freq)
    freq = lane % n_freq
    first = (lane % (2 * n_freq)) < n_freq
    inv_freq = ROPE_THETA ** (-freq.astype(F32) / n_freq)
    t = jnp.arange(l)
    pos = jnp.where(axis[None, :] == 0, (t // GRID_W)[:, None], (t % GRID_W)[:, None]).astype(F32)
    ang = pos * inv_freq[None, :]
    return jnp.cos(ang), jnp.where(first[None, :], -jnp.sin(ang), jnp.sin(ang))


def _shift_rows(x, d):
    n = x.shape[0]
    if d == 0:
        return x
    row = lax.broadcasted_iota(jnp.int32, x.shape, 0)
    rolled = pltpu.roll(x, (-d) % n, 0)
    ok = (row + d >= 0) & (row + d < n)
    return jnp.where(ok, rolled, 0.0)


def _dwconv(x, w, bias, pad_left):
    y = bias + w[0:1] * _shift_rows(x, -pad_left)
    for k in range(1, w.shape[0]):
        y = y + w[k:k + 1] * _shift_rows(x, k - pad_left)
    return y


SCAN_UNROLL = 8


def _lru_kernel(xl_ref, gl_ref, cw_ref, cb_ref, wg_ref, bg_ref, lam_ref, h0_ref, o_ref, st_ref,
                a_scr, b_scr, hf_scr, hb_scr, *, pad_left):
    l, cb = xl_ref.shape
    n_grp = cb // LANES
    xc = _dwconv(xl_ref[...], cw_ref[...], cb_ref[...], pad_left)
    lam = lam_ref[...]
    neg = -lam
    softplus = jnp.maximum(neg, 0.0) + jnp.log1p(jnp.exp(-jnp.abs(neg)))
    row8 = lax.broadcasted_iota(jnp.int32, (SUBLANES, cb), 0)
    n_chunks = l // SUBLANES

    def fill(direction):
        for g in range(n_grp):
            sl = slice(g * LANES, (g + 1) * LANES)
            xg = xc[:, sl]
            y = jnp.dot(xg.astype(BF16), wg_ref[g, :, 2 * direction * LANES:(2 * direction + 2) * LANES],
                        preferred_element_type=F32)
            t_r = jnp.tanh(y[:, :LANES] + bg_ref[2 * direction:2 * direction + 1, sl])
            t_i = jnp.tanh(y[:, LANES:] + bg_ref[2 * direction + 1:2 * direction + 2, sl])
            i = 0.5 * t_i + 0.5
            c = (-0.5 * LRU_C) * softplus[direction:direction + 1, sl]
            log_a = c * t_r + c
            a = jnp.exp(log_a)
            a_scr[:, sl] = a
            b_scr[:, sl] = jnp.sqrt((1.0 - a) * (1.0 + a)) * (i * xg)

    fill(0)

    def fwd(c, h):
        r0 = pl.multiple_of(c * SUBLANES, SUBLANES)
        a = a_scr[pl.ds(r0, SUBLANES), :]
        b = b_scr[pl.ds(r0, SUBLANES), :]
        for s in (1, 2, 4):
            keep = row8 >= s
            b = jnp.where(keep, a * pltpu.roll(b, s, 0) + b, b)
            a = jnp.where(keep, a * pltpu.roll(a, s, 0), a)
        h8 = a * h + b
        hf_scr[pl.ds(r0, SUBLANES), :] = h8
        return h8[SUBLANES - 1:SUBLANES, :]

    s_f = lax.fori_loop(0, n_chunks, fwd, h0_ref[0:1, :], unroll=SCAN_UNROLL)

    fill(1)

    def bwd(c, h):
        r0 = pl.multiple_of((n_chunks - 1 - c) * SUBLANES, SUBLANES)
        a = a_scr[pl.ds(r0, SUBLANES), :]
        b = b_scr[pl.ds(r0, SUBLANES), :]
        for s in (1, 2, 4):
            keep = row8 < SUBLANES - s
            b = jnp.where(keep, a * pltpu.roll(b, SUBLANES - s, 0) + b, b)
            a = jnp.where(keep, a * pltpu.roll(a, SUBLANES - s, 0), a)
        h8 = a * h + b
        hb_scr[pl.ds(r0, SUBLANES), :] = h8
        return h8[0:1, :]

    s_b = lax.fori_loop(0, n_chunks, bwd, h0_ref[1:2, :], unroll=SCAN_UNROLL)

    st_ref[0:1, :] = s_f
    st_ref[1:2, :] = s_b
    o_ref[...] = ((hf_scr[...] + hb_scr[...]) * jax.nn.gelu(gl_ref[...])).astype(o_ref.dtype)


def _lru(proj, tok0, b, l, col_x, col_g, w_lru, conv_w, conv_b, wg, bg, lam, h0, layer):
    cb = _pick(w_lru, tuple(c for c in (1024, 512, 256, 128) if c == 128 or l * c <= 512 * 1024))
    assert tok0 % l == 0 and col_x % cb == 0 and col_g % cb == 0
    k = conv_w.shape[1]
    n_grp = cb // LANES
    out, st = pl.pallas_call(
        functools.partial(_lru_kernel, pad_left=k // 2),
        grid=(b, w_lru // cb),
        in_specs=[
            pl.BlockSpec((l, cb), lambda bi, j: (tok0 // l + bi, col_x // cb + j)),
            pl.BlockSpec((l, cb), lambda bi, j: (tok0 // l + bi, col_g // cb + j)),
            pl.BlockSpec((None, k, cb), lambda bi, j: (layer, 0, j)),
            pl.BlockSpec((None, 1, cb), lambda bi, j: (layer, 0, j)),
            pl.BlockSpec((None, n_grp, LANES, 4 * LANES), lambda bi, j: (layer, j, 0, 0)),
            pl.BlockSpec((None, 4, cb), lambda bi, j: (layer, 0, j)),
            pl.BlockSpec((None, 2, cb), lambda bi, j: (layer, 0, j)),
            pl.BlockSpec((None, 2, cb), lambda bi, j: (bi, 0, j)),
        ],
        out_specs=[
            pl.BlockSpec((l, cb), lambda bi, j: (bi, j)),
            pl.BlockSpec((None, 2, cb), lambda bi, j: (bi, 0, j)),
        ],
        out_shape=[jax.ShapeDtypeStruct((b * l, w_lru), BF16), jax.ShapeDtypeStruct((b, 2, w_lru), F32)],
        scratch_shapes=[pltpu.VMEM((l, cb), F32)] * 4,
        compiler_params=_params(("arbitrary", "arbitrary")),
        name="rglru",
    )(proj, proj, conv_w, conv_b, wg, bg, lam, h0)
    return out, st


def _lru_gate_weights(wa, wx):
    depth, _, nb, bs, _ = wa.shape
    per = LANES // bs
    eye = jnp.eye(per, dtype=wa.dtype)

    def bd(w):
        w = w.reshape(depth, nb // per, per, bs, bs)
        return jnp.einsum("lgpde,pq->lgpdqe", w, eye).reshape(depth, nb // per, LANES, LANES)

    return (0.5 * jnp.concatenate([bd(wa[:, 0]), bd(wx[:, 0]), bd(wa[:, 1]), bd(wx[:, 1])], axis=-1)).astype(BF16)


def _hy_conv_kernel(x_ref, w_ref, b_ref, o_ref, *, pad_left):
    o_ref[...] = _dwconv(x_ref[...], w_ref[...], b_ref[...], pad_left)


def _hy_conv(proj, tok0, b, l, col0, width, conv_w, conv_b, layer):
    cb = _pick(width, (512, 256, 128))
    assert tok0 % l == 0 and col0 % cb == 0
    k = conv_w.shape[1]
    return pl.pallas_call(
        functools.partial(_hy_conv_kernel, pad_left=k // 2),
        grid=(b, width // cb),
        in_specs=[
            pl.BlockSpec((l, cb), lambda bi, j: (tok0 // l + bi, col0 // cb + j)),
            pl.BlockSpec((None, k, cb), lambda bi, j: (layer, 0, j)),
            pl.BlockSpec((None, 1, cb), lambda bi, j: (layer, 0, j)),
        ],
        out_specs=pl.BlockSpec((l, cb), lambda bi, j: (bi, j)),
        out_shape=jax.ShapeDtypeStruct((b * l, width), F32),
        compiler_params=_params(("arbitrary", "arbitrary")),
        name="hyena_dwconv",
    )(proj, conv_w, conv_b)


def _dot_split(a, b):
    a_hi, b_hi = a.astype(BF16), b.astype(BF16)
    a_lo, b_lo = (a - a_hi.astype(F32)).astype(BF16), (b - b_hi.astype(F32)).astype(BF16)
    dot = functools.partial(jnp.dot, preferred_element_type=F32)
    return dot(a_hi, b_hi) + (dot(a_hi, b_lo) + dot(a_lo, b_hi))


def _hy_filter_kernel(z_ref, w1_ref, b1_ref, fr_ref, w2_ref, b2_ref, w3_ref, b3_ref, dec_ref, o_ref, h_scr):
    @pl.when(pl.program_id(0) == 0)
    def _():
        fr = fr_ref[...]
        h = jnp.sin(fr * (_dot_split(z_ref[...], w1_ref[...]) + b1_ref[...]))
        h_scr[...] = jnp.sin(fr * (_dot_split(h, w2_ref[...]) + b2_ref[...]))

    filt = _dot_split(h_scr[...], w3_ref[...]) + b3_ref[...]
    o_ref[...] = filt * jnp.exp(-z_ref[:, 0:1] * jnp.abs(dec_ref[...]))


def _pad2(x, rows, cols):
    return jnp.pad(x, ((0, rows - x.shape[0]), (0, cols - x.shape[1])))


def _hy_filters(l, w1, b1, freq, w2, b2, w3, b3, decay):
    n = w3.shape[-1]
    t = jnp.linspace(0.0, 1.0, l, dtype=F32)[:, None]
    w = 2.0 * math.pi * jnp.arange(l, dtype=F32)[:, None] / l
    f = jnp.linspace(1e-4, HY_BANDS - 1, HY_BANDS, dtype=F32)[None]
    z = _pad2(jnp.concatenate([t, jnp.cos(f * w), -jnp.sin(f * w)], axis=-1), l, LANES)
    tn = _pick(n, (512, 256, 128))
    row = lambda v: _pad2(v.reshape(1, -1), 1, LANES)
    full = lambda shape: pl.BlockSpec(shape, lambda j: (0, 0))
    return pl.pallas_call(
        _hy_filter_kernel,
        grid=(n // tn,),
        in_specs=[full((l, LANES)), full((LANES, LANES)), full((1, LANES)), full((1, LANES)),
                  full((LANES, LANES)), full((1, LANES)),
                  pl.BlockSpec((LANES, tn), lambda j: (0, j)),
                  pl.BlockSpec((1, tn), lambda j: (0, j)),
                  pl.BlockSpec((1, tn), lambda j: (0, j))],
        out_specs=pl.BlockSpec((l, tn), lambda j: (0, j)),
        out_shape=jax.ShapeDtypeStruct((l, n), F32),
        scratch_shapes=[pltpu.VMEM((l, LANES), F32)],
        compiler_params=_params(("arbitrary",)),
        name="hyena_filters",
    )(z, _pad2(w1, LANES, LANES), row(b1), row(freq), _pad2(w2, LANES, LANES), row(b2),
      _pad2(w3, LANES, n), b3.reshape(1, n), decay.reshape(1, n))


def _dft_tile(l):
    return _pick(2 * l, (512, 256, 128))


@functools.lru_cache(maxsize=None)
def _dft_matrices(l):
    tile = _dft_tile(l)
    half = tile // 2
    r = np.arange(2 * l)
    k = (r // tile) * half + (r % half)
    is_im = (r % tile) >= half
    t = np.arange(l)
    ph = ((2 * k + 1)[:, None] * t[None, :]) % (4 * l)
    ang = ph.astype(np.float64) * (math.pi / (2 * l))
    return np.where(is_im[:, None], -np.sin(ang), np.cos(ang)).astype(np.float32)


def _cmul_store(o_ref, s, f_re, f_im):
    half = s.shape[0] // 2
    s_re, s_im = s[:half], s[half:]
    o_ref[:half, :] = (s_re * f_re - s_im * f_im).astype(o_ref.dtype)
    o_ref[half:, :] = (s_re * f_im + s_im * f_re).astype(o_ref.dtype)


def _filter_spec_kernel(a_ref, hf_ref, hb_ref, o_ref, hf_scr, hb_scr):
    @pl.when(pl.program_id(2) == 0)
    def _():
        hf_scr[...] = hf_ref[...].astype(BF16)
        hb = hb_ref[...]
        row = lax.broadcasted_iota(jnp.int32, hb.shape, 0)
        hb_scr[...] = jnp.where(row == 0, 0.0, hb).astype(BF16)

    a = a_ref[...]
    sf = jnp.dot(a, hf_scr[...], preferred_element_type=F32)
    sb = jnp.dot(a, hb_scr[...], preferred_element_type=F32)
    half = a.shape[0] // 2
    o_ref[:half, :] = sf[:half] + sb[:half]
    o_ref[half:, :] = sf[half:] - sb[half:]


def _filter_spectrum(a_mat, filt, n_order, w_hy):
    n2, l = a_mat.shape
    tm = _dft_tile(l)
    tn = _pick(w_hy, (512, 256, 128))
    nj = w_hy // tn
    return pl.pallas_call(
        _filter_spec_kernel,
        grid=(n_order, nj, n2 // tm),
        in_specs=[
            pl.BlockSpec((tm, l), lambda o, j, i: (i, 0)),
            pl.BlockSpec((l, tn), lambda o, j, i: (0, (2 * o) * nj + j)),
            pl.BlockSpec((l, tn), lambda o, j, i: (0, (2 * o + 1) * nj + j)),
        ],
        out_specs=pl.BlockSpec((None, tm, tn), lambda o, j, i: (o, i, j)),
        out_shape=jax.ShapeDtypeStruct((n_order, n2, w_hy), F32),
        scratch_shapes=[pltpu.VMEM((l, tn), BF16)] * 2,
        compiler_params=_params(("arbitrary", "arbitrary", "arbitrary")),
        name="hyena_filter_spectrum",
    )(a_mat, filt, filt)


def _dft_fwd_kernel(a_ref, z_ref, f_ref, o_ref, z_scr):
    @pl.when(pl.program_id(2) == 0)
    def _():
        z_scr[...] = z_ref[...].astype(BF16)

    s = jnp.dot(a_ref[...], z_scr[...], preferred_element_type=F32)
    half = s.shape[0] // 2
    _cmul_store(o_ref, s, f_ref[:half, :], f_ref[half:, :])


def _dft_fwd(a_mat, z, z_col0, fspec, order, b, l, w_hy):
    n2 = a_mat.shape[0]
    tm = _dft_tile(l)
    tn = _pick(w_hy, (1024, 512, 256, 128))
    assert z_col0 % tn == 0
    return pl.pallas_call(
        _dft_fwd_kernel,
        grid=(b, w_hy // tn, n2 // tm),
        in_specs=[
            pl.BlockSpec((tm, l), lambda bi, j, i: (i, 0)),
            pl.BlockSpec((l, tn), lambda bi, j, i: (bi, z_col0 // tn + j)),
            pl.BlockSpec((None, tm, tn), lambda bi, j, i: (order, i, j)),
        ],
        out_specs=pl.BlockSpec((None, tm, tn), lambda bi, j, i: (bi, i, j)),
        out_shape=jax.ShapeDtypeStruct((b, n2, w_hy), BF16),
        scratch_shapes=[pltpu.VMEM((l, tn), BF16)],
        compiler_params=_params(("arbitrary", "arbitrary", "arbitrary")),
        name="hyena_dft_fwd",
    )(a_mat, z, fspec)


def _dft_inv_kernel(at_ref, y_ref, z_ref, g_ref, d_ref, o_ref, *, inv_l):
    y = jnp.dot(at_ref[...], y_ref[...], preferred_element_type=F32) * inv_l
    z = z_ref[...]
    o_ref[...] = (g_ref[...] * (y + d_ref[...] * z)).astype(o_ref.dtype)


def _dft_inv(at_mat, y, z, z_col0, gate, gate_col0, d, layer, order, b, l, w_hy, out_dtype):
    n2 = at_mat.shape[1]
    tm = _pick(l, (512, 256, 128, 64))
    tn = _pick(w_hy, (1024, 512, 256, 128))
    nt = l // tm
    assert z_col0 % tn == 0 and gate_col0 % tn == 0
    return pl.pallas_call(
        functools.partial(_dft_inv_kernel, inv_l=1.0 / l),
        grid=(b, w_hy // tn, nt),
        in_specs=[
            pl.BlockSpec((tm, n2), lambda bi, j, i: (i, 0)),
            pl.BlockSpec((None, n2, tn), lambda bi, j, i: (bi, 0, j)),
            pl.BlockSpec((tm, tn), lambda bi, j, i: (bi * nt + i, z_col0 // tn + j)),
            pl.BlockSpec((tm, tn), lambda bi, j, i: (bi * nt + i, gate_col0 // tn + j)),
            pl.BlockSpec((None, None, 1, tn), lambda bi, j, i: (layer, order, 0, j)),
        ],
        out_specs=pl.BlockSpec((tm, tn), lambda bi, j, i: (bi * nt + i, j)),
        out_shape=jax.ShapeDtypeStruct((b * l, w_hy), out_dtype),
        compiler_params=_params(("arbitrary", "arbitrary", "arbitrary")),
        name="hyena_dft_inv",
    )(at_mat, y, z, gate, d)


def _hyena_fused_kernel(*refs, n_order, pad_left, tile, inv_l):
    it = iter(refs)
    a_ref, at_ref, f_ref, d_ref = next(it), next(it), next(it), next(it)
    x_refs = [next(it) for _ in range(n_order + 1)]
    w_refs = [next(it) for _ in range(n_order + 1)]
    b_refs = [next(it) for _ in range(n_order + 1)]
    o_ref = next(it)
    half = tile // 2
    n_tiles = a_ref.shape[0] // tile
    conv = lambda n: _dwconv(x_refs[n][...], w_refs[n][...], b_refs[n][...], pad_left)
    z = conv(0)
    for o in range(n_order):
        zb = z.astype(BF16)
        y = None
        for ti in range(n_tiles):
            r0 = ti * tile
            s = jnp.dot(a_ref[r0:r0 + tile, :], zb, preferred_element_type=F32)
            s_re, s_im = s[:half], s[half:]
            f_re, f_im = f_ref[o, r0:r0 + half, :], f_ref[o, r0 + half:r0 + tile, :]
            spec = jnp.concatenate([s_re * f_re - s_im * f_im, s_re * f_im + s_im * f_re], axis=0).astype(BF16)
            t = jnp.dot(at_ref[:, r0:r0 + tile], spec, preferred_element_type=F32)
            y = t if y is None else y + t
        z = conv(o + 1) * (y * inv_l + d_ref[o] * z)
    o_ref[...] = z.astype(o_ref.dtype)


def _hyena_fused(proj, tok0, b, l, col0, w_hy, n_order, a_mat, at_mat, fspec, conv_w, conv_b, hy_d, layer):
    n2 = a_mat.shape[0]
    tn = _pick(w_hy, (1024, 512, 256, 128))
    nj = w_hy // tn
    k = conv_w.shape[1]
    assert tok0 % l == 0 and col0 % tn == 0
    col = lambda n: (col0 + n * w_hy) // tn
    x_specs = [pl.BlockSpec((l, tn), lambda j, bi, n=n: (tok0 // l + bi, col(n) + j)) for n in range(n_order + 1)]
    w_specs = [pl.BlockSpec((None, k, tn), lambda j, bi, n=n: (layer, 0, n * nj + j)) for n in range(n_order + 1)]
    b_specs = [pl.BlockSpec((None, 1, tn), lambda j, bi, n=n: (layer, 0, n * nj + j)) for n in range(n_order + 1)]
    return pl.pallas_call(
        functools.partial(_hyena_fused_kernel, n_order=n_order, pad_left=k // 2, tile=_dft_tile(l), inv_l=1.0 / l),
        grid=(nj, b),
        in_specs=[
            pl.BlockSpec((n2, l), lambda j, bi: (0, 0)),
            pl.BlockSpec((l, n2), lambda j, bi: (0, 0)),
            pl.BlockSpec((n_order, n2, tn), lambda j, bi: (0, 0, j)),
            pl.BlockSpec((None, n_order, 1, tn), lambda j, bi: (layer, 0, 0, j)),
            *x_specs, *w_specs, *b_specs,
        ],
        out_specs=pl.BlockSpec((l, tn), lambda j, bi: (bi, j)),
        out_shape=jax.ShapeDtypeStruct((b * l, w_hy), BF16),
        compiler_params=_params(("arbitrary", "arbitrary")),
        name="hyena_fused",
    )(a_mat, at_mat, fspec, hy_d, *([proj] * (n_order + 1)), *([conv_w] * (n_order + 1)),
      *([conv_b] * (n_order + 1)))


HYENA_FUSED_MAX_L = 512


def _hyena(proj, tok0, b, l, col0, w_hy, n_order, a_mat, at_mat, fspec, conv_w, conv_b, hy_d, layer):
    if l <= HYENA_FUSED_MAX_L:
        return _hyena_fused(proj, tok0, b, l, col0, w_hy, n_order, a_mat, at_mat, fspec, conv_w, conv_b, hy_d, layer)
    hyc = _hy_conv(proj, tok0, b, l, col0, (n_order + 1) * w_hy, conv_w, conv_b, layer)
    z, z_col0 = hyc, 0
    for o in range(n_order):
        y = _dft_fwd(a_mat, z, z_col0, fspec, o, b, l, w_hy)
        last = o == n_order - 1
        z = _dft_inv(at_mat, y, z, z_col0, hyc, (o + 1) * w_hy, hy_d, layer, o, b, l, w_hy, BF16 if last else F32)
        z_col0 = 0
    return z


def _merge_kernel(h_ref, wg0_ref, wg1_ref, wg2_ref, b0_ref, b1_ref, b2_ref, wb0_ref, wb1_ref, wb2_ref, o_ref):
    h = h_ref[...]
    acc = None
    for wg_ref, b_ref, wb_ref in ((wg0_ref, b0_ref, wb0_ref), (wg1_ref, b1_ref, wb1_ref), (wg2_ref, b2_ref, wb2_ref)):
        gate = jax.nn.sigmoid(jnp.dot(h, wg_ref[...], preferred_element_type=F32))
        t = gate * jnp.dot(b_ref[...], wb_ref[...], preferred_element_type=F32)
        acc = t if acc is None else acc + t
    o_ref[...] = acc.astype(o_ref.dtype)


def _merge(h, w_gate, branches, w_br, layer):
    m, d = h.shape
    wb = w_br.shape[2]
    tm = _pick(m, (1024, 512, 256, 128))
    tn = _pick(d, (512, 256, 128))
    nj = d // tn
    gate_spec = lambda n: pl.BlockSpec((None, d, tn), lambda i, j: (layer, 0, n * nj + j))
    br_spec = pl.BlockSpec((tm, wb), lambda i, j: (i, 0))
    wbr_spec = lambda n: pl.BlockSpec((None, None, wb, tn), lambda i, j: (layer, n, 0, j))
    return pl.pallas_call(
        _merge_kernel,
        grid=(m // tm, nj),
        in_specs=[
            pl.BlockSpec((tm, d), lambda i, j: (i, 0)),
            gate_spec(0), gate_spec(1), gate_spec(2),
            br_spec, br_spec, br_spec,
            wbr_spec(0), wbr_spec(1), wbr_spec(2),
        ],
        out_specs=pl.BlockSpec((tm, tn), lambda i, j: (i, j)),
        out_shape=jax.ShapeDtypeStruct((m, d), BF16),
        compiler_params=_params(("arbitrary", "arbitrary")),
        name="gated_merge",
    )(h, w_gate, w_gate, w_gate, *branches, w_br, w_br, w_br)


def _proj_residual_kernel(a_ref, w_ref, x_ref, mod_ref, o_ref, *, gate_row):
    gate = mod_ref[gate_row:gate_row + 1, :]
    o_ref[...] = x_ref[...] + gate * jnp.dot(a_ref[...], w_ref[...], preferred_element_type=F32)


def _proj_residual(a, w, x, mod, gate_row, layer, rpm):
    m, k = a.shape
    d = x.shape[1]
    tm = _row_tile(rpm)
    tn = _pick(d, (1024, 512, 256, 128) if k <= 2048 else (512, 256, 128))
    return pl.pallas_call(
        functools.partial(_proj_residual_kernel, gate_row=gate_row),
        grid=(m // tm, d // tn),
        in_specs=[
            pl.BlockSpec((tm, k), lambda i, j: (i, 0)),
            pl.BlockSpec((None, k, tn), lambda i, j: (layer, 0, j)),
            pl.BlockSpec((tm, tn), lambda i, j: (i, j)),
            pl.BlockSpec((None, 6, tn), lambda i, j: (_mod_row(i, tm, rpm), 0, j)),
        ],
        out_specs=pl.BlockSpec((tm, tn), lambda i, j: (i, j)),
        out_shape=jax.ShapeDtypeStruct((m, d), F32),
        compiler_params=_params(("arbitrary", "arbitrary")),
        name="proj_residual",
    )(a, w, x, mod)


def _ffn_up_kernel(x_ref, mod_ref, g_ref, wg_ref, wu_ref, o_ref, h_scr):
    @pl.when(pl.program_id(1) == 0)
    def _():
        _norm_tile(x_ref, mod_ref, g_ref, h_scr, 4, 3)

    h = h_scr[...]
    a = jnp.dot(h, wg_ref[...], preferred_element_type=F32)
    u = jnp.dot(h, wu_ref[...], preferred_element_type=F32)
    o_ref[...] = ((a * jax.nn.sigmoid(a)) * u).astype(o_ref.dtype)


def _ffn_up(x, mod, g, w_gate, w_up, layer, rpm):
    m, d = x.shape
    n = w_gate.shape[-1]
    tm = _row_tile(rpm)
    tn = _pick(n, (512, 256, 128))
    w_spec = pl.BlockSpec((None, d, tn), lambda i, j: (layer, 0, j))
    return pl.pallas_call(
        _ffn_up_kernel,
        grid=(m // tm, n // tn),
        in_specs=[
            pl.BlockSpec((tm, d), lambda i, j: (i, 0)),
            pl.BlockSpec((None, 6, d), lambda i, j: (_mod_row(i, tm, rpm), 0, 0)),
            pl.BlockSpec((None, 1, d), lambda i, j: (layer, 0, 0)),
            w_spec, w_spec,
        ],
        out_specs=pl.BlockSpec((tm, tn), lambda i, j: (i, j)),
        out_shape=jax.ShapeDtypeStruct((m, n), BF16),
        scratch_shapes=[pltpu.VMEM((tm, d), BF16)],
        compiler_params=_params(("arbitrary", "arbitrary")),
        name="ffn_up",
    )(x, mod, g, w_gate, w_up)


def _final_norm_kernel(x_ref, g_ref, o_ref):
    x = x_ref[...]
    o_ref[...] = (x * lax.rsqrt(jnp.mean(x * x, axis=-1, keepdims=True) + NORM_EPS)) * g_ref[...]


def _final_norm(x, tok0, n_tok, g):
    d = x.shape[1]
    tm = _pick(math.gcd(tok0, n_tok) if tok0 else n_tok, (512, 256, 128, 64))
    return pl.pallas_call(
        _final_norm_kernel,
        grid=(n_tok // tm,),
        in_specs=[pl.BlockSpec((tm, d), lambda i: (tok0 // tm + i, 0)), pl.BlockSpec((1, d), lambda i: (0, 0))],
        out_specs=pl.BlockSpec((tm, d), lambda i: (i, 0)),
        out_shape=jax.ShapeDtypeStruct((n_tok, d), F32),
        compiler_params=_params(("arbitrary",)),
        name="final_norm",
    )(x, g)


def kernel(x_prompt, x_sample, c, cache_k, cache_v, state_lru, c_ctx, w_mod, b_mod, norm_mix, norm_ffn, w_in, w_gate, att_lambda, att_subln, lru_conv_w, lru_conv_b, lru_wa, lru_ba, lru_wx, lru_bx, lru_lambda, hy_conv_w, hy_conv_b, hy_w1, hy_b1, hy_w2, hy_b2, hy_w3, hy_b3, hy_freq, hy_decay, hy_d, w_br, w_out, w_ff_gate, w_ff_up, w_ff_down, final_norm):
    b_c, l_c, d = x_prompt.shape
    b_s, l_s, _ = x_sample.shape
    depth = w_in.shape[0]
    n_heads, d_v = cache_v.shape[3], cache_v.shape[4]
    w_att = n_heads * d_v
    w_lru = lru_lambda.shape[-1]
    n_order, w_hy = hy_d.shape[1], hy_d.shape[2]
    m_ctx, m_s = b_c * l_c, b_s * l_s
    past = cache_k.shape[2]
    col_xl, col_gl, col_hy = 3 * w_att, 3 * w_att + w_lru, 3 * w_att + 2 * w_lru

    w_in_b, w_gate_b, w_br_b, w_out_b = (w.astype(BF16) for w in (w_in, w_gate, w_br, w_out))
    w_ffg_b, w_ffu_b, w_ffd_b = (w.astype(BF16) for w in (w_ff_gate, w_ff_up, w_ff_down))
    lru_wg = _lru_gate_weights(lru_wa, lru_wx)
    lru_bg = 0.5 * jnp.stack([lru_ba[:, 0], lru_bx[:, 0], lru_ba[:, 1], lru_bx[:, 1]], axis=1)
    norm_mix3, norm_ffn3 = norm_mix.reshape(depth, 1, d), norm_ffn.reshape(depth, 1, d)
    att_subln3 = att_subln.reshape(depth, 1, d_v)
    lru_conv_b3 = lru_conv_b.reshape(depth, 1, w_lru)
    hy_conv_b3 = hy_conv_b.reshape(depth, 1, -1)
    hy_d4 = hy_d.reshape(depth, n_order, 1, w_hy)

    n_rows = 1 + b_s
    assert n_rows <= SUBLANES
    c8 = jnp.zeros((SUBLANES, d), F32).at[0].set(c_ctx).at[1:n_rows].set(c)
    mod_all = _modulation(c8, w_mod, b_mod)[:, :n_rows].reshape(depth, n_rows, 6, d)

    rope = _rope_tables(l_s, d_v // 2)
    dft = {}
    for l in sorted({l_c, l_s}):
        a_mat = jnp.asarray(_dft_matrices(l)).astype(BF16)
        dft[l] = (a_mat, a_mat.T)
    h0_ctx = jnp.zeros((b_c, 2, w_lru), F32)

    def trunk_layer(x, mod, rpm, layer, b, l, rope, cache, h0, prev_kv):
        lam_init = 0.8 - 0.6 * math.exp(-0.3 * layer)
        proj, h = _norm_proj(x, mod, norm_mix3, w_in_b, layer, rpm)
        att, *kv = _attention(proj, b, l, n_heads, d_v, lam_init, att_lambda, att_subln3, layer, rope=rope,
                              cache=cache, emit_kv=prev_kv is not None,
                              prev_kv=prev_kv if prev_kv and layer == depth - 1 else ())
        lru, st = _lru(proj, 0, b, l, col_xl, col_gl, w_lru, lru_conv_w, lru_conv_b3, lru_wg, lru_bg, lru_lambda,
                       h0, layer)
        a_mat, at_mat = dft[l]
        filt = _hy_filters(l, hy_w1[layer], hy_b1[layer], hy_freq[layer], hy_w2[layer], hy_b2[layer],
                           hy_w3[layer], hy_b3[layer], hy_decay[layer])
        fspec = _filter_spectrum(a_mat, filt, n_order, w_hy)
        hyo = _hyena(proj, 0, b, l, col_hy, w_hy, n_order, a_mat, at_mat, fspec, hy_conv_w, hy_conv_b3, hy_d4, layer)
        mixed = _merge(h, w_gate_b, (att, lru, hyo), w_br_b, layer)
        x = _proj_residual(mixed, w_out_b, x, mod, 2, layer, rpm)
        u = _ffn_up(x, mod, norm_ffn3, w_ffg_b, w_ffu_b, layer, rpm)
        x = _proj_residual(u, w_ffd_b, x, mod, 5, layer, rpm)
        return x, kv, st

    xp = x_prompt.reshape(m_ctx, d)
    kvs, ss = [], []
    for layer in range(depth):
        xp, kv_l, s_l = trunk_layer(xp, mod_all[layer, :1], m_ctx, layer, b_c, l_c, None, None, h0_ctx, kvs)
        kvs.append(tuple(kv_l))
        ss.append(s_l)
    new_k, new_v = (a.reshape(b_c, depth, l_c, n_heads, d_v) for a in kvs[-1])

    xs = x_sample.reshape(m_s, d)
    for layer in range(depth):
        ck = cache_k[:, layer].reshape(b_s, past, w_att)
        cv = cache_v[:, layer].reshape(b_s, past, w_att)
        xs, _, _ = trunk_layer(xs, mod_all[layer, 1:], l_s, layer, b_s, l_s, rope, (ck, cv), state_lru[:, layer], None)

    g_fin = final_norm.reshape(1, d)
    y_prompt = _final_norm(xp, 0, m_ctx, g_fin).reshape(b_c, l_c, d)
    y_sample = _final_norm(xs, 0, m_s, g_fin).reshape(b_s, l_s, d)
    return (y_prompt, y_sample, new_k, new_v, jnp.stack(ss, axis=1))
```

```python
import functools
import math

import jax
import jax.numpy as jnp
import numpy as np
from jax import lax
from jax.experimental import pallas as pl
from jax.experimental.pallas import tpu as pltpu

F32 = jnp.float32
BF16 = jnp.bfloat16

GRID_W = 64
ROPE_THETA = 10000.0
NORM_EPS = 1e-6
SUBLN_EPS = 1e-5
LRU_C = 8.0
HY_BANDS = 8
LANES = 128
SUBLANES = 8
BF16_ROWS = 16
VMEM_LIMIT_BYTES = 56 * 1024 * 1024


def _pick(n, prefs):
    for p in prefs:
        if n % p == 0:
            return p
    raise ValueError(f"no tile in {prefs} divides {n}")


def _params(sem):
    return pltpu.CompilerParams(dimension_semantics=sem, vmem_limit_bytes=VMEM_LIMIT_BYTES)


def _row_tile(rpm):
    return _pick(rpm, (1024, 512, 256, 128))


def _mod_row(i, tm, rpm):
    return i // (rpm // tm)


def _rms_mod(x, g, sc, sh):
    y = x * lax.rsqrt(jnp.mean(x * x, axis=-1, keepdims=True) + NORM_EPS)
    return y * (g * (1.0 + sc)) + sh


def _mod_kernel(c_ref, w_ref, b_ref, o_ref):
    c = c_ref[...]
    s = (c * jax.nn.sigmoid(c)).astype(BF16)
    o_ref[...] = jnp.dot(s, w_ref[...].astype(BF16), preferred_element_type=F32) + b_ref[...]


def _modulation(c8, w_mod, b_mod):
    depth, d, n = w_mod.shape
    tn = _pick(n, (1024, 512, 256, 128))
    return pl.pallas_call(
        _mod_kernel,
        grid=(depth, n // tn),
        in_specs=[
            pl.BlockSpec((SUBLANES, d), lambda l, j: (0, 0)),
            pl.BlockSpec((None, d, tn), lambda l, j: (l, 0, j)),
            pl.BlockSpec((None, 1, tn), lambda l, j: (l, 0, j)),
        ],
        out_specs=pl.BlockSpec((None, SUBLANES, tn), lambda l, j: (l, 0, j)),
        out_shape=jax.ShapeDtypeStruct((depth, SUBLANES, n), F32),
        compiler_params=_params(("arbitrary", "arbitrary")),
        name="modulation",
    )(c8, w_mod, b_mod.reshape(depth, 1, n))


PROLOGUE_ROWS = 128


def _norm_tile(x_ref, mod_ref, g_ref, h_ref, sc_row, sh_row):
    tm = x_ref.shape[0]
    ch = math.gcd(tm, PROLOGUE_ROWS)

    def body(c, carry):
        r0 = pl.multiple_of(c * ch, ch)
        h = _rms_mod(x_ref[pl.ds(r0, ch), :], g_ref[...], mod_ref[sc_row:sc_row + 1, :], mod_ref[sh_row:sh_row + 1, :])
        h_ref[pl.ds(r0, ch), :] = h.astype(h_ref.dtype)
        return carry

    lax.fori_loop(0, tm // ch, body, 0)


def _norm_proj_kernel(x_ref, mod_ref, g_ref, w_ref, o_ref, h_ref):
    @pl.when(pl.program_id(1) == 0)
    def _():
        _norm_tile(x_ref, mod_ref, g_ref, h_ref, 1, 0)

    o_ref[...] = jnp.dot(h_ref[...], w_ref[...], preferred_element_type=F32)


def _norm_proj(x, mod, g, w, layer, rpm):
    m, d = x.shape
    n = w.shape[-1]
    tm = _row_tile(rpm)
    tn = _pick(n, (1024, 512, 256, 128))
    return pl.pallas_call(
        _norm_proj_kernel,
        grid=(m // tm, n // tn),
        in_specs=[
            pl.BlockSpec((tm, d), lambda i, j: (i, 0)),
            pl.BlockSpec((None, 6, d), lambda i, j: (_mod_row(i, tm, rpm), 0, 0)),
            pl.BlockSpec((None, 1, d), lambda i, j: (layer, 0, 0)),
            pl.BlockSpec((None, d, tn), lambda i, j: (layer, 0, j)),
        ],
        out_specs=[pl.BlockSpec((tm, tn), lambda i, j: (i, j)), pl.BlockSpec((tm, d), lambda i, j: (i, 0))],
        out_shape=[jax.ShapeDtypeStruct((m, n), F32), jax.ShapeDtypeStruct((m, d), BF16)],
        compiler_params=_params(("arbitrary", "arbitrary")),
        name="norm_proj",
    )(x, mod, g, w)


def _rope(x, cos, sin_signed):
    n = x.shape[-1]
    half = 16
    lane = lax.broadcasted_iota(jnp.int32, x.shape, 1)
    swapped = jnp.where((lane % (2 * half)) < half, pltpu.roll(x, n - half, 1), pltpu.roll(x, half, 1))
    return x * cos + swapped * sin_signed


def _attn_kernel(*refs, lam_init, d_qk, n_hb, use_rope, use_cache, emit_kv, n_prev):
    it = iter(refs)
    lam_ref, sub_ref, q_ref, k_ref, v_ref = next(it), next(it), next(it), next(it), next(it)
    if use_rope:
        cq_ref, sq_ref, ck_ref, sk_ref = next(it), next(it), next(it), next(it)
    if use_cache:
        pk_ref, pv_ref = next(it), next(it)
    prev_refs = [(next(it), next(it)) for _ in range(n_prev)]
    o_ref = next(it)
    if emit_kv:
        ko_ref, vo_ref = next(it), next(it)
    k_scr = next(it)
    d_v = 2 * d_qk
    heads = [slice(hh * d_v, (hh + 1) * d_v) for hh in range(n_hb)]

    @pl.when(pl.program_id(2) == 0)
    def _():
        if emit_kv and n_prev:
            for n, (pk, pv) in enumerate(prev_refs):
                ko_ref[n] = pk[...]
                vo_ref[n] = pv[...]
            ko_ref[n_prev] = k_ref[...]
            vo_ref[n_prev] = v_ref[...]
        elif emit_kv:
            ko_ref[...] = k_ref[...]
            vo_ref[...] = v_ref[...]
        for sl in heads:
            k = k_ref[:, sl]
            if use_rope:
                k = _rope(k, ck_ref[...], sk_ref[...])
            k_scr[:, sl] = k.astype(BF16)

    a = lam_ref[...]
    lam = (jnp.exp(jnp.sum(a[0:1] * a[1:2], axis=-1, keepdims=True))
           - jnp.exp(jnp.sum(a[2:3] * a[3:4], axis=-1, keepdims=True)) + lam_init)
    scale = d_qk ** -0.5 * math.log2(math.e)
    nt = (((1,), (1,)), ((), ()))

    for sl in heads:
        q = q_ref[:, sl]
        if use_rope:
            q = _rope(q, cq_ref[...], sq_ref[...])
        q = q * scale
        lane = lax.broadcasted_iota(jnp.int32, q.shape, 1)
        q1 = jnp.where(lane < d_qk, q, 0.0).astype(BF16)
        q2 = jnp.where(lane >= d_qk, q, 0.0).astype(BF16)
        keys = [k_scr[:, sl]]
        vals = [v_ref[:, sl].astype(BF16)]
        if use_cache:
            keys.append(pk_ref[:, sl].astype(BF16))
            vals.append(pv_ref[:, sl].astype(BF16))

        def softmax_v(qh):
            s = [lax.dot_general(qh, kk, nt, preferred_element_type=F32) for kk in keys]
            mx = functools.reduce(jnp.maximum, [jnp.max(x, axis=-1, keepdims=True) for x in s])
            p = [jnp.exp2(x - mx) for x in s]
            den = functools.reduce(jnp.add, [jnp.sum(x, axis=-1, keepdims=True) for x in p])
            pv = functools.reduce(jnp.add, [jnp.dot(x.astype(BF16), vv, preferred_element_type=F32)
                                            for x, vv in zip(p, vals)])
            return pv, 1.0 / den

        o1, r1 = softmax_v(q1)
        o2, r2 = softmax_v(q2)
        o = o1 * r1 - o2 * (lam * r2)
        y = o * lax.rsqrt(jnp.mean(o * o, axis=-1, keepdims=True) + SUBLN_EPS)
        o_ref[:, sl] = ((y * sub_ref[...]) * (1.0 - lam_init)).astype(o_ref.dtype)


def _attention(proj, b, l, h, d_v, lam_init, att_lambda, att_subln, layer, rope=None, cache=None, emit_kv=False,
               prev_kv=()):
    assert d_v == LANES
    tq = _pick(l, (1024, 512, 256, 128, 64))
    nq = l // tq
    n_hb = max(n for n in range(1, h + 1) if h % n == 0 and n * l <= max(l, 2048))
    wb = n_hb * d_v
    hg = h // n_hb
    in_specs = [
        pl.BlockSpec((None, 4, att_lambda.shape[-1]), lambda bi, hi, qi: (layer, 0, 0)),
        pl.BlockSpec((None, 1, d_v), lambda bi, hi, qi: (layer, 0, 0)),
        pl.BlockSpec((tq, wb), lambda bi, hi, qi: (bi * nq + qi, hi)),
        pl.BlockSpec((l, wb), lambda bi, hi, qi: (bi, hg + hi)),
        pl.BlockSpec((l, wb), lambda bi, hi, qi: (bi, 2 * hg + hi)),
    ]
    args = [att_lambda, att_subln, proj, proj, proj]
    if rope is not None:
        cos, sin = rope
        in_specs += [
            pl.BlockSpec((tq, d_v), lambda bi, hi, qi: (qi, 0)),
            pl.BlockSpec((tq, d_v), lambda bi, hi, qi: (qi, 0)),
            pl.BlockSpec((l, d_v), lambda bi, hi, qi: (0, 0)),
            pl.BlockSpec((l, d_v), lambda bi, hi, qi: (0, 0)),
        ]
        args += [cos, sin, cos, sin]
    if cache is not None:
        ck, cv = cache
        past = ck.shape[1]
        in_specs += [
            pl.BlockSpec((None, past, wb), lambda bi, hi, qi: (bi, 0, hi)),
            pl.BlockSpec((None, past, wb), lambda bi, hi, qi: (bi, 0, hi)),
        ]
        args += [ck, cv]
    for pk, pv in prev_kv:
        in_specs += [pl.BlockSpec((l, wb), lambda bi, hi, qi: (bi, hi))] * 2
        args += [pk, pv]
    out_specs = [pl.BlockSpec((tq, wb), lambda bi, hi, qi: (bi * nq + qi, hi))]
    out_shape = [jax.ShapeDtypeStruct((b * l, h * d_v), BF16)]
    if emit_kv and prev_kv:
        n_layers = len(prev_kv) + 1
        out_specs += [pl.BlockSpec((None, n_layers, l, wb), lambda bi, hi, qi: (bi, 0, 0, hi))] * 2
        out_shape += [jax.ShapeDtypeStruct((b, n_layers, l, h * d_v), F32)] * 2
    elif emit_kv:
        out_specs += [pl.BlockSpec((l, wb), lambda bi, hi, qi: (bi, hi))] * 2
        out_shape += [jax.ShapeDtypeStruct((b * l, h * d_v), F32)] * 2
    return pl.pallas_call(
        functools.partial(_attn_kernel, lam_init=lam_init, d_qk=d_v // 2, n_hb=n_hb, use_rope=rope is not None,
                          use_cache=cache is not None, emit_kv=emit_kv, n_prev=len(prev_kv)),
        grid=(b, hg, nq),
        in_specs=in_specs,
        out_specs=out_specs,
        out_shape=out_shape,
        scratch_shapes=[pltpu.VMEM((l, wb), BF16)],
        compiler_params=_params(("arbitrary", "arbitrary", "arbitrary")),
        name="diff_attention",
    )(*args)


def _rope_tables(l, d_qk):
    n_freq = d_qk // 4
    lane = jnp.arange(2 * d_qk)
    axis = (lane % d_qk) // (2 * n_freq)
    freq = lane % n_freq
    first = (lane % (2 * n_freq)) < n_freq
    inv_freq = ROPE_THETA ** (-freq.astype(F32) / n_freq)
    t = jnp.arange(l)
    pos = jnp.where(axis[None, :] == 0, (t // GRID_W)[:, None], (t % GRID_W)[:, None]).astype(F32)
    ang = pos * inv_freq[None, :]
    return jnp.cos(ang), jnp.where(first[None, :], -jnp.sin(ang), jnp.sin(ang))


def _shift_rows(x, d):
    n = x.shape[0]
    if d == 0:
        return x
    row = lax.broadcasted_iota(jnp.int32, x.shape, 0)
    rolled = pltpu.roll(x, (-d) % n, 0)
    ok = (row + d >= 0) & (row + d < n)
    return jnp.where(ok, rolled, 0.0)


def _dwconv(x, w, bias, pad_left):
    y = bias + w[0:1] * _shift_rows(x, -pad_left)
    for k in range(1, w.shape[0]):
        y = y + w[k:k + 1] * _shift_rows(x, k - pad_left)
    return y


SCAN_UNROLL = 8


def _lru_kernel(xl_ref, gl_ref, cw_ref, cb_ref, wg_ref, bg_ref, lam_ref, h0_ref, o_ref, st_ref,
                a_scr, b_scr, hf_scr, hb_scr, *, pad_left):
    l, cb = xl_ref.shape
    n_grp = cb // LANES
    xc = _dwconv(xl_ref[...], cw_ref[...], cb_ref[...], pad_left)
    lam = lam_ref[...]
    neg = -lam
    softplus = jnp.maximum(neg, 0.0) + jnp.log1p(jnp.exp(-jnp.abs(neg)))
    row8 = lax.broadcasted_iota(jnp.int32, (SUBLANES, cb), 0)
    n_chunks = l // SUBLANES

    def fill(direction):
        for g in range(n_grp):
            sl = slice(g * LANES, (g + 1) * LANES)
            xg = xc[:, sl]
            y = jnp.dot(xg.astype(BF16), wg_ref[g, :, 2 * direction * LANES:(2 * direction + 2) * LANES],
                        preferred_element_type=F32)
            t_r = jnp.tanh(y[:, :LANES] + bg_ref[2 * direction:2 * direction + 1, sl])
            t_i = jnp.tanh(y[:, LANES:] + bg_ref[2 * direction + 1:2 * direction + 2, sl])
            i = 0.5 * t_i + 0.5
            c = (-0.5 * LRU_C) * softplus[direction:direction + 1, sl]
            log_a = c * t_r + c
            a = jnp.exp(log_a)
            a_scr[:, sl] = a
            b_scr[:, sl] = jnp.sqrt((1.0 - a) * (1.0 + a)) * (i * xg)

    fill(0)

    def fwd(c, h):
        r0 = pl.multiple_of(c * SUBLANES, SUBLANES)
        a = a_scr[pl.ds(r0, SUBLANES), :]
        b = b_scr[pl.ds(r0, SUBLANES), :]
        for s in (1, 2, 4):
            keep = row8 >= s
            b = jnp.where(keep, a * pltpu.roll(b, s, 0) + b, b)
            a = jnp.where(keep, a * pltpu.roll(a, s, 0), a)
        h8 = a * h + b
        hf_scr[pl.ds(r0, SUBLANES), :] = h8
        return h8[SUBLANES - 1:SUBLANES, :]

    s_f = lax.fori_loop(0, n_chunks, fwd, h0_ref[0:1, :], unroll=SCAN_UNROLL)

    fill(1)

    def bwd(c, h):
        r0 = pl.multiple_of((n_chunks - 1 - c) * SUBLANES, SUBLANES)
        a = a_scr[pl.ds(r0, SUBLANES), :]
        b = b_scr[pl.ds(r0, SUBLANES), :]
        for s in (1, 2, 4):
            keep = row8 < SUBLANES - s
            b = jnp.where(keep, a * pltpu.roll(b, SUBLANES - s, 0) + b, b)
            a = jnp.where(keep, a * pltpu.roll(a, SUBLANES - s, 0), a)
        h8 = a * h + b
        hb_scr[pl.ds(r0, SUBLANES), :] = h8
        return h8[0:1, :]

    s_b = lax.fori_loop(0, n_chunks, bwd, h0_ref[1:2, :], unroll=SCAN_UNROLL)

    st_ref[0:1, :] = s_f
    st_ref[1:2, :] = s_b
    o_ref[...] = ((hf_scr[...] + hb_scr[...]) * jax.nn.gelu(gl_ref[...])).astype(o_ref.dtype)


def _lru(proj, tok0, b, l, col_x, col_g, w_lru, conv_w, conv_b, wg, bg, lam, h0, layer):
    cb = _pick(w_lru, tuple(c for c in (1024, 512, 256, 128) if c == 128 or l * c <= 512 * 1024))
    assert tok0 % l == 0 and col_x % cb == 0 and col_g % cb == 0
    k = conv_w.shape[1]
    n_grp = cb // LANES
    out, st = pl.pallas_call(
        functools.partial(_lru_kernel, pad_left=k // 2),
        grid=(b, w_lru // cb),
        in_specs=[
            pl.BlockSpec((l, cb), lambda bi, j: (tok0 // l + bi, col_x // cb + j)),
            pl.BlockSpec((l, cb), lambda bi, j: (tok0 // l + bi, col_g // cb + j)),
            pl.BlockSpec((None, k, cb), lambda bi, j: (layer, 0, j)),
            pl.BlockSpec((None, 1, cb), lambda bi, j: (layer, 0, j)),
            pl.BlockSpec((None, n_grp, LANES, 4 * LANES), lambda bi, j: (layer, j, 0, 0)),
            pl.BlockSpec((None, 4, cb), lambda bi, j: (layer, 0, j)),
            pl.BlockSpec((None, 2, cb), lambda bi, j: (layer, 0, j)),
            pl.BlockSpec((None, 2, cb), lambda bi, j: (bi, 0, j)),
        ],
        out_specs=[
            pl.BlockSpec((l, cb), lambda bi, j: (bi, j)),
            pl.BlockSpec((None, 2, cb), lambda bi, j: (bi, 0, j)),
        ],
        out_shape=[jax.ShapeDtypeStruct((b * l, w_lru), BF16), jax.ShapeDtypeStruct((b, 2, w_lru), F32)],
        scratch_shapes=[pltpu.VMEM((l, cb), F32)] * 4,
        compiler_params=_params(("arbitrary", "arbitrary")),
        name="rglru",
    )(proj, proj, conv_w, conv_b, wg, bg, lam, h0)
    return out, st


def _lru_gate_weights(wa, wx):
    depth, _, nb, bs, _ = wa.shape
    per = LANES // bs
    eye = jnp.eye(per, dtype=wa.dtype)

    def bd(w):
        w = w.reshape(depth, nb // per, per, bs, bs)
        return jnp.einsum("lgpde,pq->lgpdqe", w, eye).reshape(depth, nb // per, LANES, LANES)

    return (0.5 * jnp.concatenate([bd(wa[:, 0]), bd(wx[:, 0]), bd(wa[:, 1]), bd(wx[:, 1])], axis=-1)).astype(BF16)


def _hy_conv_kernel(x_ref, w_ref, b_ref, o_ref, *, pad_left):
    o_ref[...] = _dwconv(x_ref[...], w_ref[...], b_ref[...], pad_left)


def _hy_conv(proj, tok0, b, l, col0, width, conv_w, conv_b, layer):
    cb = _pick(width, (512, 256, 128))
    assert tok0 % l == 0 and col0 % cb == 0
    k = conv_w.shape[1]
    return pl.pallas_call(
        functools.partial(_hy_conv_kernel, pad_left=k // 2),
        grid=(b, width // cb),
        in_specs=[
            pl.BlockSpec((l, cb), lambda bi, j: (tok0 // l + bi, col0 // cb + j)),
            pl.BlockSpec((None, k, cb), lambda bi, j: (layer, 0, j)),
            pl.BlockSpec((None, 1, cb), lambda bi, j: (layer, 0, j)),
        ],
        out_specs=pl.BlockSpec((l, cb), lambda bi, j: (bi, j)),
        out_shape=jax.ShapeDtypeStruct((b * l, width), F32),
        compiler_params=_params(("arbitrary", "arbitrary")),
        name="hyena_dwconv",
    )(proj, conv_w, conv_b)


def _dot_split(a, b):
    a_hi, b_hi = a.astype(BF16), b.astype(BF16)
    a_lo, b_lo = (a - a_hi.astype(F32)).astype(BF16), (b - b_hi.astype(F32)).astype(BF16)
    dot = functools.partial(jnp.dot, preferred_element_type=F32)
    return dot(a_hi, b_hi) + (dot(a_hi, b_lo) + dot(a_lo, b_hi))


def _hy_filter_kernel(z_ref, w1_ref, b1_ref, fr_ref, w2_ref, b2_ref, w3_ref, b3_ref, dec_ref, o_ref, h_scr):
    @pl.when(pl.program_id(0) == 0)
    def _():
        fr = fr_ref[...]
        h = jnp.sin(fr * (_dot_split(z_ref[...], w1_ref[...]) + b1_ref[...]))
        h_scr[...] = jnp.sin(fr * (_dot_split(h, w2_ref[...]) + b2_ref[...]))

    filt = _dot_split(h_scr[...], w3_ref[...]) + b3_ref[...]
    o_ref[...] = filt * jnp.exp(-z_ref[:, 0:1] * jnp.abs(dec_ref[...]))


def _pad2(x, rows, cols):
    return jnp.pad(x, ((0, rows - x.shape[0]), (0, cols - x.shape[1])))


def _hy_filters(l, w1, b1, freq, w2, b2, w3, b3, decay):
    n = w3.shape[-1]
    t = jnp.linspace(0.0, 1.0, l, dtype=F32)[:, None]
    w = 2.0 * math.pi * jnp.arange(l, dtype=F32)[:, None] / l
    f = jnp.linspace(1e-4, HY_BANDS - 1, HY_BANDS, dtype=F32)[None]
    z = _pad2(jnp.concatenate([t, jnp.cos(f * w), -jnp.sin(f * w)], axis=-1), l, LANES)
    tn = _pick(n, (512, 256, 128))
    row = lambda v: _pad2(v.reshape(1, -1), 1, LANES)
    full = lambda shape: pl.BlockSpec(shape, lambda j: (0, 0))
    return pl.pallas_call(
        _hy_filter_kernel,
        grid=(n // tn,),
        in_specs=[full((l, LANES)), full((LANES, LANES)), full((1, LANES)), full((1, LANES)),
                  full((LANES, LANES)), full((1, LANES)),
                  pl.BlockSpec((LANES, tn), lambda j: (0, j)),
                  pl.BlockSpec((1, tn), lambda j: (0, j)),
                  pl.BlockSpec((1, tn), lambda j: (0, j))],
        out_specs=pl.BlockSpec((l, tn), lambda j: (0, j)),
        out_shape=jax.ShapeDtypeStruct((l, n), F32),
        scratch_shapes=[pltpu.VMEM((l, LANES), F32)],
        compiler_params=_params(("arbitrary",)),
        name="hyena_filters",
    )(z, _pad2(w1, LANES, LANES), row(b1), row(freq), _pad2(w2, LANES, LANES), row(b2),
      _pad2(w3, LANES, n), b3.reshape(1, n), decay.reshape(1, n))


def _dft_tile(l):
    return _pick(2 * l, (512, 256, 128))


@functools.lru_cache(maxsize=None)
def _dft_matrices(l):
    tile = _dft_tile(l)
    half = tile // 2
    r = np.arange(2 * l)
    k = (r // tile) * half + (r % half)
    is_im = (r % tile) >= half
    t = np.arange(l)
    ph = ((2 * k + 1)[:, None] * t[None, :]) % (4 * l)
    ang = ph.astype(np.float64) * (math.pi / (2 * l))
    return np.where(is_im[:, None], -np.sin(ang), np.cos(ang)).astype(np.float32)


def _cmul_store(o_ref, s, f_re, f_im):
    half = s.shape[0] // 2
    s_re, s_im = s[:half], s[half:]
    o_ref[:half, :] = (s_re * f_re - s_im * f_im).astype(o_ref.dtype)
    o_ref[half:, :] = (s_re * f_im + s_im * f_re).astype(o_ref.dtype)


def _filter_spec_kernel(a_ref, hf_ref, hb_ref, o_ref, hf_scr, hb_scr):
    @pl.when(pl.program_id(2) == 0)
    def _():
        hf_scr[...] = hf_ref[...].astype(BF16)
        hb = hb_ref[...]
        row = lax.broadcasted_iota(jnp.int32, hb.shape, 0)
        hb_scr[...] = jnp.where(row == 0, 0.0, hb).astype(BF16)

    a = a_ref[...]
    sf = jnp.dot(a, hf_scr[...], preferred_element_type=F32)
    sb = jnp.dot(a, hb_scr[...], preferred_element_type=F32)
    half = a.shape[0] // 2
    o_ref[:half, :] = sf[:half] + sb[:half]
    o_ref[half:, :] = sf[half:] - sb[half:]


def _filter_spectrum(a_mat, filt, n_order, w_hy):
    n2, l = a_mat.shape
    tm = _dft_tile(l)
    tn = _pick(w_hy, (512, 256, 128))
    nj = w_hy // tn
    return pl.pallas_call(
        _filter_spec_kernel,
        grid=(n_order, nj, n2 // tm),
        in_specs=[
            pl.BlockSpec((tm, l), lambda o, j, i: (i, 0)),
            pl.BlockSpec((l, tn), lambda o, j, i: (0, (2 * o) * nj + j)),
            pl.BlockSpec((l, tn), lambda o, j, i: (0, (2 * o + 1) * nj + j)),
        ],
        out_specs=pl.BlockSpec((None, tm, tn), lambda o, j, i: (o, i, j)),
        out_shape=jax.ShapeDtypeStruct((n_order, n2, w_hy), F32),
        scratch_shapes=[pltpu.VMEM((l, tn), BF16)] * 2,
        compiler_params=_params(("arbitrary", "arbitrary", "arbitrary")),
        name="hyena_filter_spectrum",
    )(a_mat, filt, filt)


def _dft_fwd_kernel(a_ref, z_ref, f_ref, o_ref, z_scr):
    @pl.when(pl.program_id(2) == 0)
    def _():
        z_scr[...] = z_ref[...].astype(BF16)

    s = jnp.dot(a_ref[...], z_scr[...], preferred_element_type=F32)
    half = s.shape[0] // 2
    _cmul_store(o_ref, s, f_ref[:half, :], f_ref[half:, :])


def _dft_fwd(a_mat, z, z_col0, fspec, order, b, l, w_hy):
    n2 = a_mat.shape[0]
    tm = _dft_tile(l)
    tn = _pick(w_hy, (1024, 512, 256, 128))
    assert z_col0 % tn == 0
    return pl.pallas_call(
        _dft_fwd_kernel,
        grid=(b, w_hy // tn, n2 // tm),
        in_specs=[
            pl.BlockSpec((tm, l), lambda bi, j, i: (i, 0)),
            pl.BlockSpec((l, tn), lambda bi, j, i: (bi, z_col0 // tn + j)),
            pl.BlockSpec((None, tm, tn), lambda bi, j, i: (order, i, j)),
        ],
        out_specs=pl.BlockSpec((None, tm, tn), lambda bi, j, i: (bi, i, j)),
        out_shape=jax.ShapeDtypeStruct((b, n2, w_hy), BF16),
        scratch_shapes=[pltpu.VMEM((l, tn), BF16)],
        compiler_params=_params(("arbitrary", "arbitrary", "arbitrary")),
        name="hyena_dft_fwd",
    )(a_mat, z, fspec)


def _dft_inv_kernel(at_ref, y_ref, z_ref, g_ref, d_ref, o_ref, *, inv_l):
    y = jnp.dot(at_ref[...], y_ref[...], preferred_element_type=F32) * inv_l
    z = z_ref[...]
    o_ref[...] = (g_ref[...] * (y + d_ref[...] * z)).astype(o_ref.dtype)


def _dft_inv(at_mat, y, z, z_col0, gate, gate_col0, d, layer, order, b, l, w_hy, out_dtype):
    n2 = at_mat.shape[1]
    tm = _pick(l, (512, 256, 128, 64))
    tn = _pick(w_hy, (1024, 512, 256, 128))
    nt = l // tm
    assert z_col0 % tn == 0 and gate_col0 % tn == 0
    return pl.pallas_call(
        functools.partial(_dft_inv_kernel, inv_l=1.0 / l),
        grid=(b, w_hy // tn, nt),
        in_specs=[
            pl.BlockSpec((tm, n2), lambda bi, j, i: (i, 0)),
            pl.BlockSpec((None, n2, tn), lambda bi, j, i: (bi, 0, j)),
            pl.BlockSpec((tm, tn), lambda bi, j, i: (bi * nt + i, z_col0 // tn + j)),
            pl.BlockSpec((tm, tn), lambda bi, j, i: (bi * nt + i, gate_col0 // tn + j)),
            pl.BlockSpec((None, None, 1, tn), lambda bi, j, i: (layer, order, 0, j)),
        ],
        out_specs=pl.BlockSpec((tm, tn), lambda bi, j, i: (bi * nt + i, j)),
        out_shape=jax.ShapeDtypeStruct((b * l, w_hy), out_dtype),
        compiler_params=_params(("arbitrary", "arbitrary", "arbitrary")),
        name="hyena_dft_inv",
    )(at_mat, y, z, gate, d)


def _hyena_fused_kernel(*refs, n_order, pad_left, tile, inv_l):
    it = iter(refs)
    a_ref, at_ref, f_ref, d_ref = next(it), next(it), next(it), next(it)
    x_refs = [next(it) for _ in range(n_order + 1)]
    w_refs = [next(it) for _ in range(n_order + 1)]
    b_refs = [next(it) for _ in range(n_order + 1)]
    o_ref = next(it)
    half = tile // 2
    n_tiles = a_ref.shape[0] // tile
    conv = lambda n: _dwconv(x_refs[n][...], w_refs[n][...], b_refs[n][...], pad_left)
    z = conv(0)
    for o in range(n_order):
        zb = z.astype(BF16)
        y = None
        for ti in range(n_tiles):
            r0 = ti * tile
            s = jnp.dot(a_ref[r0:r0 + tile, :], zb, preferred_element_type=F32)
            s_re, s_im = s[:half], s[half:]
            f_re, f_im = f_ref[o, r0:r0 + half, :], f_ref[o, r0 + half:r0 + tile, :]
            spec = jnp.concatenate([s_re * f_re - s_im * f_im, s_re * f_im + s_im * f_re], axis=0).astype(BF16)
            t = jnp.dot(at_ref[:, r0:r0 + tile], spec, preferred_element_type=F32)
            y = t if y is None else y + t
        z = conv(o + 1) * (y * inv_l + d_ref[o] * z)
    o_ref[...] = z.astype(o_ref.dtype)


def _hyena_fused(proj, tok0, b, l, col0, w_hy, n_order, a_mat, at_mat, fspec, conv_w, conv_b, hy_d, layer):
    n2 = a_mat.shape[0]
    tn = _pick(w_hy, (1024, 512, 256, 128))
    nj = w_hy // tn
    k = conv_w.shape[1]
    assert tok0 % l == 0 and col0 % tn == 0
    col = lambda n: (col0 + n * w_hy) // tn
    x_specs = [pl.BlockSpec((l, tn), lambda j, bi, n=n: (tok0 // l + bi, col(n) + j)) for n in range(n_order + 1)]
    w_specs = [pl.BlockSpec((None, k, tn), lambda j, bi, n=n: (layer, 0, n * nj + j)) for n in range(n_order + 1)]
    b_specs = [pl.BlockSpec((None, 1, tn), lambda j, bi, n=n: (layer, 0, n * nj + j)) for n in range(n_order + 1)]
    return pl.pallas_call(
        functools.partial(_hyena_fused_kernel, n_order=n_order, pad_left=k // 2, tile=_dft_tile(l), inv_l=1.0 / l),
        grid=(nj, b),
        in_specs=[
            pl.BlockSpec((n2, l), lambda j, bi: (0, 0)),
            pl.BlockSpec((l, n2), lambda j, bi: (0, 0)),
            pl.BlockSpec((n_order, n2, tn), lambda j, bi: (0, 0, j)),
            pl.BlockSpec((None, n_order, 1, tn), lambda j, bi: (layer, 0, 0, j)),
            *x_specs, *w_specs, *b_specs,
        ],
        out_specs=pl.BlockSpec((l, tn), lambda j, bi: (bi, j)),
        out_shape=jax.ShapeDtypeStruct((b * l, w_hy), BF16),
        compiler_params=_params(("arbitrary", "arbitrary")),
        name="hyena_fused",
    )(a_mat, at_mat, fspec, hy_d, *([proj] * (n_order + 1)), *([conv_w] * (n_order + 1)),
      *([conv_b] * (n_order + 1)))


HYENA_FUSED_MAX_L = 512


def _hyena(proj, tok0, b, l, col0, w_hy, n_order, a_mat, at_mat, fspec, conv_w, conv_b, hy_d, layer):
    if l <= HYENA_FUSED_MAX_L:
        return _hyena_fused(proj, tok0, b, l, col0, w_hy, n_order, a_mat, at_mat, fspec, conv_w, conv_b, hy_d, layer)
    hyc = _hy_conv(proj, tok0, b, l, col0, (n_order + 1) * w_hy, conv_w, conv_b, layer)
    z, z_col0 = hyc, 0
    for o in range(n_order):
        y = _dft_fwd(a_mat, z, z_col0, fspec, o, b, l, w_hy)
        last = o == n_order - 1
        z = _dft_inv(at_mat, y, z, z_col0, hyc, (o + 1) * w_hy, hy_d, layer, o, b, l, w_hy, BF16 if last else F32)
        z_col0 = 0
    return z


def _merge_kernel(h_ref, wg0_ref, wg1_ref, wg2_ref, b0_ref, b1_ref, b2_ref, wb0_ref, wb1_ref, wb2_ref, o_ref):
    h = h_ref[...]
    acc = None
    for wg_ref, b_ref, wb_ref in ((wg0_ref, b0_ref, wb0_ref), (wg1_ref, b1_ref, wb1_ref), (wg2_ref, b2_ref, wb2_ref)):
        gate = jax.nn.sigmoid(jnp.dot(h, wg_ref[...], preferred_element_type=F32))
        t = gate * jnp.dot(b_ref[...], wb_ref[...], preferred_element_type=F32)
        acc = t if acc is None else acc + t
    o_ref[...] = acc.astype(o_ref.dtype)


def _merge(h, w_gate, branches, w_br, layer):
    m, d = h.shape
    wb = w_br.shape[2]
    tm = _pick(m, (1024, 512, 256, 128))
    tn = _pick(d, (512, 256, 128))
    nj = d // tn
    gate_spec = lambda n: pl.BlockSpec((None, d, tn), lambda i, j: (layer, 0, n * nj + j))
    br_spec = pl.BlockSpec((tm, wb), lambda i, j: (i, 0))
    wbr_spec = lambda n: pl.BlockSpec((None, None, wb, tn), lambda i, j: (layer, n, 0, j))
    return pl.pallas_call(
        _merge_kernel,
        grid=(m // tm, nj),
        in_specs=[
            pl.BlockSpec((tm, d), lambda i, j: (i, 0)),
            gate_spec(0), gate_spec(1), gate_spec(2),
            br_spec, br_spec, br_spec,
            wbr_spec(0), wbr_spec(1), wbr_spec(2),
        ],
        out_specs=pl.BlockSpec((tm, tn), lambda i, j: (i, j)),
        out_shape=jax.ShapeDtypeStruct((m, d), BF16),
        compiler_params=_params(("arbitrary", "arbitrary")),
        name="gated_merge",
    )(h, w_gate, w_gate, w_gate, *branches, w_br, w_br, w_br)


def _proj_residual_kernel(a_ref, w_ref, x_ref, mod_ref, o_ref, *, gate_row):
    gate = mod_ref[gate_row:gate_row + 1, :]
    o_ref[...] = x_ref[...] + gate * jnp.dot(a_ref[...], w_ref[...], preferred_element_type=F32)


def _proj_residual(a, w, x, mod, gate_row, layer, rpm):
    m, k = a.shape
    d = x.shape[1]
    tm = _row_tile(rpm)
    tn = _pick(d, (1024, 512, 256, 128) if k <= 2048 else (512, 256, 128))
    return pl.pallas_call(
        functools.partial(_proj_residual_kernel, gate_row=gate_row),
        grid=(m // tm, d // tn),
        in_specs=[
            pl.BlockSpec((tm, k), lambda i, j: (i, 0)),
            pl.BlockSpec((None, k, tn), lambda i, j: (layer, 0, j)),
            pl.BlockSpec((tm, tn), lambda i, j: (i, j)),
            pl.BlockSpec((None, 6, tn), lambda i, j: (_mod_row(i, tm, rpm), 0, j)),
        ],
        out_specs=pl.BlockSpec((tm, tn), lambda i, j: (i, j)),
        out_shape=jax.ShapeDtypeStruct((m, d), F32),
        compiler_params=_params(("arbitrary", "arbitrary")),
        name="proj_residual",
    )(a, w, x, mod)


def _ffn_up_kernel(x_ref, mod_ref, g_ref, wg_ref, wu_ref, o_ref, h_scr):
    @pl.when(pl.program_id(1) == 0)
    def _():
        _norm_tile(x_ref, mod_ref, g_ref, h_scr, 4, 3)

    h = h_scr[...]
    a = jnp.dot(h, wg_ref[...], preferred_element_type=F32)
    u = jnp.dot(h, wu_ref[...], preferred_element_type=F32)
    o_ref[...] = ((a * jax.nn.sigmoid(a)) * u).astype(o_ref.dtype)


def _ffn_up(x, mod, g, w_gate, w_up, layer, rpm):
    m, d = x.shape
    n = w_gate.shape[-1]
    tm = _row_tile(rpm)
    tn = _pick(n, (512, 256, 128))
    w_spec = pl.BlockSpec((None, d, tn), lambda i, j: (layer, 0, j))
    return pl.pallas_call(
        _ffn_up_kernel,
        grid=(m // tm, n // tn),
        in_specs=[
            pl.BlockSpec((tm, d), lambda i, j: (i, 0)),
            pl.BlockSpec((None, 6, d), lambda i, j: (_mod_row(i, tm, rpm), 0, 0)),
            pl.BlockSpec((None, 1, d), lambda i, j: (layer, 0, 0)),
            w_spec, w_spec,
        ],
        out_specs=pl.BlockSpec((tm, tn), lambda i, j: (i, j)),
        out_shape=jax.ShapeDtypeStruct((m, n), BF16),
        scratch_shapes=[pltpu.VMEM((tm, d), BF16)],
        compiler_params=_params(("arbitrary", "arbitrary")),
        name="ffn_up",
    )(x, mod, g, w_gate, w_up)


def _final_norm_kernel(x_ref, g_ref, o_ref):
    x = x_ref[...]
    o_ref[...] = (x * lax.rsqrt(jnp.mean(x * x, axis=-1, keepdims=True) + NORM_EPS)) * g_ref[...]


def _final_norm(x, tok0, n_tok, g):
    d = x.shape[1]
    tm = _pick(math.gcd(tok0, n_tok) if tok0 else n_tok, (512, 256, 128, 64))
    return pl.pallas_call(
        _final_norm_kernel,
        grid=(n_tok // tm,),
        in_specs=[pl.BlockSpec((tm, d), lambda i: (tok0 // tm + i, 0)), pl.BlockSpec((1, d), lambda i: (0, 0))],
        out_specs=pl.BlockSpec((tm, d), lambda i: (i, 0)),
        out_shape=jax.ShapeDtypeStruct((n_tok, d), F32),
        compiler_params=_params(("arbitrary",)),
        name="final_norm",
    )(x, g)


def kernel(x_prompt, x_sample, c, cache_k, cache_v, state_lru, c_ctx, w_mod, b_mod, norm_mix, norm_ffn, w_in, w_gate, att_lambda, att_subln, lru_conv_w, lru_conv_b, lru_wa, lru_ba, lru_wx, lru_bx, lru_lambda, hy_conv_w, hy_conv_b, hy_w1, hy_b1, hy_w2, hy_b2, hy_w3, hy_b3, hy_freq, hy_decay, hy_d, w_br, w_out, w_ff_gate, w_ff_up, w_ff_down, final_norm):
    b_c, l_c, d = x_prompt.shape
    b_s, l_s, _ = x_sample.shape
    depth = w_in.shape[0]
    n_heads, d_v = cache_v.shape[3], cache_v.shape[4]
    w_att = n_heads * d_v
    w_lru = lru_lambda.shape[-1]
    n_order, w_hy = hy_d.shape[1], hy_d.shape[2]
    m_ctx, m_s = b_c * l_c, b_s * l_s
    past = cache_k.shape[2]
    col_xl, col_gl, col_hy = 3 * w_att, 3 * w_att + w_lru, 3 * w_att + 2 * w_lru

    w_in_b, w_gate_b, w_br_b, w_out_b = (w.astype(BF16) for w in (w_in, w_gate, w_br, w_out))
    w_ffg_b, w_ffu_b, w_ffd_b = (w.astype(BF16) for w in (w_ff_gate, w_ff_up, w_ff_down))
    lru_wg = _lru_gate_weights(lru_wa, lru_wx)
    lru_bg = 0.5 * jnp.stack([lru_ba[:, 0], lru_bx[:, 0], lru_ba[:, 1], lru_bx[:, 1]], axis=1)
    norm_mix3, norm_ffn3 = norm_mix.reshape(depth, 1, d), norm_ffn.reshape(depth, 1, d)
    att_subln3 = att_subln.reshape(depth, 1, d_v)
    lru_conv_b3 = lru_conv_b.reshape(depth, 1, w_lru)
    hy_conv_b3 = hy_conv_b.reshape(depth, 1, -1)
    hy_d4 = hy_d.reshape(depth, n_order, 1, w_hy)

    n_rows = 1 + b_s
    assert n_rows <= SUBLANES
    c8 = jnp.zeros((SUBLANES, d), F32).at[0].set(c_ctx).at[1:n_rows].set(c)
    mod_all = _modulation(c8, w_mod, b_mod)[:, :n_rows].reshape(depth, n_rows, 6, d)

    rope = _rope_tables(l_s, d_v // 2)
    dft = {}
    for l in sorted({l_c, l_s}):
        a_mat = jnp.asarray(_dft_matrices(l)).astype(BF16)
        dft[l] = (a_mat, a_mat.T)
    h0_ctx = jnp.zeros((b_c, 2, w_lru), F32)

    def trunk_layer(x, mod, rpm, layer, b, l, rope, cache, h0, prev_kv):
        lam_init = 0.8 - 0.6 * math.exp(-0.3 * layer)
        proj, h = _norm_proj(x, mod, norm_mix3, w_in_b, layer, rpm)
        att, *kv = _attention(proj, b, l, n_heads, d_v, lam_init, att_lambda, att_subln3, layer, rope=rope,
                              cache=cache, emit_kv=prev_kv is not None,
                              prev_kv=prev_kv if prev_kv and layer == depth - 1 else ())
        lru, st = _lru(proj, 0, b, l, col_xl, col_gl, w_lru, lru_conv_w, lru_conv_b3, lru_wg, lru_bg, lru_lambda,
                       h0, layer)
        a_mat, at_mat = dft[l]
        filt = _hy_filters(l, hy_w1[layer], hy_b1[layer], hy_freq[layer], hy_w2[layer], hy_b2[layer],
                           hy_w3[layer], hy_b3[layer], hy_decay[layer])
        fspec = _filter_spectrum(a_mat, filt, n_order, w_hy)
        hyo = _hyena(proj, 0, b, l, col_hy, w_hy, n_order, a_mat, at_mat, fspec, hy_conv_w, hy_conv_b3, hy_d4, layer)
        mixed = _merge(h, w_gate_b, (att, lru, hyo), w_br_b, layer)
        x = _proj_residual(mixed, w_out_b, x, mod, 2, layer, rpm)
        u = _ffn_up(x, mod, norm_ffn3, w_ffg_b, w_ffu_b, layer, rpm)
        x = _proj_residual(u, w_ffd_b, x, mod, 5, layer, rpm)
        return x, kv, st

    xp = x_prompt.reshape(m_ctx, d)
    kvs, ss = [], []
    for layer in range(depth):
        xp, kv_l, s_l = trunk_layer(xp, mod_all[layer, :1], m_ctx, layer, b_c, l_c, None, None, h0_ctx, kvs)
        kvs.append(tuple(kv_l))
        ss.append(s_l)
    new_k, new_v = (a.reshape(b_c, depth, l_c, n_heads, d_v) for a in kvs[-1])

    xs = x_sample.reshape(m_s, d)
    for layer in range(depth):
        ck = cache_k[:, layer].reshape(b_s, past, w_att)
        cv = cache_v[:, layer].reshape(b_s, past, w_att)
        xs, _, _ = trunk_layer(xs, mod_all[layer, 1:], l_s, layer, b_s, l_s, rope, (ck, cv), state_lru[:, layer], None)

    g_fin = final_norm.reshape(1, d)
    y_prompt = _final_norm(xp, 0, m_ctx, g_fin).reshape(b_c, l_c, d)
    y_sample = _final_norm(xs, 0, m_s, g_fin).reshape(b_s, l_s, d)
    return (y_prompt, y_sample, new_k, new_v, jnp.stack(ss, axis=1))
```

```python
import functools
import math

import jax
import jax.numpy as jnp
import numpy as np
from jax import lax
from jax.experimental import pallas as pl
from jax.experimental.pallas import tpu as pltpu

F32 = jnp.float32
BF16 = jnp.bfloat16

GRID_W = 64
ROPE_THETA = 10000.0
NORM_EPS = 1e-6
SUBLN_EPS = 1e-5
LRU_C = 8.0
HY_BANDS = 8
LANES = 128
SUBLANES = 8
BF16_ROWS = 16
VMEM_LIMIT_BYTES = 56 * 1024 * 1024


def _pick(n, prefs):
    for p in prefs:
        if n % p == 0:
            return p
    raise ValueError(f"no tile in {prefs} divides {n}")


def _params(sem):
    return pltpu.CompilerParams(dimension_semantics=sem, vmem_limit_bytes=VMEM_LIMIT_BYTES)


def _row_tile(rpm):
    return _pick(rpm, (1024, 512, 256, 128))


def _mod_row(i, tm, rpm):
    return i // (rpm // tm)


def _rms_mod(x, g, sc, sh):
    y = x * lax.rsqrt(jnp.mean(x * x, axis=-1, keepdims=True) + NORM_EPS)
    return y * (g * (1.0 + sc)) + sh


def _mod_kernel(c_ref, w_ref, b_ref, o_ref):
    c = c_ref[...]
    s = (c * jax.nn.sigmoid(c)).astype(BF16)
    o_ref[...] = jnp.dot(s, w_ref[...].astype(BF16), preferred_element_type=F32) + b_ref[...]


def _modulation(c8, w_mod, b_mod):
    depth, d, n = w_mod.shape
    tn = _pick(n, (1024, 512, 256, 128))
    return pl.pallas_call(
        _mod_kernel,
        grid=(depth, n // tn),
        in_specs=[
            pl.BlockSpec((SUBLANES, d), lambda l, j: (0, 0)),
            pl.BlockSpec((None, d, tn), lambda l, j: (l, 0, j)),
            pl.BlockSpec((None, 1, tn), lambda l, j: (l, 0, j)),
        ],
        out_specs=pl.BlockSpec((None, SUBLANES, tn), lambda l, j: (l, 0, j)),
        out_shape=jax.ShapeDtypeStruct((depth, SUBLANES, n), F32),
        compiler_params=_params(("arbitrary", "arbitrary")),
        name="modulation",
    )(c8, w_mod, b_mod.reshape(depth, 1, n))


PROLOGUE_ROWS = 128


def _norm_tile(x_ref, mod_ref, g_ref, h_ref, sc_row, sh_row):
    tm = x_ref.shape[0]
    ch = math.gcd(tm, PROLOGUE_ROWS)

    def body(c, carry):
        r0 = pl.multiple_of(c * ch, ch)
        h = _rms_mod(x_ref[pl.ds(r0, ch), :], g_ref[...], mod_ref[sc_row:sc_row + 1, :], mod_ref[sh_row:sh_row + 1, :])
        h_ref[pl.ds(r0, ch), :] = h.astype(h_ref.dtype)
        return carry

    lax.fori_loop(0, tm // ch, body, 0)


def _norm_proj_kernel(x_ref, mod_ref, g_ref, w_ref, o_ref, h_ref):
    @pl.when(pl.program_id(1) == 0)
    def _():
        _norm_tile(x_ref, mod_ref, g_ref, h_ref, 1, 0)

    o_ref[...] = jnp.dot(h_ref[...], w_ref[...], preferred_element_type=F32)


def _norm_proj(x, mod, g, w, layer, rpm):
    m, d = x.shape
    n = w.shape[-1]
    tm = _row_tile(rpm)
    tn = _pick(n, (1024, 512, 256, 128))
    return pl.pallas_call(
        _norm_proj_kernel,
        grid=(m // tm, n // tn),
        in_specs=[
            pl.BlockSpec((tm, d), lambda i, j: (i, 0)),
            pl.BlockSpec((None, 6, d), lambda i, j: (_mod_row(i, tm, rpm), 0, 0)),
            pl.BlockSpec((None, 1, d), lambda i, j: (layer, 0, 0)),
            pl.BlockSpec((None, d, tn), lambda i, j: (layer, 0, j)),
        ],
        out_specs=[pl.BlockSpec((tm, tn), lambda i, j: (i, j)), pl.BlockSpec((tm, d), lambda i, j: (i, 0))],
        out_shape=[jax.ShapeDtypeStruct((m, n), F32), jax.ShapeDtypeStruct((m, d), BF16)],
        compiler_params=_params(("arbitrary", "arbitrary")),
        name="norm_proj",
    )(x, mod, g, w)


def _rope(x, cos, sin_signed):
    n = x.shape[-1]
    half = 16
    lane = lax.broadcasted_iota(jnp.int32, x.shape, 1)
    swapped = jnp.where((lane % (2 * half)) < half, pltpu.roll(x, n - half, 1), pltpu.roll(x, half, 1))
    return x * cos + swapped * sin_signed


STACKED_SCORE_ELEMS = 512 * 1024


def _attn_kernel(*refs, lam_init, d_qk, n_hb, use_rope, use_cache, emit_kv, n_prev):
    it = iter(refs)
    lam_ref, sub_ref, q_ref, k_ref, v_ref = next(it), next(it), next(it), next(it), next(it)
    if use_rope:
        cq_ref, sq_ref, ck_ref, sk_ref = next(it), next(it), next(it), next(it)
    if use_cache:
        pk_ref, pv_ref = next(it), next(it)
    prev_refs = [(next(it), next(it)) for _ in range(n_prev)]
    o_ref = next(it)
    if emit_kv:
        ko_ref, vo_ref = next(it), next(it)
    k_scr = next(it)
    d_v = 2 * d_qk
    heads = [slice(hh * d_v, (hh + 1) * d_v) for hh in range(n_hb)]

    @pl.when(pl.program_id(2) == 0)
    def _():
        if emit_kv and n_prev:
            for n, (pk, pv) in enumerate(prev_refs):
                ko_ref[n] = pk[...]
                vo_ref[n] = pv[...]
            ko_ref[n_prev] = k_ref[...]
            vo_ref[n_prev] = v_ref[...]
        elif emit_kv:
            ko_ref[...] = k_ref[...]
            vo_ref[...] = v_ref[...]
        for sl in heads:
            k = k_ref[:, sl]
            if use_rope:
                k = _rope(k, ck_ref[...], sk_ref[...])
            k_scr[:, sl] = k.astype(BF16)

    a = lam_ref[...]
    lam = (jnp.exp(jnp.sum(a[0:1] * a[1:2], axis=-1, keepdims=True))
           - jnp.exp(jnp.sum(a[2:3] * a[3:4], axis=-1, keepdims=True)) + lam_init)
    scale = d_qk ** -0.5 * math.log2(math.e)
    nt = (((1,), (1,)), ((), ()))

    for sl in heads:
        q = q_ref[:, sl]
        if use_rope:
            q = _rope(q, cq_ref[...], sq_ref[...])
        q = q * scale
        lane = lax.broadcasted_iota(jnp.int32, q.shape, 1)
        q1 = jnp.where(lane < d_qk, q, 0.0).astype(BF16)
        q2 = jnp.where(lane >= d_qk, q, 0.0).astype(BF16)
        keys = [k_scr[:, sl]]
        vals = [v_ref[:, sl].astype(BF16)]
        if use_cache:
            keys.append(pk_ref[:, sl].astype(BF16))
            vals.append(pv_ref[:, sl].astype(BF16))

        def softmax_v(qh):
            s = [lax.dot_general(qh, kk, nt, preferred_element_type=F32) for kk in keys]
            mx = functools.reduce(jnp.maximum, [jnp.max(x, axis=-1, keepdims=True) for x in s])
            p = [jnp.exp2(x - mx) for x in s]
            den = functools.reduce(jnp.add, [jnp.sum(x, axis=-1, keepdims=True) for x in p])
            pv = functools.reduce(jnp.add, [jnp.dot(x.astype(BF16), vv, preferred_element_type=F32)
                                            for x, vv in zip(p, vals)])
            return pv, 1.0 / den

        tq = q.shape[0]
        if 2 * tq * sum(kk.shape[0] for kk in keys) <= STACKED_SCORE_ELEMS:
            pv, r = softmax_v(jnp.concatenate([q1, q2], axis=0))
            o = pv[:tq] * r[:tq] - pv[tq:] * (lam * r[tq:])
        else:
            o1, r1 = softmax_v(q1)
            o2, r2 = softmax_v(q2)
            o = o1 * r1 - o2 * (lam * r2)
        y = o * lax.rsqrt(jnp.mean(o * o, axis=-1, keepdims=True) + SUBLN_EPS)
        o_ref[:, sl] = ((y * sub_ref[...]) * (1.0 - lam_init)).astype(o_ref.dtype)


def _attention(proj, b, l, h, d_v, lam_init, att_lambda, att_subln, layer, rope=None, cache=None, emit_kv=False,
               prev_kv=()):
    assert d_v == LANES
    tq = _pick(l, (1024, 512, 256, 128, 64))
    nq = l // tq
    n_hb = max(n for n in range(1, h + 1) if h % n == 0 and n * l <= max(l, 2048))
    wb = n_hb * d_v
    hg = h // n_hb
    in_specs = [
        pl.BlockSpec((None, 4, att_lambda.shape[-1]), lambda bi, hi, qi: (layer, 0, 0)),
        pl.BlockSpec((None, 1, d_v), lambda bi, hi, qi: (layer, 0, 0)),
        pl.BlockSpec((tq, wb), lambda bi, hi, qi: (bi * nq + qi, hi)),
        pl.BlockSpec((l, wb), lambda bi, hi, qi: (bi, hg + hi)),
        pl.BlockSpec((l, wb), lambda bi, hi, qi: (bi, 2 * hg + hi)),
    ]
    args = [att_lambda, att_subln, proj, proj, proj]
    if rope is not None:
        cos, sin = rope
        in_specs += [
            pl.BlockSpec((tq, d_v), lambda bi, hi, qi: (qi, 0)),
            pl.BlockSpec((tq, d_v), lambda bi, hi, qi: (qi, 0)),
            pl.BlockSpec((l, d_v), lambda bi, hi, qi: (0, 0)),
            pl.BlockSpec((l, d_v), lambda bi, hi, qi: (0, 0)),
        ]
        args += [cos, sin, cos, sin]
    if cache is not None:
        ck, cv = cache
        past = ck.shape[1]
        in_specs += [
            pl.BlockSpec((None, past, wb), lambda bi, hi, qi: (bi, 0, hi)),
            pl.BlockSpec((None, past, wb), lambda bi, hi, qi: (bi, 0, hi)),
        ]
        args += [ck, cv]
    for pk, pv in prev_kv:
        in_specs += [pl.BlockSpec((l, wb), lambda bi, hi, qi: (bi, hi))] * 2
        args += [pk, pv]
    out_specs = [pl.BlockSpec((tq, wb), lambda bi, hi, qi: (bi * nq + qi, hi))]
    out_shape = [jax.ShapeDtypeStruct((b * l, h * d_v), BF16)]
    if emit_kv and prev_kv:
        n_layers = len(prev_kv) + 1
        out_specs += [pl.BlockSpec((None, n_layers, l, wb), lambda bi, hi, qi: (bi, 0, 0, hi))] * 2
        out_shape += [jax.ShapeDtypeStruct((b, n_layers, l, h * d_v), F32)] * 2
    elif emit_kv:
        out_specs += [pl.BlockSpec((l, wb), lambda bi, hi, qi: (bi, hi))] * 2
        out_shape += [jax.ShapeDtypeStruct((b * l, h * d_v), F32)] * 2
    return pl.pallas_call(
        functools.partial(_attn_kernel, lam_init=lam_init, d_qk=d_v // 2, n_hb=n_hb, use_rope=rope is not None,
                          use_cache=cache is not None, emit_kv=emit_kv, n_prev=len(prev_kv)),
        grid=(b, hg, nq),
        in_specs=in_specs,
        out_specs=out_specs,
        out_shape=out_shape,
        scratch_shapes=[pltpu.VMEM((l, wb), BF16)],
        compiler_params=_params(("arbitrary", "arbitrary", "arbitrary")),
        name="diff_attention",
    )(*args)


def _rope_tables(l, d_qk):
    n_freq = d_qk // 4
    lane = jnp.arange(2 * d_qk)
    axis = (lane % d_qk) // (2 * n_freq)
    freq = lane % n_freq
    first = (lane % (2 * n_freq)) < n_freq
    inv_freq = ROPE_THETA ** (-freq.astype(F32) / n_freq)
    t = jnp.arange(l)
    pos = jnp.where(axis[None, :] == 0, (t // GRID_W)[:, None], (t % GRID_W)[:, None]).astype(F32)
    ang = pos * inv_freq[None, :]
    return jnp.cos(ang), jnp.where(first[None, :], -jnp.sin(ang), jnp.sin(ang))


def _shift_rows(x, d):
    n = x.shape[0]
    if d == 0:
        return x
    row = lax.broadcasted_iota(jnp.int32, x.shape, 0)
    rolled = pltpu.roll(x, (-d) % n, 0)
    ok = (row + d >= 0) & (row + d < n)
    return jnp.where(ok, rolled, 0.0)


def _dwconv(x, w, bias, pad_left):
    y = bias + w[0:1] * _shift_rows(x, -pad_left)
    for k in range(1, w.shape[0]):
        y = y + w[k:k + 1] * _shift_rows(x, k - pad_left)
    return y


SCAN_UNROLL = 8


def _lru_kernel(xl_ref, gl_ref, cw_ref, cb_ref, wg_ref, bg_ref, lam_ref, h0_ref, o_ref, st_ref,
                a_scr, b_scr, hf_scr, hb_scr, *, pad_left):
    l, cb = xl_ref.shape
    n_grp = cb // LANES
    xc = _dwconv(xl_ref[...], cw_ref[...], cb_ref[...], pad_left)
    lam = lam_ref[...]
    neg = -lam
    softplus = jnp.maximum(neg, 0.0) + jnp.log1p(jnp.exp(-jnp.abs(neg)))
    row8 = lax.broadcasted_iota(jnp.int32, (SUBLANES, cb), 0)
    n_chunks = l // SUBLANES

    def fill(direction):
        for g in range(n_grp):
            sl = slice(g * LANES, (g + 1) * LANES)
            xg = xc[:, sl]
            y = jnp.dot(xg.astype(BF16), wg_ref[g, :, 2 * direction * LANES:(2 * direction + 2) * LANES],
                        preferred_element_type=F32)
            t_r = jnp.tanh(y[:, :LANES] + bg_ref[2 * direction:2 * direction + 1, sl])
            t_i = jnp.tanh(y[:, LANES:] + bg_ref[2 * direction + 1:2 * direction + 2, sl])
            i = 0.5 * t_i + 0.5
            c = (-0.5 * LRU_C) * softplus[direction:direction + 1, sl]
            log_a = c * t_r + c
            a = jnp.exp(log_a)
            a_scr[:, sl] = a
            b_scr[:, sl] = jnp.sqrt((1.0 - a) * (1.0 + a)) * (i * xg)

    fill(0)

    def fwd(c, h):
        r0 = pl.multiple_of(c * SUBLANES, SUBLANES)
        a = a_scr[pl.ds(r0, SUBLANES), :]
        b = b_scr[pl.ds(r0, SUBLANES), :]
        for s in (1, 2, 4):
            keep = row8 >= s
            b = jnp.where(keep, a * pltpu.roll(b, s, 0) + b, b)
            a = jnp.where(keep, a * pltpu.roll(a, s, 0), a)
        h8 = a * h + b
        hf_scr[pl.ds(r0, SUBLANES), :] = h8
        return h8[SUBLANES - 1:SUBLANES, :]

    s_f = lax.fori_loop(0, n_chunks, fwd, h0_ref[0:1, :], unroll=SCAN_UNROLL)

    fill(1)

    def bwd(c, h):
        r0 = pl.multiple_of((n_chunks - 1 - c) * SUBLANES, SUBLANES)
        a = a_scr[pl.ds(r0, SUBLANES), :]
        b = b_scr[pl.ds(r0, SUBLANES), :]
        for s in (1, 2, 4):
            keep = row8 < SUBLANES - s
            b = jnp.where(keep, a * pltpu.roll(b, SUBLANES - s, 0) + b, b)
            a = jnp.where(keep, a * pltpu.roll(a, SUBLANES - s, 0), a)
        h8 = a * h + b
        hb_scr[pl.ds(r0, SUBLANES), :] = h8
        return h8[0:1, :]

    s_b = lax.fori_loop(0, n_chunks, bwd, h0_ref[1:2, :], unroll=SCAN_UNROLL)

    st_ref[0:1, :] = s_f
    st_ref[1:2, :] = s_b
    o_ref[...] = ((hf_scr[...] + hb_scr[...]) * jax.nn.gelu(gl_ref[...])).astype(o_ref.dtype)


def _lru(proj, tok0, b, l, col_x, col_g, w_lru, conv_w, conv_b, wg, bg, lam, h0, layer):
    cb = _pick(w_lru, tuple(c for c in (1024, 512, 256, 128) if c == 128 or l * c <= 512 * 1024))
    assert tok0 % l == 0 and col_x % cb == 0 and col_g % cb == 0
    k = conv_w.shape[1]
    n_grp = cb // LANES
    out, st = pl.pallas_call(
        functools.partial(_lru_kernel, pad_left=k // 2),
        grid=(b, w_lru // cb),
        in_specs=[
            pl.BlockSpec((l, cb), lambda bi, j: (tok0 // l + bi, col_x // cb + j)),
            pl.BlockSpec((l, cb), lambda bi, j: (tok0 // l + bi, col_g // cb + j)),
            pl.BlockSpec((None, k, cb), lambda bi, j: (layer, 0, j)),
            pl.BlockSpec((None, 1, cb), lambda bi, j: (layer, 0, j)),
            pl.BlockSpec((None, n_grp, LANES, 4 * LANES), lambda bi, j: (layer, j, 0, 0)),
            pl.BlockSpec((None, 4, cb), lambda bi, j: (layer, 0, j)),
            pl.BlockSpec((None, 2, cb), lambda bi, j: (layer, 0, j)),
            pl.BlockSpec((None, 2, cb), lambda bi, j: (bi, 0, j)),
        ],
        out_specs=[
            pl.BlockSpec((l, cb), lambda bi, j: (bi, j)),
            pl.BlockSpec((None, 2, cb), lambda bi, j: (bi, 0, j)),
        ],
        out_shape=[jax.ShapeDtypeStruct((b * l, w_lru), BF16), jax.ShapeDtypeStruct((b, 2, w_lru), F32)],
        scratch_shapes=[pltpu.VMEM((l, cb), F32)] * 4,
        compiler_params=_params(("arbitrary", "arbitrary")),
        name="rglru",
    )(proj, proj, conv_w, conv_b, wg, bg, lam, h0)
    return out, st


def _lru_gate_weights(wa, wx):
    depth, _, nb, bs, _ = wa.shape
    per = LANES // bs
    eye = jnp.eye(per, dtype=wa.dtype)

    def bd(w):
        w = w.reshape(depth, nb // per, per, bs, bs)
        return jnp.einsum("lgpde,pq->lgpdqe", w, eye).reshape(depth, nb // per, LANES, LANES)

    return (0.5 * jnp.concatenate([bd(wa[:, 0]), bd(wx[:, 0]), bd(wa[:, 1]), bd(wx[:, 1])], axis=-1)).astype(BF16)


def _hy_conv_kernel(x_ref, w_ref, b_ref, o_ref, *, pad_left):
    o_ref[...] = _dwconv(x_ref[...], w_ref[...], b_ref[...], pad_left)


def _hy_conv(proj, tok0, b, l, col0, width, conv_w, conv_b, layer):
    cb = _pick(width, (512, 256, 128))
    assert tok0 % l == 0 and col0 % cb == 0
    k = conv_w.shape[1]
    return pl.pallas_call(
        functools.partial(_hy_conv_kernel, pad_left=k // 2),
        grid=(b, width // cb),
        in_specs=[
            pl.BlockSpec((l, cb), lambda bi, j: (tok0 // l + bi, col0 // cb + j)),
            pl.BlockSpec((None, k, cb), lambda bi, j: (layer, 0, j)),
            pl.BlockSpec((None, 1, cb), lambda bi, j: (layer, 0, j)),
        ],
        out_specs=pl.BlockSpec((l, cb), lambda bi, j: (bi, j)),
        out_shape=jax.ShapeDtypeStruct((b * l, width), F32),
        compiler_params=_params(("arbitrary", "arbitrary")),
        name="hyena_dwconv",
    )(proj, conv_w, conv_b)


def _dot_split(a, b):
    a_hi, b_hi = a.astype(BF16), b.astype(BF16)
    a_lo, b_lo = (a - a_hi.astype(F32)).astype(BF16), (b - b_hi.astype(F32)).astype(BF16)
    dot = functools.partial(jnp.dot, preferred_element_type=F32)
    return dot(a_hi, b_hi) + (dot(a_hi, b_lo) + dot(a_lo, b_hi))


def _hy_filter_kernel(z_ref, w1_ref, b1_ref, fr_ref, w2_ref, b2_ref, w3_ref, b3_ref, dec_ref, o_ref, h_scr):
    @pl.when(pl.program_id(0) == 0)
    def _():
        fr = fr_ref[...]
        h = jnp.sin(fr * (_dot_split(z_ref[...], w1_ref[...]) + b1_ref[...]))
        h_scr[...] = jnp.sin(fr * (_dot_split(h, w2_ref[...]) + b2_ref[...]))

    filt = _dot_split(h_scr[...], w3_ref[...]) + b3_ref[...]
    o_ref[...] = filt * jnp.exp(-z_ref[:, 0:1] * jnp.abs(dec_ref[...]))


def _pad2(x, rows, cols):
    return jnp.pad(x, ((0, rows - x.shape[0]), (0, cols - x.shape[1])))


def _hy_filters(l, w1, b1, freq, w2, b2, w3, b3, decay):
    n = w3.shape[-1]
    t = jnp.linspace(0.0, 1.0, l, dtype=F32)[:, None]
    w = 2.0 * math.pi * jnp.arange(l, dtype=F32)[:, None] / l
    f = jnp.linspace(1e-4, HY_BANDS - 1, HY_BANDS, dtype=F32)[None]
    z = _pad2(jnp.concatenate([t, jnp.cos(f * w), -jnp.sin(f * w)], axis=-1), l, LANES)
    tn = _pick(n, (512, 256, 128))
    row = lambda v: _pad2(v.reshape(1, -1), 1, LANES)
    full = lambda shape: pl.BlockSpec(shape, lambda j: (0, 0))
    return pl.pallas_call(
        _hy_filter_kernel,
        grid=(n // tn,),
        in_specs=[full((l, LANES)), full((LANES, LANES)), full((1, LANES)), full((1, LANES)),
                  full((LANES, LANES)), full((1, LANES)),
                  pl.BlockSpec((LANES, tn), lambda j: (0, j)),
                  pl.BlockSpec((1, tn), lambda j: (0, j)),
                  pl.BlockSpec((1, tn), lambda j: (0, j))],
        out_specs=pl.BlockSpec((l, tn), lambda j: (0, j)),
        out_shape=jax.ShapeDtypeStruct((l, n), F32),
        scratch_shapes=[pltpu.VMEM((l, LANES), F32)],
        compiler_params=_params(("arbitrary",)),
        name="hyena_filters",
    )(z, _pad2(w1, LANES, LANES), row(b1), row(freq), _pad2(w2, LANES, LANES), row(b2),
      _pad2(w3, LANES, n), b3.reshape(1, n), decay.reshape(1, n))


def _dft_tile(l):
    return _pick(2 * l, (512, 256, 128))


@functools.lru_cache(maxsize=None)
def _dft_matrices(l):
    tile = _dft_tile(l)
    half = tile // 2
    r = np.arange(2 * l)
    k = (r // tile) * half + (r % half)
    is_im = (r % tile) >= half
    t = np.arange(l)
    ph = ((2 * k + 1)[:, None] * t[None, :]) % (4 * l)
    ang = ph.astype(np.float64) * (math.pi / (2 * l))
    return np.where(is_im[:, None], -np.sin(ang), np.cos(ang)).astype(np.float32)


def _cmul_store(o_ref, s, f_re, f_im):
    half = s.shape[0] // 2
    s_re, s_im = s[:half], s[half:]
    o_ref[:half, :] = (s_re * f_re - s_im * f_im).astype(o_ref.dtype)
    o_ref[half:, :] = (s_re * f_im + s_im * f_re).astype(o_ref.dtype)


def _filter_spec_kernel(a_ref, hf_ref, hb_ref, o_ref, hf_scr, hb_scr):
    @pl.when(pl.program_id(2) == 0)
    def _():
        hf_scr[...] = hf_ref[...].astype(BF16)
        hb = hb_ref[...]
        row = lax.broadcasted_iota(jnp.int32, hb.shape, 0)
        hb_scr[...] = jnp.where(row == 0, 0.0, hb).astype(BF16)

    a = a_ref[...]
    sf = jnp.dot(a, hf_scr[...], preferred_element_type=F32)
    sb = jnp.dot(a, hb_scr[...], preferred_element_type=F32)
    half = a.shape[0] // 2
    o_ref[:half, :] = sf[:half] + sb[:half]
    o_ref[half:, :] = sf[half:] - sb[half:]


def _filter_spectrum(a_mat, filt, n_order, w_hy):
    n2, l = a_mat.shape
    tm = _dft_tile(l)
    tn = _pick(w_hy, (512, 256, 128))
    nj = w_hy // tn
    return pl.pallas_call(
        _filter_spec_kernel,
        grid=(n_order, nj, n2 // tm),
        in_specs=[
            pl.BlockSpec((tm, l), lambda o, j, i: (i, 0)),
            pl.BlockSpec((l, tn), lambda o, j, i: (0, (2 * o) * nj + j)),
            pl.BlockSpec((l, tn), lambda o, j, i: (0, (2 * o + 1) * nj + j)),
        ],
        out_specs=pl.BlockSpec((None, tm, tn), lambda o, j, i: (o, i, j)),
        out_shape=jax.ShapeDtypeStruct((n_order, n2, w_hy), F32),
        scratch_shapes=[pltpu.VMEM((l, tn), BF16)] * 2,
        compiler_params=_params(("arbitrary", "arbitrary", "arbitrary")),
        name="hyena_filter_spectrum",
    )(a_mat, filt, filt)


def _dft_fwd_kernel(a_ref, z_ref, f_ref, o_ref, z_scr):
    @pl.when(pl.program_id(2) == 0)
    def _():
        z_scr[...] = z_ref[...].astype(BF16)

    s = jnp.dot(a_ref[...], z_scr[...], preferred_element_type=F32)
    half = s.shape[0] // 2
    _cmul_store(o_ref, s, f_ref[:half, :], f_ref[half:, :])


def _dft_fwd(a_mat, z, z_col0, fspec, order, b, l, w_hy):
    n2 = a_mat.shape[0]
    tm = _dft_tile(l)
    tn = _pick(w_hy, (1024, 512, 256, 128))
    assert z_col0 % tn == 0
    return pl.pallas_call(
        _dft_fwd_kernel,
        grid=(b, w_hy // tn, n2 // tm),
        in_specs=[
            pl.BlockSpec((tm, l), lambda bi, j, i: (i, 0)),
            pl.BlockSpec((l, tn), lambda bi, j, i: (bi, z_col0 // tn + j)),
            pl.BlockSpec((None, tm, tn), lambda bi, j, i: (order, i, j)),
        ],
        out_specs=pl.BlockSpec((None, tm, tn), lambda bi, j, i: (bi, i, j)),
        out_shape=jax.ShapeDtypeStruct((b, n2, w_hy), BF16),
        scratch_shapes=[pltpu.VMEM((l, tn), BF16)],
        compiler_params=_params(("arbitrary", "arbitrary", "arbitrary")),
        name="hyena_dft_fwd",
    )(a_mat, z, fspec)


def _dft_inv_kernel(at_ref, y_ref, z_ref, g_ref, d_ref, o_ref, *, inv_l):
    y = jnp.dot(at_ref[...], y_ref[...], preferred_element_type=F32) * inv_l
    z = z_ref[...]
    o_ref[...] = (g_ref[...] * (y + d_ref[...] * z)).astype(o_ref.dtype)


def _dft_inv(at_mat, y, z, z_col0, gate, gate_col0, d, layer, order, b, l, w_hy, out_dtype):
    n2 = at_mat.shape[1]
    tm = _pick(l, (512, 256, 128, 64))
    tn = _pick(w_hy, (1024, 512, 256, 128))
    nt = l // tm
    assert z_col0 % tn == 0 and gate_col0 % tn == 0
    return pl.pallas_call(
        functools.partial(_dft_inv_kernel, inv_l=1.0 / l),
        grid=(b, w_hy // tn, nt),
        in_specs=[
            pl.BlockSpec((tm, n2), lambda bi, j, i: (i, 0)),
            pl.BlockSpec((None, n2, tn), lambda bi, j, i: (bi, 0, j)),
            pl.BlockSpec((tm, tn), lambda bi, j, i: (bi * nt + i, z_col0 // tn + j)),
            pl.BlockSpec((tm, tn), lambda bi, j, i: (bi * nt + i, gate_col0 // tn + j)),
            pl.BlockSpec((None, None, 1, tn), lambda bi, j, i: (layer, order, 0, j)),
        ],
        out_specs=pl.BlockSpec((tm, tn), lambda bi, j, i: (bi * nt + i, j)),
        out_shape=jax.ShapeDtypeStruct((b * l, w_hy), out_dtype),
        compiler_params=_params(("arbitrary", "arbitrary", "arbitrary")),
        name="hyena_dft_inv",
    )(at_mat, y, z, gate, d)


def _hyena_fused_kernel(*refs, n_order, pad_left, tile, inv_l):
    it = iter(refs)
    a_ref, at_ref, f_ref, d_ref = next(it), next(it), next(it), next(it)
    x_refs = [next(it) for _ in range(n_order + 1)]
    w_refs = [next(it) for _ in range(n_order + 1)]
    b_refs = [next(it) for _ in range(n_order + 1)]
    o_ref = next(it)
    half = tile // 2
    n_tiles = a_ref.shape[0] // tile
    conv = lambda n: _dwconv(x_refs[n][...], w_refs[n][...], b_refs[n][...], pad_left)
    z = conv(0)
    for o in range(n_order):
        zb = z.astype(BF16)
        y = None
        for ti in range(n_tiles):
            r0 = ti * tile
            s = jnp.dot(a_ref[r0:r0 + tile, :], zb, preferred_element_type=F32)
            s_re, s_im = s[:half], s[half:]
            f_re, f_im = f_ref[o, r0:r0 + half, :], f_ref[o, r0 + half:r0 + tile, :]
            spec = jnp.concatenate([s_re * f_re - s_im * f_im, s_re * f_im + s_im * f_re], axis=0).astype(BF16)
            t = jnp.dot(at_ref[:, r0:r0 + tile], spec, preferred_element_type=F32)
            y = t if y is None else y + t
        z = conv(o + 1) * (y * inv_l + d_ref[o] * z)
    o_ref[...] = z.astype(o_ref.dtype)


def _hyena_fused(proj, tok0, b, l, col0, w_hy, n_order, a_mat, at_mat, fspec, conv_w, conv_b, hy_d, layer):
    n2 = a_mat.shape[0]
    tn = _pick(w_hy, (1024, 512, 256, 128))
    nj = w_hy // tn
    k = conv_w.shape[1]
    assert tok0 % l == 0 and col0 % tn == 0
    col = lambda n: (col0 + n * w_hy) // tn
    x_specs = [pl.BlockSpec((l, tn), lambda j, bi, n=n: (tok0 // l + bi, col(n) + j)) for n in range(n_order + 1)]
    w_specs = [pl.BlockSpec((None, k, tn), lambda j, bi, n=n: (layer, 0, n * nj + j)) for n in range(n_order + 1)]
    b_specs = [pl.BlockSpec((None, 1, tn), lambda j, bi, n=n: (layer, 0, n * nj + j)) for n in range(n_order + 1)]
    return pl.pallas_call(
        functools.partial(_hyena_fused_kernel, n_order=n_order, pad_left=k // 2, tile=_dft_tile(l), inv_l=1.0 / l),
        grid=(nj, b),
        in_specs=[
            pl.BlockSpec((n2, l), lambda j, bi: (0, 0)),
            pl.BlockSpec((l, n2), lambda j, bi: (0, 0)),
            pl.BlockSpec((n_order, n2, tn), lambda j, bi: (0, 0, j)),
            pl.BlockSpec((None, n_order, 1, tn), lambda j, bi: (layer, 0, 0, j)),
            *x_specs, *w_specs, *b_specs,
        ],
        out_specs=pl.BlockSpec((l, tn), lambda j, bi: (bi, j)),
        out_shape=jax.ShapeDtypeStruct((b * l, w_hy), BF16),
        compiler_params=_params(("arbitrary", "arbitrary")),
        name="hyena_fused",
    )(a_mat, at_mat, fspec, hy_d, *([proj] * (n_order + 1)), *([conv_w] * (n_order + 1)),
      *([conv_b] * (n_order + 1)))


HYENA_FUSED_MAX_L = 512


def _hyena(proj, tok0, b, l, col0, w_hy, n_order, a_mat, at_mat, fspec, conv_w, conv_b, hy_d, layer):
    if l <= HYENA_FUSED_MAX_L:
        return _hyena_fused(proj, tok0, b, l, col0, w_hy, n_order, a_mat, at_mat, fspec, conv_w, conv_b, hy_d, layer)
    hyc = _hy_conv(proj, tok0, b, l, col0, (n_order + 1) * w_hy, conv_w, conv_b, layer)
    z, z_col0 = hyc, 0
    for o in range(n_order):
        y = _dft_fwd(a_mat, z, z_col0, fspec, o, b, l, w_hy)
        last = o == n_order - 1
        z = _dft_inv(at_mat, y, z, z_col0, hyc, (o + 1) * w_hy, hy_d, layer, o, b, l, w_hy, BF16 if last else F32)
        z_col0 = 0
    return z


def _merge_kernel(h_ref, wg0_ref, wg1_ref, wg2_ref, b0_ref, b1_ref, b2_ref, wb0_ref, wb1_ref, wb2_ref, o_ref):
    h = h_ref[...]
    acc = None
    for wg_ref, b_ref, wb_ref in ((wg0_ref, b0_ref, wb0_ref), (wg1_ref, b1_ref, wb1_ref), (wg2_ref, b2_ref, wb2_ref)):
        gate = jax.nn.sigmoid(jnp.dot(h, wg_ref[...], preferred_element_type=F32))
        t = gate * jnp.dot(b_ref[...], wb_ref[...], preferred_element_type=F32)
        acc = t if acc is None else acc + t
    o_ref[...] = acc.astype(o_ref.dtype)


def _merge(h, w_gate, branches, w_br, layer):
    m, d = h.shape
    wb = w_br.shape[2]
    tm = _pick(m, (1024, 512, 256, 128))
    tn = _pick(d, (512, 256, 128))
    nj = d // tn
    gate_spec = lambda n: pl.BlockSpec((None, d, tn), lambda i, j: (layer, 0, n * nj + j))
    br_spec = pl.BlockSpec((tm, wb), lambda i, j: (i, 0))
    wbr_spec = lambda n: pl.BlockSpec((None, None, wb, tn), lambda i, j: (layer, n, 0, j))
    return pl.pallas_call(
        _merge_kernel,
        grid=(m // tm, nj),
        in_specs=[
            pl.BlockSpec((tm, d), lambda i, j: (i, 0)),
            gate_spec(0), gate_spec(1), gate_spec(2),
            br_spec, br_spec, br_spec,
            wbr_spec(0), wbr_spec(1), wbr_spec(2),
        ],
        out_specs=pl.BlockSpec((tm, tn), lambda i, j: (i, j)),
        out_shape=jax.ShapeDtypeStruct((m, d), BF16),
        compiler_params=_params(("arbitrary", "arbitrary")),
        name="gated_merge",
    )(h, w_gate, w_gate, w_gate, *branches, w_br, w_br, w_br)


def _proj_residual_kernel(a_ref, w_ref, x_ref, mod_ref, o_ref, *, gate_row):
    gate = mod_ref[gate_row:gate_row + 1, :]
    o_ref[...] = x_ref[...] + gate * jnp.dot(a_ref[...], w_ref[...], preferred_element_type=F32)


def _proj_residual(a, w, x, mod, gate_row, layer, rpm):
    m, k = a.shape
    d = x.shape[1]
    tm = _row_tile(rpm)
    tn = _pick(d, (1024, 512, 256, 128) if k <= 2048 else (512, 256, 128))
    return pl.pallas_call(
        functools.partial(_proj_residual_kernel, gate_row=gate_row),
        grid=(m // tm, d // tn),
        in_specs=[
            pl.BlockSpec((tm, k), lambda i, j: (i, 0)),
            pl.BlockSpec((None, k, tn), lambda i, j: (layer, 0, j)),
            pl.BlockSpec((tm, tn), lambda i, j: (i, j)),
            pl.BlockSpec((None, 6, tn), lambda i, j: (_mod_row(i, tm, rpm), 0, j)),
        ],
        out_specs=pl.BlockSpec((tm, tn), lambda i, j: (i, j)),
        out_shape=jax.ShapeDtypeStruct((m, d), F32),
        compiler_params=_params(("arbitrary", "arbitrary")),
        name="proj_residual",
    )(a, w, x, mod)


def _ffn_up_kernel(x_ref, mod_ref, g_ref, wg_ref, wu_ref, o_ref, h_scr):
    @pl.when(pl.program_id(1) == 0)
    def _():
        _norm_tile(x_ref, mod_ref, g_ref, h_scr, 4, 3)

    h = h_scr[...]
    a = jnp.dot(h, wg_ref[...], preferred_element_type=F32)
    u = jnp.dot(h, wu_ref[...], preferred_element_type=F32)
    o_ref[...] = ((a * jax.nn.sigmoid(a)) * u).astype(o_ref.dtype)


def _ffn_up(x, mod, g, w_gate, w_up, layer, rpm):
    m, d = x.shape
    n = w_gate.shape[-1]
    tm = _row_tile(rpm)
    tn = _pick(n, (512, 256, 128))
    w_spec = pl.BlockSpec((None, d, tn), lambda i, j: (layer, 0, j))
    return pl.pallas_call(
        _ffn_up_kernel,
        grid=(m // tm, n // tn),
        in_specs=[
            pl.BlockSpec((tm, d), lambda i, j: (i, 0)),
            pl.BlockSpec((None, 6, d), lambda i, j: (_mod_row(i, tm, rpm), 0, 0)),
            pl.BlockSpec((None, 1, d), lambda i, j: (layer, 0, 0)),
            w_spec, w_spec,
        ],
        out_specs=pl.BlockSpec((tm, tn), lambda i, j: (i, j)),
        out_shape=jax.ShapeDtypeStruct((m, n), BF16),
        scratch_shapes=[pltpu.VMEM((tm, d), BF16)],
        compiler_params=_params(("arbitrary", "arbitrary")),
        name="ffn_up",
    )(x, mod, g, w_gate, w_up)


def _final_norm_kernel(x_ref, g_ref, o_ref):
    x = x_ref[...]
    o_ref[...] = (x * lax.rsqrt(jnp.mean(x * x, axis=-1, keepdims=True) + NORM_EPS)) * g_ref[...]


def _final_norm(x, tok0, n_tok, g):
    d = x.shape[1]
    tm = _pick(math.gcd(tok0, n_tok) if tok0 else n_tok, (512, 256, 128, 64))
    return pl.pallas_call(
        _final_norm_kernel,
        grid=(n_tok // tm,),
        in_specs=[pl.BlockSpec((tm, d), lambda i: (tok0 // tm + i, 0)), pl.BlockSpec((1, d), lambda i: (0, 0))],
        out_specs=pl.BlockSpec((tm, d), lambda i: (i, 0)),
        out_shape=jax.ShapeDtypeStruct((n_tok, d), F32),
        compiler_params=_params(("arbitrary",)),
        name="final_norm",
    )(x, g)


def kernel(x_prompt, x_sample, c, cache_k, cache_v, state_lru, c_ctx, w_mod, b_mod, norm_mix, norm_ffn, w_in, w_gate, att_lambda, att_subln, lru_conv_w, lru_conv_b, lru_wa, lru_ba, lru_wx, lru_bx, lru_lambda, hy_conv_w, hy_conv_b, hy_w1, hy_b1, hy_w2, hy_b2, hy_w3, hy_b3, hy_freq, hy_decay, hy_d, w_br, w_out, w_ff_gate, w_ff_up, w_ff_down, final_norm):
    b_c, l_c, d = x_prompt.shape
    b_s, l_s, _ = x_sample.shape
    depth = w_in.shape[0]
    n_heads, d_v = cache_v.shape[3], cache_v.shape[4]
    w_att = n_heads * d_v
    w_lru = lru_lambda.shape[-1]
    n_order, w_hy = hy_d.shape[1], hy_d.shape[2]
    m_ctx, m_s = b_c * l_c, b_s * l_s
    past = cache_k.shape[2]
    col_xl, col_gl, col_hy = 3 * w_att, 3 * w_att + w_lru, 3 * w_att + 2 * w_lru

    w_in_b, w_gate_b, w_br_b, w_out_b = (w.astype(BF16) for w in (w_in, w_gate, w_br, w_out))
    w_ffg_b, w_ffu_b, w_ffd_b = (w.astype(BF16) for w in (w_ff_gate, w_ff_up, w_ff_down))
    lru_wg = _lru_gate_weights(lru_wa, lru_wx)
    lru_bg = 0.5 * jnp.stack([lru_ba[:, 0], lru_bx[:, 0], lru_ba[:, 1], lru_bx[:, 1]], axis=1)
    norm_mix3, norm_ffn3 = norm_mix.reshape(depth, 1, d), norm_ffn.reshape(depth, 1, d)
    att_subln3 = att_subln.reshape(depth, 1, d_v)
    lru_conv_b3 = lru_conv_b.reshape(depth, 1, w_lru)
    hy_conv_b3 = hy_conv_b.reshape(depth, 1, -1)
    hy_d4 = hy_d.reshape(depth, n_order, 1, w_hy)

    n_rows = 1 + b_s
    assert n_rows <= SUBLANES
    c8 = jnp.zeros((SUBLANES, d), F32).at[0].set(c_ctx).at[1:n_rows].set(c)
    mod_all = _modulation(c8, w_mod, b_mod)[:, :n_rows].reshape(depth, n_rows, 6, d)

    rope = _rope_tables(l_s, d_v // 2)
    dft = {}
    for l in sorted({l_c, l_s}):
        a_mat = jnp.asarray(_dft_matrices(l)).astype(BF16)
        dft[l] = (a_mat, a_mat.T)
    h0_ctx = jnp.zeros((b_c, 2, w_lru), F32)

    def trunk_layer(x, mod, rpm, layer, b, l, rope, cache, h0, prev_kv):
        lam_init = 0.8 - 0.6 * math.exp(-0.3 * layer)
        proj, h = _norm_proj(x, mod, norm_mix3, w_in_b, layer, rpm)
        att, *kv = _attention(proj, b, l, n_heads, d_v, lam_init, att_lambda, att_subln3, layer, rope=rope,
                              cache=cache, emit_kv=prev_kv is not None,
                              prev_kv=prev_kv if prev_kv and layer == depth - 1 else ())
        lru, st = _lru(proj, 0, b, l, col_xl, col_gl, w_lru, lru_conv_w, lru_conv_b3, lru_wg, lru_bg, lru_lambda,
                       h0, layer)
        a_mat, at_mat = dft[l]
        filt = _hy_filters(l, hy_w1[layer], hy_b1[layer], hy_freq[layer], hy_w2[layer], hy_b2[layer],
                           hy_w3[layer], hy_b3[layer], hy_decay[layer])
        fspec = _filter_spectrum(a_mat, filt, n_order, w_hy)
        hyo = _hyena(proj, 0, b, l, col_hy, w_hy, n_order, a_mat, at_mat, fspec, hy_conv_w, hy_conv_b3, hy_d4, layer)
        mixed = _merge(h, w_gate_b, (att, lru, hyo), w_br_b, layer)
        x = _proj_residual(mixed, w_out_b, x, mod, 2, layer, rpm)
        u = _ffn_up(x, mod, norm_ffn3, w_ffg_b, w_ffu_b, layer, rpm)
        x = _proj_residual(u, w_ffd_b, x, mod, 5, layer, rpm)
        return x, kv, st

    xp = x_prompt.reshape(m_ctx, d)
    kvs, ss = [], []
    for layer in range(depth):
        xp, kv_l, s_l = trunk_layer(xp, mod_all[layer, :1], m_ctx, layer, b_c, l_c, None, None, h0_ctx, kvs)
        kvs.append(tuple(kv_l))
        ss.append(s_l)
    new_k, new_v = (a.reshape(b_c, depth, l_c, n_heads, d_v) for a in kvs[-1])

    xs = x_sample.reshape(m_s, d)
    for layer in range(depth):
        ck = cache_k[:, layer].reshape(b_s, past, w_att)
        cv = cache_v[:, layer].reshape(b_s, past, w_att)
        xs, _, _ = trunk_layer(xs, mod_all[layer, 1:], l_s, layer, b_s, l_s, rope, (ck, cv), state_lru[:, layer], None)

    g_fin = final_norm.reshape(1, d)
    y_prompt = _final_norm(xp, 0, m_ctx, g_fin).reshape(b_c, l_c, d)
    y_sample = _final_norm(xs, 0, m_s, g_fin).reshape(b_s, l_s, d)
    return (y_prompt, y_sample, new_k, new_v, jnp.stack(ss, axis=1))
```
